```python
import jax, jax.numpy as jnp
from jax import lax
import numpy as np

D_MODEL = 1024
BATCH = 16
SEQ = 4096
DEPTH = 1
DEC_BATCH = 16
DEC_SEQ = 16
PAST_LEN = 1024

CHUNK = 64
N_META = 16
W_SHORT = 512
W_LRU = 512
N_LRU_HEADS = 8
LRU_HEAD_DIM = W_LRU // N_LRU_HEADS
SHORT_CONV_W = 3
LRU_CONV_W = 4
RG_C = 8.0
D_IN_PROJ = 3 * W_SHORT + 2 * W_LRU
N_EXPERTS = 32
TOP_K = 4
D_FF = 1024
SWIGLU_LIMIT = 7.0
SWIGLU_ALPHA = 1.702
MOE_BLOCK = 128
EPS = 1e-6

kernel_name = "hymba_conv_rglru_moe_stream_step"


def rmsnorm(x, g):
    xf = x.astype(jnp.float32)
    y = xf * lax.rsqrt(jnp.mean(xf * xf, axis=-1, keepdims=True) + EPS)
    return (y * g.astype(jnp.float32)).astype(x.dtype)


def causal_dwconv(x, buf, w):
    width = w.shape[0]
    L = x.shape[1]
    xp = jnp.concatenate([buf.astype(x.dtype), x], axis=1)
    y = xp[:, 0:L] * w[0]
    for k in range(1, width):
        y = y + xp[:, k:k + L] * w[k]
    return y, xp[:, xp.shape[1] - (width - 1):]


def rg_lru(xc, h0, wa, ba, wx, bx, lam):
    B, L, _ = xc.shape
    xh = xc.reshape(B, L, N_LRU_HEADS, LRU_HEAD_DIM)
    ga = jnp.einsum('blhi,hij->blhj', xh, wa).reshape(B, L, W_LRU) + ba
    gx = jnp.einsum('blhi,hij->blhj', xh, wx).reshape(B, L, W_LRU) + bx
    r = jax.nn.sigmoid(ga.astype(jnp.float32))
    i = jax.nn.sigmoid(gx.astype(jnp.float32))
    log_a = -RG_C * r * jax.nn.softplus(-lam.astype(jnp.float32))
    a = jnp.exp(log_a)
    u = jnp.sqrt(-jnp.expm1(2.0 * log_a)) * (i * xc.astype(jnp.float32))
    u = u.at[:, 0].add(a[:, 0] * h0.astype(jnp.float32))

    def combine(left, right):
        a1, b1 = left
        a2, b2 = right
        return a1 * a2, a2 * b1 + b2

    _, h = lax.associative_scan(combine, (a, u), axis=1)
    return h.astype(xc.dtype), h[:, -1].astype(h0.dtype)


def moe(x, w_router, b_router, w_gu, b_gu, w_down, b_down):
    lead = x.shape[:-1]
    xf = x.reshape(-1, D_MODEL)
    T = xf.shape[0]
    logits = xf.astype(jnp.float32) @ w_router.astype(jnp.float32) + b_router.astype(jnp.float32)
    top_val, top_idx = lax.top_k(logits, TOP_K)
    gates = jax.nn.softmax(top_val, axis=-1)
    n_assign = T * TOP_K
    flat_e = top_idx.reshape(-1)
    order = jnp.argsort(flat_e, stable=True)
    sorted_e = flat_e[order]
    counts = jnp.bincount(flat_e, length=N_EXPERTS)
    padded = (counts + MOE_BLOCK - 1) // MOE_BLOCK * MOE_BLOCK
    seg_start = jnp.cumsum(counts) - counts
    pad_end = jnp.cumsum(padded)
    pad_start = pad_end - padded
    dest = pad_start[sorted_e] + jnp.arange(n_assign) - seg_start[sorted_e]
    n_blocks = -(-(n_assign + N_EXPERTS * (MOE_BLOCK - 1)) // MOE_BLOCK)
    n_rows = n_blocks * MOE_BLOCK
    row_tok = jnp.zeros((n_rows,), jnp.int32).at[dest].set((order // TOP_K).astype(jnp.int32))
    row_w = jnp.zeros((n_rows,), jnp.float32).at[dest].set(gates.reshape(-1)[order])
    block_e = jnp.minimum(
        jnp.searchsorted(pad_end, jnp.arange(n_blocks) * MOE_BLOCK, side='right'), N_EXPERTS - 1)
    xs = xf[row_tok].reshape(n_blocks, MOE_BLOCK, D_MODEL)

    def expert_block(args):
        xb, e = args
        gu = xb @ w_gu[e] + b_gu[e]
        gate, up = gu[:, :D_FF], gu[:, D_FF:]
        gate = jnp.minimum(gate, SWIGLU_LIMIT)
        up = jnp.clip(up, -SWIGLU_LIMIT, SWIGLU_LIMIT)
        glu = gate * jax.nn.sigmoid(gate * SWIGLU_ALPHA)
        return ((up + 1.0) * glu) @ w_down[e] + b_down[e]

    ys = lax.map(expert_block, (xs, block_e)).reshape(n_rows, D_MODEL)
    out = jnp.zeros((T, D_MODEL), jnp.float32).at[row_tok].add(ys.astype(jnp.float32) * row_w[:, None])
    return out.astype(x.dtype).reshape(lead + (D_MODEL,))


def layer(h, cs_buf, cl_buf, h0, g_mix, w_in, cs_w, cl_w, cl_b, wa, ba, wx, bx, lam, w_out,
          g_ffn, w_r, b_r, w_gu, b_gu, w_d, b_d):
    u = rmsnorm(h, g_mix)
    proj = u @ w_in
    b_g, c_g, v, lx, ly = jnp.split(
        proj, [W_SHORT, 2 * W_SHORT, 3 * W_SHORT, 3 * W_SHORT + W_LRU], axis=-1)
    conv_a, new_cs = causal_dwconv(c_g * v, cs_buf, cs_w)
    out_a = b_g * conv_a
    xc, new_cl = causal_dwconv(lx, cl_buf, cl_w)
    hs, h_last = rg_lru(xc + cl_b, h0, wa, ba, wx, bx, lam)
    out_b = hs * jax.nn.gelu(ly)
    h = h + jnp.concatenate([out_a, out_b], axis=-1) @ w_out
    h = h + moe(rmsnorm(h, g_ffn), w_r, b_r, w_gu, b_gu, w_d, b_d)
    return h, new_cs, new_cl, h_last


def setup_inputs(seed: int = 0) -> dict:
    key = jax.random.key(seed)
    ks = jax.random.split(key, 26)
    f32 = jnp.float32
    nrm = lambda k, s, sc: jax.random.normal(k, s, f32) * sc
    u_a = jax.random.uniform(ks[12], (DEPTH, W_LRU), f32, 0.9, 0.999)
    a_base = u_a ** (1.0 / RG_C)
    rg_lambda = jnp.log(a_base) - jnp.log1p(-a_base)
    return {
        "x_prompt": nrm(ks[0], (BATCH, SEQ, D_MODEL), 1.0),
        "x_sample": nrm(ks[1], (DEC_BATCH, DEC_SEQ, D_MODEL), 1.0),
        "state_conv_short": nrm(ks[2], (DEPTH, DEC_BATCH, SHORT_CONV_W - 1, W_SHORT), 1.0),
        "state_conv_lru": nrm(ks[3], (DEPTH, DEC_BATCH, LRU_CONV_W - 1, W_LRU), 1.0),
        "state_lru_h": nrm(ks[4], (DEPTH, DEC_BATCH, W_LRU), 0.5),
        "meta_tokens": nrm(ks[5], (N_META, D_MODEL), 1.0),
        "norm_mix_g": 1.0 + nrm(ks[6], (DEPTH, D_MODEL), 0.02),
        "w_in": nrm(ks[7], (DEPTH, D_MODEL, D_IN_PROJ), D_MODEL ** -0.5),
        "conv_short_w": nrm(ks[8], (DEPTH, SHORT_CONV_W, W_SHORT), SHORT_CONV_W ** -0.5),
        "conv_lru_w": nrm(ks[9], (DEPTH, LRU_CONV_W, W_LRU), LRU_CONV_W ** -0.5),
        "conv_lru_b": nrm(ks[10], (DEPTH, W_LRU), 0.01),
        "w_rg_a": nrm(ks[11], (DEPTH, N_LRU_HEADS, LRU_HEAD_DIM, LRU_HEAD_DIM), LRU_HEAD_DIM ** -0.5),
        "b_rg_a": nrm(ks[13], (DEPTH, W_LRU), 0.01),
        "w_rg_x": nrm(ks[14], (DEPTH, N_LRU_HEADS, LRU_HEAD_DIM, LRU_HEAD_DIM), LRU_HEAD_DIM ** -0.5),
        "b_rg_x": nrm(ks[15], (DEPTH, W_LRU), 0.01),
        "rg_lambda": rg_lambda,
        "w_out": nrm(ks[16], (DEPTH, W_SHORT + W_LRU, D_MODEL), (W_SHORT + W_LRU) ** -0.5),
        "norm_ffn_g": 1.0 + nrm(ks[17], (DEPTH, D_MODEL), 0.02),
        "w_router": nrm(ks[18], (DEPTH, D_MODEL, N_EXPERTS), D_MODEL ** -0.5),
        "b_router": nrm(ks[19], (DEPTH, N_EXPERTS), 0.01),
        "w_gate_up": nrm(ks[20], (DEPTH, N_EXPERTS, D_MODEL, 2 * D_FF), D_MODEL ** -0.5),
        "b_gate_up": nrm(ks[21], (DEPTH, N_EXPERTS, 2 * D_FF), 0.01),
        "w_down": nrm(ks[22], (DEPTH, N_EXPERTS, D_FF, D_MODEL), D_FF ** -0.5),
        "b_down": nrm(ks[23], (DEPTH, N_EXPERTS, D_MODEL), 0.01),
        "final_norm_g": 1.0 + nrm(ks[24], (D_MODEL,), 0.02),
    }


def reference(x_prompt, x_sample, state_conv_short, state_conv_lru, state_lru_h, meta_tokens,
              norm_mix_g, w_in, conv_short_w, conv_lru_w, conv_lru_b, w_rg_a, b_rg_a, w_rg_x, b_rg_x,
              rg_lambda, w_out, norm_ffn_g, w_router, b_router, w_gate_up, b_gate_up, w_down, b_down,
              final_norm_g):
    bp = x_prompt.shape[0]
    dt = x_prompt.dtype
    meta = jnp.broadcast_to(meta_tokens[None].astype(dt), (bp, N_META, D_MODEL))
    hp = jnp.concatenate([meta, x_prompt], axis=1)
    hs = x_sample
    zero_cs = jnp.zeros((bp, SHORT_CONV_W - 1, W_SHORT), dt)
    zero_cl = jnp.zeros((bp, LRU_CONV_W - 1, W_LRU), dt)
    zero_h = jnp.zeros((bp, W_LRU), state_lru_h.dtype)
    cs_p, cl_p, h_p, cs_s, cl_s, h_s = [], [], [], [], [], []
    for l in range(DEPTH):
        params = (norm_mix_g[l], w_in[l], conv_short_w[l], conv_lru_w[l], conv_lru_b[l],
                  w_rg_a[l], b_rg_a[l], w_rg_x[l], b_rg_x[l], rg_lambda[l], w_out[l],
                  norm_ffn_g[l], w_router[l], b_router[l], w_gate_up[l], b_gate_up[l],
                  w_down[l], b_down[l])
        hp, a1, a2, a3 = layer(hp, zero_cs, zero_cl, zero_h, *params)
        hs, b1, b2, b3 = layer(hs, state_conv_short[l], state_conv_lru[l], state_lru_h[l], *params)
        cs_p.append(a1); cl_p.append(a2); h_p.append(a3)
        cs_s.append(b1); cl_s.append(b2); h_s.append(b3)
    y_prompt = rmsnorm(hp, final_norm_g)[:, N_META:]
    y_sample = rmsnorm(hs, final_norm_g)
    return (y_prompt, y_sample, jnp.stack(cs_p), jnp.stack(cl_p), jnp.stack(h_p),
            jnp.stack(cs_s), jnp.stack(cl_s), jnp.stack(h_s))
```

```python
import functools

import jax
import jax.numpy as jnp
from jax import lax
from jax.experimental import pallas as pl
from jax.experimental.pallas import tpu as pltpu

D_MODEL = 1024
W_SHORT = 512
W_LRU = 512
N_LRU_HEADS = 8
LRU_HEAD_DIM = W_LRU // N_LRU_HEADS
D_IN_PROJ = 3 * W_SHORT + 2 * W_LRU
N_EXPERTS = 32
TOP_K = 4
D_FF = 1024
RG_C = 8.0
SWIGLU_LIMIT = 7.0
SWIGLU_ALPHA = 1.702
EPS = 1e-6

SUBLANES = 8
MIX_ROWS = 256
TOK_TILE = 256
EXP_ROWS = 512
VMEM_LIMIT = 56 * 1024 * 1024

f32 = jnp.float32
bf16 = jnp.bfloat16


def _rmsnorm(x, g):
    return (x * lax.rsqrt(jnp.mean(x * x, axis=-1, keepdims=True) + EPS)) * g


def _mixer_kernel(x_ref, cs0_ref, cl0_ref, h0_ref, g_ref, win_ref, csw_ref, clw_ref, clb_ref,
                  wg_ref, ba_ref, bx_ref, lam_ref, wout_ref,
                  h1_ref, ncs_ref, ncl_ref, nh_ref,
                  cvbuf, lxbuf, hcar, *, tl):
    j = pl.program_id(1)
    hd = SUBLANES

    @pl.when(j == 0)
    def _():
        cvbuf[0:hd, :] = jnp.zeros((hd, W_SHORT), f32)
        cvbuf[hd - 2:hd, :] = cs0_ref[0]
        lxbuf[0:hd, :] = jnp.zeros((hd, W_LRU), f32)
        lxbuf[hd - 3:hd, :] = cl0_ref[0]
        hcar[...] = h0_ref[0]

    x = x_ref[0]
    u = _rmsnorm(x, g_ref[...])
    proj = jnp.dot(u.astype(bf16), win_ref[...], preferred_element_type=f32)
    b_g = proj[:, 0:W_SHORT]
    c_g = proj[:, W_SHORT:2 * W_SHORT]
    v = proj[:, 2 * W_SHORT:3 * W_SHORT]
    lx = proj[:, 3 * W_SHORT:3 * W_SHORT + W_LRU]
    ly = proj[:, 3 * W_SHORT + W_LRU:]

    cv = c_g * v
    cvbuf[hd:hd + tl, :] = cv
    conv_a = (cvbuf[hd - 2:hd - 2 + tl, :] * csw_ref[0:1, :]
              + cvbuf[hd - 1:hd - 1 + tl, :] * csw_ref[1:2, :]
              + cv * csw_ref[2:3, :])
    out_a = b_g * conv_a

    lxbuf[hd:hd + tl, :] = lx
    xc = (lxbuf[hd - 3:hd - 3 + tl, :] * clw_ref[0:1, :]
          + lxbuf[hd - 2:hd - 2 + tl, :] * clw_ref[1:2, :]
          + lxbuf[hd - 1:hd - 1 + tl, :] * clw_ref[2:3, :]
          + lx * clw_ref[3:4, :]) + clb_ref[...]
    xcb = xc.astype(bf16)
    half = W_LRU // 2
    g0 = jnp.dot(xcb[:, :half], wg_ref[0], preferred_element_type=f32)
    g1 = jnp.dot(xcb[:, half:], wg_ref[1], preferred_element_type=f32)
    ga = jnp.concatenate([g0[:, :half], g1[:, :half]], axis=1) + ba_ref[...]
    gx = jnp.concatenate([g0[:, half:], g1[:, half:]], axis=1) + bx_ref[...]
    r = jax.nn.sigmoid(ga)
    ig = jax.nn.sigmoid(gx)
    z = -lam_ref[...]
    softplus = jnp.maximum(z, 0.0) + jnp.log1p(jnp.exp(-jnp.abs(z)))
    log_a = (-RG_C * r) * softplus
    a = jnp.exp(log_a)
    th = jnp.tanh(log_a)
    uu = jnp.sqrt((-2.0 * th) / (1.0 - th)) * (ig * xc)

    row = lax.broadcasted_iota(jnp.int32, (tl, W_LRU), 0)
    uu = uu + jnp.where(row == 0, a * hcar[...], 0.0)
    aa, hh = a, uu
    s = 1
    while s < tl:
        h_sh = jnp.where(row >= s, pltpu.roll(hh, s, 0), 0.0)
        hh = aa * h_sh + hh
        if 2 * s < tl:
            a_sh = jnp.where(row >= s, pltpu.roll(aa, s, 0), 1.0)
            aa = aa * a_sh
        s *= 2
    hcar[...] = hh[tl - 1:tl, :]

    gelu = ly * (0.5 * (1.0 + jnp.tanh(0.7978845608028654 * (ly + 0.044715 * (ly * ly * ly)))))
    out_b = hh * gelu
    y = jnp.concatenate([out_a, out_b], axis=1).astype(bf16)
    h1_ref[0] = x + jnp.dot(y, wout_ref[...], preferred_element_type=f32)

    @pl.when(j == pl.num_programs(1) - 1)
    def _():
        ncs_ref[0] = cvbuf[hd + tl - 2:hd + tl, :]
        ncl_ref[0] = lxbuf[hd + tl - 3:hd + tl, :]
        nh_ref[0] = hh[tl - 1:tl, :]

    cvbuf[0:hd, :] = cvbuf[tl:tl + hd, :]
    lxbuf[0:hd, :] = lxbuf[tl:tl + hd, :]


def _mixer(x, cs0, cl0, h0, mw, tl):
    nb, seq, _ = x.shape
    assert seq % tl == 0 and tl % SUBLANES == 0
    full = lambda a: pl.BlockSpec(a.shape, lambda b, j: (0,) * a.ndim)
    per_b = lambda a: pl.BlockSpec((1,) + a.shape[1:], lambda b, j: (b,) + (0,) * (a.ndim - 1))
    weights = (mw["g"], mw["w_in"], mw["csw"], mw["clw"], mw["clb"], mw["wg"], mw["ba"], mw["bx"],
               mw["lam"], mw["w_out"])
    return pl.pallas_call(
        functools.partial(_mixer_kernel, tl=tl),
        grid=(nb, seq // tl),
        in_specs=[pl.BlockSpec((1, tl, D_MODEL), lambda b, j: (b, j, 0)),
                  per_b(cs0), per_b(cl0), per_b(h0)] + [full(w) for w in weights],
        out_specs=[pl.BlockSpec((1, tl, D_MODEL), lambda b, j: (b, j, 0)),
                   per_b(cs0), per_b(cl0), per_b(h0)],
        out_shape=[jax.ShapeDtypeStruct(x.shape, f32),
                   jax.ShapeDtypeStruct(cs0.shape, f32),
                   jax.ShapeDtypeStruct(cl0.shape, f32),
                   jax.ShapeDtypeStruct(h0.shape, f32)],
        scratch_shapes=[pltpu.VMEM((tl + SUBLANES, W_SHORT), f32),
                        pltpu.VMEM((tl + SUBLANES, W_LRU), f32),
                        pltpu.VMEM((1, W_LRU), f32)],
        compiler_params=pltpu.CompilerParams(
            dimension_semantics=("arbitrary", "arbitrary"), vmem_limit_bytes=VMEM_LIMIT),
        name=f"mixer_{seq}",
    )(x, cs0, cl0, h0, *weights)


def _router_kernel(hp_ref, hs_ref, g_ref, wrt_ref, br_ref,
                   xn_ref, idx_ref, rank_ref, gate_ref, cnt_ref, carry, *, n_prompt_tiles):
    i = pl.program_id(0)
    tr = hp_ref.shape[0]

    @pl.when(i == 0)
    def _():
        carry[...] = jnp.zeros(carry.shape, f32)

    h = jnp.where(i < n_prompt_tiles, hp_ref[...], hs_ref[...])
    xn = _rmsnorm(h, g_ref[...])
    xn_ref[...] = xn
    logits = lax.dot_general(wrt_ref[...], xn, (((1,), (1,)), ((), ())),
                             precision=lax.Precision.HIGHEST,
                             preferred_element_type=f32) + br_ref[...]

    rows = lax.broadcasted_iota(jnp.int32, (N_EXPERTS, tr), 0)
    vals, idxs = [], []
    cur = logits
    for _ in range(TOP_K):
        m = jnp.max(cur, axis=0, keepdims=True)
        ik = jnp.min(jnp.where(cur == m, rows, N_EXPERTS), axis=0, keepdims=True)
        vals.append(m)
        idxs.append(ik)
        cur = jnp.where(rows == ik, -jnp.inf, cur)
    ex = [jnp.exp(v - vals[0]) for v in vals]
    denom = ex[0] + ex[1] + ex[2] + ex[3]

    onehot = jnp.zeros((N_EXPERTS, tr), f32)
    for ik in idxs:
        onehot = onehot + jnp.where(rows == ik, 1.0, 0.0)
    src = lax.broadcasted_iota(jnp.int32, (tr, tr), 0)
    dst = lax.broadcasted_iota(jnp.int32, (tr, tr), 1)
    tri = jnp.where(src < dst, 1.0, 0.0).astype(bf16)
    before = jnp.dot(onehot.astype(bf16), tri, preferred_element_type=f32) + carry[:, 0:1]

    gate_ref[...] = jnp.zeros(gate_ref.shape, f32)
    for k in range(TOP_K):
        idx_ref[k:k + 1, :] = idxs[k]
        rank_ref[k:k + 1, :] = jnp.sum(jnp.where(rows == idxs[k], before, 0.0), axis=0,
                                       keepdims=True).astype(jnp.int32)
        gate_ref[k:k + 1, :] = ex[k] / denom
    carry[...] = carry[...] + jnp.sum(onehot, axis=1, keepdims=True)
    cnt_ref[...] = carry[...].astype(jnp.int32)


def _router(h1p, h1s, g, w_router_t, b_router_col):
    n_prompt_tiles = h1p.shape[0] // TOK_TILE
    assert h1p.shape[0] % TOK_TILE == 0 and h1s.shape[0] == TOK_TILE
    n_tok = h1p.shape[0] + h1s.shape[0]
    n_tiles = n_tok // TOK_TILE
    const = lambda a: pl.BlockSpec(a.shape, lambda i: (0,) * a.ndim)
    lane_blk = lambda rows: pl.BlockSpec((rows, TOK_TILE), lambda i: (0, i))
    return pl.pallas_call(
        functools.partial(_router_kernel, n_prompt_tiles=n_prompt_tiles),
        grid=(n_tiles,),
        in_specs=[pl.BlockSpec((TOK_TILE, D_MODEL), lambda i: (jnp.minimum(i, n_prompt_tiles - 1), 0)),
                  const(h1s), const(g), const(w_router_t), const(b_router_col)],
        out_specs=[pl.BlockSpec((TOK_TILE, D_MODEL), lambda i: (i, 0)),
                   lane_blk(TOP_K), lane_blk(TOP_K), lane_blk(SUBLANES),
                   pl.BlockSpec((N_EXPERTS, 128), lambda i: (0, 0))],
        out_shape=[jax.ShapeDtypeStruct((n_tok, D_MODEL), f32),
                   jax.ShapeDtypeStruct((TOP_K, n_tok), jnp.int32),
                   jax.ShapeDtypeStruct((TOP_K, n_tok), jnp.int32),
                   jax.ShapeDtypeStruct((SUBLANES, n_tok), f32),
                   jax.ShapeDtypeStruct((N_EXPERTS, 128), jnp.int32)],
        scratch_shapes=[pltpu.VMEM((N_EXPERTS, 128), f32)],
        compiler_params=pltpu.CompilerParams(
            dimension_semantics=("arbitrary",), vmem_limit_bytes=VMEM_LIMIT),
        name="router",
    )(h1p, h1s, g, w_router_t, b_router_col)


def _start_row_gather(idx_ref, n, src_hbm, dst_ref, sem):
    def body(q, carry):
        pltpu.make_async_copy(src_hbm.at[pl.ds(idx_ref[0, 0, q], 1)],
                              dst_ref.at[pl.ds(q, 1)], sem).start()
        return carry
    lax.fori_loop(0, n, body, 0, unroll=8)


def _wait_row_gather(n, src_hbm, dst_ref, sem):
    pltpu.make_async_copy(src_hbm.at[pl.ds(0, n)], dst_ref, sem).wait()


def _expert_kernel(be_ref, nu_ref, rt_cur, rt_nxt, xn_hbm, wgu_ref, bgu_ref, wd_ref, bd_ref,
                   ys_ref, xbuf, sem):
    i = pl.program_id(0)
    n_used = nu_ref[0]
    tm = ys_ref.shape[0]
    slot = i % 2

    @pl.when(i == 0)
    def _():
        _start_row_gather(rt_cur, tm, xn_hbm, xbuf.at[0], sem.at[0])

    @pl.when(i + 1 < n_used)
    def _():
        _start_row_gather(rt_nxt, tm, xn_hbm, xbuf.at[1 - slot], sem.at[1 - slot])

    @pl.when(i < n_used)
    def _():
        _wait_row_gather(tm, xn_hbm, xbuf.at[slot], sem.at[slot])
        x = xbuf[slot].astype(bf16)
        gu = jnp.dot(x, wgu_ref[0], preferred_element_type=f32) + bgu_ref[0]
        gate = jnp.minimum(gu[:, :D_FF], SWIGLU_LIMIT)
        up = jnp.clip(gu[:, D_FF:], -SWIGLU_LIMIT, SWIGLU_LIMIT)
        glu = gate * jax.nn.sigmoid(gate * SWIGLU_ALPHA)
        act = ((up + 1.0) * glu).astype(bf16)
        ys_ref[...] = jnp.dot(act, wd_ref[0], preferred_element_type=f32) + bd_ref[0]

    @pl.when(i >= n_used)
    def _():
        ys_ref[...] = jnp.zeros(ys_ref.shape, f32)


def _experts(block_e, n_used, row_tok, xn, w_gu, b_gu, w_down, b_down):
    n_blocks = row_tok.shape[0]
    by_expert = lambda a: pl.BlockSpec((1,) + a.shape[1:], lambda i, be, nu: (be[i],) + (0,) * (a.ndim - 1))
    grid_spec = pltpu.PrefetchScalarGridSpec(
        num_scalar_prefetch=2,
        grid=(n_blocks,),
        in_specs=[pl.BlockSpec((1, 1, EXP_ROWS), lambda i, be, nu: (i, 0, 0), memory_space=pltpu.SMEM),
                  pl.BlockSpec((1, 1, EXP_ROWS), lambda i, be, nu: (jnp.minimum(i + 1, n_blocks - 1), 0, 0),
                               memory_space=pltpu.SMEM),
                  pl.BlockSpec(memory_space=pl.ANY),
                  by_expert(w_gu), by_expert(b_gu), by_expert(w_down), by_expert(b_down)],
        out_specs=pl.BlockSpec((EXP_ROWS, D_MODEL), lambda i, be, nu: (i, 0)),
        scratch_shapes=[pltpu.VMEM((2, EXP_ROWS, D_MODEL), f32), pltpu.SemaphoreType.DMA((2,))],
    )
    return pl.pallas_call(
        _expert_kernel,
        grid_spec=grid_spec,
        out_shape=jax.ShapeDtypeStruct((n_blocks * EXP_ROWS, D_MODEL), f32),
        compiler_params=pltpu.CompilerParams(
            dimension_semantics=("arbitrary",), vmem_limit_bytes=VMEM_LIMIT),
        name="experts",
    )(block_e, n_used, row_tok, row_tok, xn, w_gu, b_gu, w_down, b_down)


def _combine_kernel(d_cur, d_nxt, gate_ref, hp_ref, hs_ref, ys_hbm, gf_ref,
                    yp_ref, ysm_ref, ybuf, sem, *, n_prompt_tiles):
    i = pl.program_id(0)
    tr = hp_ref.shape[0]
    n_rows = TOP_K * tr
    slot = i % 2

    @pl.when(i == 0)
    def _():
        _start_row_gather(d_cur, n_rows, ys_hbm, ybuf.at[0], sem.at[0])

    @pl.when(i + 1 < pl.num_programs(0))
    def _():
        _start_row_gather(d_nxt, n_rows, ys_hbm, ybuf.at[1 - slot], sem.at[1 - slot])

    _wait_row_gather(n_rows, ys_hbm, ybuf.at[slot], sem.at[slot])
    h = jnp.where(i < n_prompt_tiles, hp_ref[...], hs_ref[...])
    gates = gate_ref[...].T
    moe = gates[:, 0:1] * ybuf[slot, 0:tr, :]
    for k in range(1, TOP_K):
        moe = moe + gates[:, k:k + 1] * ybuf[slot, k * tr:(k + 1) * tr, :]
    out = _rmsnorm(h + moe, gf_ref[...])

    @pl.when(i < n_prompt_tiles)
    def _():
        yp_ref[...] = out

    @pl.when(i >= n_prompt_tiles)
    def _():
        ysm_ref[...] = out


def _combine(dest_tiles, gates, h1p, h1s, ys, g_final):
    n_prompt_tiles = h1p.shape[0] // TOK_TILE
    n_tiles = dest_tiles.shape[0]
    const = lambda a: pl.BlockSpec(a.shape, lambda i: (0,) * a.ndim)
    prompt_blk = pl.BlockSpec((TOK_TILE, D_MODEL), lambda i: (jnp.minimum(i, n_prompt_tiles - 1), 0))
    return pl.pallas_call(
        functools.partial(_combine_kernel, n_prompt_tiles=n_prompt_tiles),
        grid=(n_tiles,),
        in_specs=[pl.BlockSpec((1, 1, TOP_K * TOK_TILE), lambda i: (i, 0, 0), memory_space=pltpu.SMEM),
                  pl.BlockSpec((1, 1, TOP_K * TOK_TILE), lambda i: (jnp.minimum(i + 1, n_tiles - 1), 0, 0),
                               memory_space=pltpu.SMEM),
                  pl.BlockSpec((SUBLANES, TOK_TILE), lambda i: (0, i)),
                  prompt_blk, const(h1s),
                  pl.BlockSpec(memory_space=pl.ANY),
                  const(g_final)],
        out_specs=[prompt_blk, const(h1s)],
        out_shape=[jax.ShapeDtypeStruct(h1p.shape, f32), jax.ShapeDtypeStruct(h1s.shape, f32)],
        scratch_shapes=[pltpu.VMEM((2, TOP_K * TOK_TILE, D_MODEL), f32), pltpu.SemaphoreType.DMA((2,))],
        compiler_params=pltpu.CompilerParams(
            dimension_semantics=("arbitrary",), vmem_limit_bytes=VMEM_LIMIT),
        name="combine",
    )(dest_tiles, dest_tiles, gates, h1p, h1s, ys, g_final)


def _block_diag_gate_weights(wa, wx):
    heads_per_half = N_LRU_HEADS // 2
    eye = jnp.eye(heads_per_half, dtype=wa.dtype)

    def bd(w):
        return jnp.einsum("hij,hg->higj", w, eye).reshape(heads_per_half * LRU_HEAD_DIM,
                                                          heads_per_half * LRU_HEAD_DIM)

    halves = [jnp.concatenate([bd(wa[s * heads_per_half:(s + 1) * heads_per_half]),
                               bd(wx[s * heads_per_half:(s + 1) * heads_per_half])], axis=1)
              for s in range(2)]
    return jnp.stack(halves).astype(bf16)


def kernel(x_prompt, x_sample, state_conv_short, state_conv_lru, state_lru_h, meta_tokens,
           norm_mix_g, w_in, conv_short_w, conv_lru_w, conv_lru_b, w_rg_a, b_rg_a, w_rg_x, b_rg_x,
           rg_lambda, w_out, norm_ffn_g, w_router, b_router, w_gate_up, b_gate_up, w_down, b_down,
           final_norm_g):
    assert norm_mix_g.shape[0] == 1, "single layer"
    bp, seq, _ = x_prompt.shape
    bs, seq_s, _ = x_sample.shape
    n_meta = meta_tokens.shape[0]
    row = lambda a: a.reshape(1, -1).astype(f32)

    mw = dict(g=row(norm_mix_g[0]), w_in=w_in[0].astype(bf16), csw=conv_short_w[0], clw=conv_lru_w[0],
              clb=row(conv_lru_b[0]), wg=_block_diag_gate_weights(w_rg_a[0], w_rg_x[0]),
              ba=row(b_rg_a[0]), bx=row(b_rg_x[0]), lam=row(rg_lambda[0]), w_out=w_out[0].astype(bf16))

    zeros = lambda *s: jnp.zeros(s, f32)
    _, cs_m, cl_m, h_m = _mixer(meta_tokens[None].astype(f32), zeros(1, 2, W_SHORT), zeros(1, 3, W_LRU),
                                zeros(1, 1, W_LRU), mw, n_meta)
    rep = lambda a: jnp.broadcast_to(a, (bp,) + a.shape[1:])
    h1p, cs_p, cl_p, h_p = _mixer(x_prompt, rep(cs_m), rep(cl_m), rep(h_m), mw, MIX_ROWS)
    h1s, cs_s, cl_s, h_s = _mixer(x_sample, state_conv_short[0], state_conv_lru[0],
                                  state_lru_h[0][:, None, :], mw, seq_s)
    h1p = h1p.reshape(bp * seq, D_MODEL)
    h1s = h1s.reshape(bs * seq_s, D_MODEL)
    n_tok = h1p.shape[0] + h1s.shape[0]

    xn, top_idx, rank, gates, counts = _router(
        h1p, h1s, row(norm_ffn_g[0]), w_router[0].T.astype(f32), b_router[0].reshape(-1, 1).astype(f32))

    counts = counts[:, 0]
    padded = (counts + EXP_ROWS - 1) // EXP_ROWS * EXP_ROWS
    pad_end = jnp.cumsum(padded)
    pad_start = pad_end - padded
    dest = pad_start[top_idx] + rank
    n_blocks = -(-(n_tok * TOP_K + N_EXPERTS * (EXP_ROWS - 1)) // EXP_ROWS)
    tok_ids = jnp.broadcast_to(jnp.arange(n_tok, dtype=jnp.int32)[None], dest.shape)
    row_tok = jnp.zeros((n_blocks * EXP_ROWS,), jnp.int32).at[dest.reshape(-1)].set(tok_ids.reshape(-1))
    block_e = jnp.minimum(
        jnp.searchsorted(pad_end, jnp.arange(n_blocks, dtype=jnp.int32) * EXP_ROWS, side="right"),
        N_EXPERTS - 1).astype(jnp.int32)
    n_used = (pad_end[-1:] // EXP_ROWS).astype(jnp.int32)

    ys = _experts(block_e, n_used, row_tok.reshape(n_blocks, 1, EXP_ROWS), xn,
                  w_gate_up[0].astype(bf16), b_gate_up[0][:, None, :].astype(f32),
                  w_down[0].astype(bf16), b_down[0][:, None, :].astype(f32))

    n_tiles = n_tok // TOK_TILE
    dest_tiles = dest.reshape(TOP_K, n_tiles, TOK_TILE).transpose(1, 0, 2).reshape(n_tiles, 1, TOP_K * TOK_TILE)
    yp, ysm = _combine(dest_tiles, gates, h1p, h1s, ys, row(final_norm_g))

    st = lambda a: a[None]
    return (yp.reshape(bp, seq, D_MODEL), ysm.reshape(bs, seq_s, D_MODEL),
            st(cs_p), st(cl_p), st(h_p[:, 0, :]), st(cs_s), st(cl_s), st(h_s[:, 0, :]))
```

```python
import functools

import jax
import jax.numpy as jnp
from jax import lax
from jax.experimental import pallas as pl
from jax.experimental.pallas import tpu as pltpu

D_MODEL = 1024
W_SHORT = 512
W_LRU = 512
N_LRU_HEADS = 8
LRU_HEAD_DIM = W_LRU // N_LRU_HEADS
D_IN_PROJ = 3 * W_SHORT + 2 * W_LRU
N_EXPERTS = 32
TOP_K = 4
D_FF = 1024
RG_C = 8.0
SWIGLU_LIMIT = 7.0
SWIGLU_ALPHA = 1.702
EPS = 1e-6

SUBLANES = 8
MIX_ROWS = 256
TOK_TILE = 256
EXP_ROWS = 512
VMEM_LIMIT = 56 * 1024 * 1024

f32 = jnp.float32
bf16 = jnp.bfloat16


def _rmsnorm(x, g):
    return (x * lax.rsqrt(jnp.mean(x * x, axis=-1, keepdims=True) + EPS)) * g


LANES = 128
ROW_SUB = D_MODEL // LANES


def _row_tile(ref, r):
    return ref.at[pl.ds(pl.multiple_of(r * ROW_SUB, ROW_SUB), ROW_SUB)]


def _store_rows_as_tiles(ref, x):
    for j in range(ROW_SUB):
        ref[pl.ds(j, x.shape[0], stride=ROW_SUB), :] = x[:, j * LANES:(j + 1) * LANES]


def _load_tiles_as_rows(ref, first, n):
    return jnp.concatenate([ref[pl.ds(first * ROW_SUB + j, n, stride=ROW_SUB), :] for j in range(ROW_SUB)],
                           axis=1)


def _for_each_row(n, fn, group=8):
    def body(g, carry):
        q0 = pl.multiple_of(g * group, group)
        for start in [fn(q0 + u) for u in range(group)]:
            start()
        return carry
    lax.fori_loop(0, n // group, body, 0)


def _mixer_kernel(x_ref, cs0_ref, cl0_ref, h0_ref, g_ref, win_ref, csw_ref, clw_ref, clb_ref,
                  wg_ref, ba_ref, bx_ref, lam_ref, wout_ref,
                  h1_ref, ncs_ref, ncl_ref, nh_ref,
                  cvbuf, lxbuf, hcar, *, tl):
    j = pl.program_id(1)
    hd = SUBLANES

    @pl.when(j == 0)
    def _():
        cvbuf[0:hd, :] = jnp.zeros((hd, W_SHORT), f32)
        cvbuf[hd - 2:hd, :] = cs0_ref[0]
        lxbuf[0:hd, :] = jnp.zeros((hd, W_LRU), f32)
        lxbuf[hd - 3:hd, :] = cl0_ref[0]
        hcar[...] = h0_ref[0]

    x = x_ref[0]
    u = _rmsnorm(x, g_ref[...])
    proj = jnp.dot(u.astype(bf16), win_ref[...], preferred_element_type=f32)
    b_g = proj[:, 0:W_SHORT]
    c_g = proj[:, W_SHORT:2 * W_SHORT]
    v = proj[:, 2 * W_SHORT:3 * W_SHORT]
    lx = proj[:, 3 * W_SHORT:3 * W_SHORT + W_LRU]
    ly = proj[:, 3 * W_SHORT + W_LRU:]

    cv = c_g * v
    cvbuf[hd:hd + tl, :] = cv
    conv_a = (cvbuf[hd - 2:hd - 2 + tl, :] * csw_ref[0:1, :]
              + cvbuf[hd - 1:hd - 1 + tl, :] * csw_ref[1:2, :]
              + cv * csw_ref[2:3, :])
    out_a = b_g * conv_a

    lxbuf[hd:hd + tl, :] = lx
    xc = (lxbuf[hd - 3:hd - 3 + tl, :] * clw_ref[0:1, :]
          + lxbuf[hd - 2:hd - 2 + tl, :] * clw_ref[1:2, :]
          + lxbuf[hd - 1:hd - 1 + tl, :] * clw_ref[2:3, :]
          + lx * clw_ref[3:4, :]) + clb_ref[...]
    xcb = xc.astype(bf16)
    half = W_LRU // 2
    g0 = jnp.dot(xcb[:, :half], wg_ref[0], preferred_element_type=f32)
    g1 = jnp.dot(xcb[:, half:], wg_ref[1], preferred_element_type=f32)
    ga = jnp.concatenate([g0[:, :half], g1[:, :half]], axis=1) + ba_ref[...]
    gx = jnp.concatenate([g0[:, half:], g1[:, half:]], axis=1) + bx_ref[...]
    r = jax.nn.sigmoid(ga)
    ig = jax.nn.sigmoid(gx)
    z = -lam_ref[...]
    softplus = jnp.maximum(z, 0.0) + jnp.log1p(jnp.exp(-jnp.abs(z)))
    log_a = (-RG_C * r) * softplus
    a = jnp.exp(log_a)
    th = jnp.tanh(log_a)
    uu = jnp.sqrt((-2.0 * th) / (1.0 - th)) * (ig * xc)

    row = lax.broadcasted_iota(jnp.int32, (tl, W_LRU), 0)
    uu = uu + jnp.where(row == 0, a * hcar[...], 0.0)
    aa, hh = a, uu
    s = 1
    while s < tl:
        h_sh = jnp.where(row >= s, pltpu.roll(hh, s, 0), 0.0)
        hh = aa * h_sh + hh
        if 2 * s < tl:
            a_sh = jnp.where(row >= s, pltpu.roll(aa, s, 0), 1.0)
            aa = aa * a_sh
        s *= 2
    hcar[...] = hh[tl - 1:tl, :]

    gelu = ly * (0.5 * (1.0 + jnp.tanh(0.7978845608028654 * (ly + 0.044715 * (ly * ly * ly)))))
    out_b = hh * gelu
    y = jnp.concatenate([out_a, out_b], axis=1).astype(bf16)
    h1_ref[0] = x + jnp.dot(y, wout_ref[...], preferred_element_type=f32)

    @pl.when(j == pl.num_programs(1) - 1)
    def _():
        ncs_ref[0] = cvbuf[hd + tl - 2:hd + tl, :]
        ncl_ref[0] = lxbuf[hd + tl - 3:hd + tl, :]
        nh_ref[0] = hh[tl - 1:tl, :]

    cvbuf[0:hd, :] = cvbuf[tl:tl + hd, :]
    lxbuf[0:hd, :] = lxbuf[tl:tl + hd, :]


def _mixer(x, cs0, cl0, h0, mw, tl):
    nb, seq, _ = x.shape
    assert seq % tl == 0 and tl % SUBLANES == 0
    full = lambda a: pl.BlockSpec(a.shape, lambda b, j: (0,) * a.ndim)
    per_b = lambda a: pl.BlockSpec((1,) + a.shape[1:], lambda b, j: (b,) + (0,) * (a.ndim - 1))
    weights = (mw["g"], mw["w_in"], mw["csw"], mw["clw"], mw["clb"], mw["wg"], mw["ba"], mw["bx"],
               mw["lam"], mw["w_out"])
    return pl.pallas_call(
        functools.partial(_mixer_kernel, tl=tl),
        grid=(nb, seq // tl),
        in_specs=[pl.BlockSpec((1, tl, D_MODEL), lambda b, j: (b, j, 0)),
                  per_b(cs0), per_b(cl0), per_b(h0)] + [full(w) for w in weights],
        out_specs=[pl.BlockSpec((1, tl, D_MODEL), lambda b, j: (b, j, 0)),
                   per_b(cs0), per_b(cl0), per_b(h0)],
        out_shape=[jax.ShapeDtypeStruct(x.shape, f32),
                   jax.ShapeDtypeStruct(cs0.shape, f32),
                   jax.ShapeDtypeStruct(cl0.shape, f32),
                   jax.ShapeDtypeStruct(h0.shape, f32)],
        scratch_shapes=[pltpu.VMEM((tl + SUBLANES, W_SHORT), f32),
                        pltpu.VMEM((tl + SUBLANES, W_LRU), f32),
                        pltpu.VMEM((1, W_LRU), f32)],
        compiler_params=pltpu.CompilerParams(
            dimension_semantics=("arbitrary", "arbitrary"), vmem_limit_bytes=VMEM_LIMIT),
        name=f"mixer_{seq}",
    )(x, cs0, cl0, h0, *weights)


def _router_kernel(hp_ref, hs_ref, g_ref, wrt_ref, br_ref,
                   xn_ref, idx_ref, rank_ref, gate_ref, cnt_ref, carry, *, n_prompt_tiles):
    i = pl.program_id(0)
    tr = hp_ref.shape[0]

    @pl.when(i == 0)
    def _():
        carry[...] = jnp.zeros(carry.shape, f32)

    h = jnp.where(i < n_prompt_tiles, hp_ref[...], hs_ref[...])
    xn = _rmsnorm(h, g_ref[...])
    _store_rows_as_tiles(xn_ref, xn)
    logits = lax.dot_general(wrt_ref[...], xn, (((1,), (1,)), ((), ())),
                             precision=lax.Precision.HIGHEST,
                             preferred_element_type=f32) + br_ref[...]

    rows = lax.broadcasted_iota(jnp.int32, (N_EXPERTS, tr), 0)
    vals, idxs = [], []
    cur = logits
    for _ in range(TOP_K):
        m = jnp.max(cur, axis=0, keepdims=True)
        ik = jnp.min(jnp.where(cur == m, rows, N_EXPERTS), axis=0, keepdims=True)
        vals.append(m)
        idxs.append(ik)
        cur = jnp.where(rows == ik, -jnp.inf, cur)
    ex = [jnp.exp(v - vals[0]) for v in vals]
    denom = ex[0] + ex[1] + ex[2] + ex[3]

    onehot = jnp.zeros((N_EXPERTS, tr), f32)
    for ik in idxs:
        onehot = onehot + jnp.where(rows == ik, 1.0, 0.0)
    src = lax.broadcasted_iota(jnp.int32, (tr, tr), 0)
    dst = lax.broadcasted_iota(jnp.int32, (tr, tr), 1)
    tri = jnp.where(src < dst, 1.0, 0.0).astype(bf16)
    before = jnp.dot(onehot.astype(bf16), tri, preferred_element_type=f32) + carry[:, 0:1]

    gate_ref[...] = jnp.zeros(gate_ref.shape, f32)
    for k in range(TOP_K):
        idx_ref[k:k + 1, :] = idxs[k]
        rank_ref[k:k + 1, :] = jnp.sum(jnp.where(rows == idxs[k], before, 0.0), axis=0,
                                       keepdims=True).astype(jnp.int32)
        gate_ref[k:k + 1, :] = ex[k] / denom
    carry[...] = carry[...] + jnp.sum(onehot, axis=1, keepdims=True)
    cnt_ref[...] = carry[...].astype(jnp.int32)


def _router(h1p, h1s, g, w_router_t, b_router_col):
    n_prompt_tiles = h1p.shape[0] // TOK_TILE
    assert h1p.shape[0] % TOK_TILE == 0 and h1s.shape[0] == TOK_TILE
    n_tok = h1p.shape[0] + h1s.shape[0]
    n_tiles = n_tok // TOK_TILE
    const = lambda a: pl.BlockSpec(a.shape, lambda i: (0,) * a.ndim)
    lane_blk = lambda rows: pl.BlockSpec((rows, TOK_TILE), lambda i: (0, i))
    return pl.pallas_call(
        functools.partial(_router_kernel, n_prompt_tiles=n_prompt_tiles),
        grid=(n_tiles,),
        in_specs=[pl.BlockSpec((TOK_TILE, D_MODEL), lambda i: (jnp.minimum(i, n_prompt_tiles - 1), 0)),
                  const(h1s), const(g), const(w_router_t), const(b_router_col)],
        out_specs=[pl.BlockSpec((TOK_TILE * ROW_SUB, LANES), lambda i: (i, 0)),
                   lane_blk(TOP_K), lane_blk(TOP_K), lane_blk(SUBLANES),
                   pl.BlockSpec((N_EXPERTS, 128), lambda i: (0, 0))],
        out_shape=[jax.ShapeDtypeStruct((n_tok * ROW_SUB, LANES), f32),
                   jax.ShapeDtypeStruct((TOP_K, n_tok), jnp.int32),
                   jax.ShapeDtypeStruct((TOP_K, n_tok), jnp.int32),
                   jax.ShapeDtypeStruct((SUBLANES, n_tok), f32),
                   jax.ShapeDtypeStruct((N_EXPERTS, 128), jnp.int32)],
        scratch_shapes=[pltpu.VMEM((N_EXPERTS, 128), f32)],
        compiler_params=pltpu.CompilerParams(
            dimension_semantics=("arbitrary",), vmem_limit_bytes=VMEM_LIMIT),
        name="router",
    )(h1p, h1s, g, w_router_t, b_router_col)


def _wait_rows(n, src_hbm, dst_ref, sem):
    pltpu.make_async_copy(src_hbm.at[pl.ds(0, n * ROW_SUB)], dst_ref.at[pl.ds(0, n * ROW_SUB)], sem).wait()


FILL_CHUNKS = tuple(1 << b for b in reversed(range(EXP_ROWS.bit_length() - 1)))


def _zero_fill_copies(fs_ref, fl_ref, zbuf, xs_hbm, sem):
    out = []
    for e in range(N_EXPERTS):
        start, length = fs_ref[e], fl_ref[e]
        for chunk in FILL_CHUNKS:
            offset = length & ~(2 * chunk - 1)
            first = pl.multiple_of((start + offset) * ROW_SUB, ROW_SUB)
            copy = pltpu.make_async_copy(zbuf.at[pl.ds(0, chunk * ROW_SUB)],
                                         xs_hbm.at[pl.ds(first, chunk * ROW_SUB)], sem)
            out.append(((length & chunk) != 0, copy))
    start, length = fs_ref[N_EXPERTS], fl_ref[N_EXPERTS]
    chunk = FILL_CHUNKS[0]
    for c in range(N_EXPERTS * EXP_ROWS // chunk):
        first = pl.multiple_of((start + c * chunk) * ROW_SUB, ROW_SUB)
        copy = pltpu.make_async_copy(zbuf.at[pl.ds(0, chunk * ROW_SUB)],
                                     xs_hbm.at[pl.ds(first, chunk * ROW_SUB)], sem)
        out.append((c * chunk < length, copy))
    return out


def _dispatch_kernel(fs_ref, fl_ref, d_ref, xn_hbm, xs_hbm, zbuf, sem, zsem):
    i = pl.program_id(0)
    last = pl.num_programs(0) - 1
    n_rows = TOP_K * TOK_TILE
    slot = i % 2

    @pl.when(i == 0)
    def _():
        zbuf[...] = jnp.zeros(zbuf.shape, f32)
        for present, copy in _zero_fill_copies(fs_ref, fl_ref, zbuf, xs_hbm, zsem):
            pl.when(present)(copy.start)

    base = i * TOK_TILE

    def scatter_row(q):
        tok = base + (q & (TOK_TILE - 1))
        return pltpu.make_async_copy(_row_tile(xn_hbm, tok), _row_tile(xs_hbm, d_ref[0, 0, q]),
                                     sem.at[slot]).start
    _for_each_row(n_rows, scatter_row)

    @pl.when(i > 0)
    def _():
        _wait_rows(n_rows, xn_hbm, xs_hbm, sem.at[1 - slot])

    @pl.when(i == last)
    def _():
        _wait_rows(n_rows, xn_hbm, xs_hbm, sem.at[slot])
        for present, copy in _zero_fill_copies(fs_ref, fl_ref, zbuf, xs_hbm, zsem):
            pl.when(present)(copy.wait)


def _dispatch(fill_start, fill_len, dest_tiles, xn, n_rows):
    n_tiles = dest_tiles.shape[0]
    grid_spec = pltpu.PrefetchScalarGridSpec(
        num_scalar_prefetch=2,
        grid=(n_tiles,),
        in_specs=[pl.BlockSpec((1, 1, TOP_K * TOK_TILE), lambda i, fs, fl: (i, 0, 0), memory_space=pltpu.SMEM),
                  pl.BlockSpec(memory_space=pl.ANY)],
        out_specs=pl.BlockSpec(memory_space=pl.ANY),
        scratch_shapes=[pltpu.VMEM((FILL_CHUNKS[0] * ROW_SUB, LANES), f32),
                        pltpu.SemaphoreType.DMA((2,)), pltpu.SemaphoreType.DMA(())],
    )
    return pl.pallas_call(
        _dispatch_kernel,
        grid_spec=grid_spec,
        out_shape=jax.ShapeDtypeStruct((n_rows * ROW_SUB, LANES), f32),
        compiler_params=pltpu.CompilerParams(
            dimension_semantics=("arbitrary",), vmem_limit_bytes=VMEM_LIMIT),
        name="dispatch",
    )(fill_start, fill_len, dest_tiles, xn)


def _expert_kernel(be_ref, nu_ref, xs_ref, wgu_ref, bgu_ref, wd_ref, bd_ref, ys_ref):
    i = pl.program_id(0)
    n_used = nu_ref[0]

    @pl.when(i < n_used)
    def _():
        x = _load_tiles_as_rows(xs_ref, 0, EXP_ROWS).astype(bf16)
        gu = jnp.dot(x, wgu_ref[0], preferred_element_type=f32) + bgu_ref[0]
        gate = jnp.minimum(gu[:, :D_FF], SWIGLU_LIMIT)
        up = jnp.clip(gu[:, D_FF:], -SWIGLU_LIMIT, SWIGLU_LIMIT)
        glu = gate * jax.nn.sigmoid(gate * SWIGLU_ALPHA)
        act = ((up + 1.0) * glu).astype(bf16)
        _store_rows_as_tiles(ys_ref, jnp.dot(act, wd_ref[0], preferred_element_type=f32) + bd_ref[0])

    @pl.when(i >= n_used)
    def _():
        ys_ref[...] = jnp.zeros(ys_ref.shape, f32)


def _experts(block_e, n_used, xs, w_gu, b_gu, w_down, b_down):
    n_blocks = xs.shape[0] // (EXP_ROWS * ROW_SUB)
    by_expert = lambda a: pl.BlockSpec((1,) + a.shape[1:], lambda i, be, nu: (be[i],) + (0,) * (a.ndim - 1))
    grid_spec = pltpu.PrefetchScalarGridSpec(
        num_scalar_prefetch=2,
        grid=(n_blocks,),
        in_specs=[pl.BlockSpec((EXP_ROWS * ROW_SUB, LANES), lambda i, be, nu: (jnp.minimum(i, nu[0] - 1), 0)),
                  by_expert(w_gu), by_expert(b_gu), by_expert(w_down), by_expert(b_down)],
        out_specs=pl.BlockSpec((EXP_ROWS * ROW_SUB, LANES), lambda i, be, nu: (i, 0)),
    )
    return pl.pallas_call(
        _expert_kernel,
        grid_spec=grid_spec,
        out_shape=jax.ShapeDtypeStruct(xs.shape, f32),
        compiler_params=pltpu.CompilerParams(
            dimension_semantics=("arbitrary",), vmem_limit_bytes=VMEM_LIMIT),
        name="experts",
    )(block_e, n_used, xs, w_gu, b_gu, w_down, b_down)


def _combine_kernel(d_cur, d_nxt, gate_ref, hp_ref, hs_ref, ys_hbm, gf_ref,
                    yp_ref, ysm_ref, ybuf, sem, *, n_prompt_tiles):
    i = pl.program_id(0)
    tr = hp_ref.shape[0]
    n_rows = TOP_K * tr
    slot = i % 2

    def start_gather(d_ref, sl):
        def gather_row(q):
            return pltpu.make_async_copy(_row_tile(ys_hbm, d_ref[0, 0, q]), _row_tile(ybuf.at[sl], q),
                                         sem.at[sl]).start
        _for_each_row(n_rows, gather_row)

    @pl.when(i == 0)
    def _():
        start_gather(d_cur, 0)

    @pl.when(i + 1 < pl.num_programs(0))
    def _():
        start_gather(d_nxt, 1 - slot)

    _wait_rows(n_rows, ys_hbm, ybuf.at[slot], sem.at[slot])
    h = jnp.where(i < n_prompt_tiles, hp_ref[...], hs_ref[...])
    gates = gate_ref[...].T
    moe = gates[:, 0:1] * _load_tiles_as_rows(ybuf.at[slot], 0, tr)
    for k in range(1, TOP_K):
        moe = moe + gates[:, k:k + 1] * _load_tiles_as_rows(ybuf.at[slot], k * tr, tr)
    out = _rmsnorm(h + moe, gf_ref[...])

    @pl.when(i < n_prompt_tiles)
    def _():
        yp_ref[...] = out

    @pl.when(i >= n_prompt_tiles)
    def _():
        ysm_ref[...] = out


def _combine(dest_tiles, gates, h1p, h1s, ys, g_final):
    n_prompt_tiles = h1p.shape[0] // TOK_TILE
    n_tiles = dest_tiles.shape[0]
    const = lambda a: pl.BlockSpec(a.shape, lambda i: (0,) * a.ndim)
    prompt_blk = pl.BlockSpec((TOK_TILE, D_MODEL), lambda i: (jnp.minimum(i, n_prompt_tiles - 1), 0))
    return pl.pallas_call(
        functools.partial(_combine_kernel, n_prompt_tiles=n_prompt_tiles),
        grid=(n_tiles,),
        in_specs=[pl.BlockSpec((1, 1, TOP_K * TOK_TILE), lambda i: (i, 0, 0), memory_space=pltpu.SMEM),
                  pl.BlockSpec((1, 1, TOP_K * TOK_TILE), lambda i: (jnp.minimum(i + 1, n_tiles - 1), 0, 0),
                               memory_space=pltpu.SMEM),
                  pl.BlockSpec((SUBLANES, TOK_TILE), lambda i: (0, i)),
                  prompt_blk, const(h1s),
                  pl.BlockSpec(memory_space=pl.ANY),
                  const(g_final)],
        out_specs=[prompt_blk, const(h1s)],
        out_shape=[jax.ShapeDtypeStruct(h1p.shape, f32), jax.ShapeDtypeStruct(h1s.shape, f32)],
        scratch_shapes=[pltpu.VMEM((2, TOP_K * TOK_TILE * ROW_SUB, LANES), f32), pltpu.SemaphoreType.DMA((2,))],
        compiler_params=pltpu.CompilerParams(
            dimension_semantics=("arbitrary",), vmem_limit_bytes=VMEM_LIMIT),
        name="combine",
    )(dest_tiles, dest_tiles, gates, h1p, h1s, ys, g_final)


def _block_diag_gate_weights(wa, wx):
    heads_per_half = N_LRU_HEADS // 2
    eye = jnp.eye(heads_per_half, dtype=wa.dtype)

    def bd(w):
        return jnp.einsum("hij,hg->higj", w, eye).reshape(heads_per_half * LRU_HEAD_DIM,
                                                          heads_per_half * LRU_HEAD_DIM)

    halves = [jnp.concatenate([bd(wa[s * heads_per_half:(s + 1) * heads_per_half]),
                               bd(wx[s * heads_per_half:(s + 1) * heads_per_half])], axis=1)
              for s in range(2)]
    return jnp.stack(halves).astype(bf16)


def kernel(x_prompt, x_sample, state_conv_short, state_conv_lru, state_lru_h, meta_tokens,
           norm_mix_g, w_in, conv_short_w, conv_lru_w, conv_lru_b, w_rg_a, b_rg_a, w_rg_x, b_rg_x,
           rg_lambda, w_out, norm_ffn_g, w_router, b_router, w_gate_up, b_gate_up, w_down, b_down,
           final_norm_g):
    assert norm_mix_g.shape[0] == 1, "single layer"
    bp, seq, _ = x_prompt.shape
    bs, seq_s, _ = x_sample.shape
    n_meta = meta_tokens.shape[0]
    row = lambda a: a.reshape(1, -1).astype(f32)

    mw = dict(g=row(norm_mix_g[0]), w_in=w_in[0].astype(bf16), csw=conv_short_w[0], clw=conv_lru_w[0],
              clb=row(conv_lru_b[0]), wg=_block_diag_gate_weights(w_rg_a[0], w_rg_x[0]),
              ba=row(b_rg_a[0]), bx=row(b_rg_x[0]), lam=row(rg_lambda[0]), w_out=w_out[0].astype(bf16))

    zeros = lambda *s: jnp.zeros(s, f32)
    _, cs_m, cl_m, h_m = _mixer(meta_tokens[None].astype(f32), zeros(1, 2, W_SHORT), zeros(1, 3, W_LRU),
                                zeros(1, 1, W_LRU), mw, n_meta)
    rep = lambda a: jnp.broadcast_to(a, (bp,) + a.shape[1:])
    h1p, cs_p, cl_p, h_p = _mixer(x_prompt, rep(cs_m), rep(cl_m), rep(h_m), mw, MIX_ROWS)
    h1s, cs_s, cl_s, h_s = _mixer(x_sample, state_conv_short[0], state_conv_lru[0],
                                  state_lru_h[0][:, None, :], mw, seq_s)
    h1p = h1p.reshape(bp * seq, D_MODEL)
    h1s = h1s.reshape(bs * seq_s, D_MODEL)
    n_tok = h1p.shape[0] + h1s.shape[0]

    xn, top_idx, rank, gates, counts = _router(
        h1p, h1s, row(norm_ffn_g[0]), w_router[0].T.astype(f32), b_router[0].reshape(-1, 1).astype(f32))

    counts = counts[:, 0]
    padded = (counts + EXP_ROWS - 1) // EXP_ROWS * EXP_ROWS
    pad_end = jnp.cumsum(padded)
    pad_start = pad_end - padded
    expert_ids = jnp.arange(N_EXPERTS, dtype=jnp.int32)
    dest = rank + jnp.sum(jnp.where(top_idx[None] == expert_ids[:, None, None],
                                    pad_start[:, None, None], 0), axis=0)
    n_blocks = -(-(n_tok * TOP_K + N_EXPERTS * (EXP_ROWS - 1)) // EXP_ROWS)
    block_first_row = jnp.arange(n_blocks, dtype=jnp.int32) * EXP_ROWS
    block_e = jnp.minimum(jnp.sum(pad_end[None, :] <= block_first_row[:, None], axis=1),
                          N_EXPERTS - 1).astype(jnp.int32)
    n_used = (pad_end[-1:] // EXP_ROWS).astype(jnp.int32)
    n_tiles = n_tok // TOK_TILE
    dest_tiles = dest.reshape(TOP_K, n_tiles, TOK_TILE).transpose(1, 0, 2).reshape(n_tiles, 1, TOP_K * TOK_TILE)

    n_rows = n_blocks * EXP_ROWS
    fill_start = jnp.concatenate([pad_start + counts, pad_end[-1:]]).astype(jnp.int32)
    fill_len = jnp.concatenate([padded - counts, n_rows - pad_end[-1:]]).astype(jnp.int32)
    xs = _dispatch(fill_start, fill_len, dest_tiles, xn, n_rows)
    ys = _experts(block_e, n_used, xs,
                  w_gate_up[0].astype(bf16), b_gate_up[0][:, None, :].astype(f32),
                  w_down[0].astype(bf16), b_down[0][:, None, :].astype(f32))
    yp, ysm = _combine(dest_tiles, gates, h1p, h1s, ys, row(final_norm_g))

    st = lambda a: a[None]
    return (yp.reshape(bp, seq, D_MODEL), ysm.reshape(bs, seq_s, D_MODEL),
            st(cs_p), st(cl_p), st(h_p[:, 0, :]), st(cs_s), st(cl_s), st(h_s[:, 0, :]))
```

```python
import functools

import jax
import jax.numpy as jnp
from jax import lax
from jax.experimental import pallas as pl
from jax.experimental.pallas import tpu as pltpu

D_MODEL = 1024
W_SHORT = 512
W_LRU = 512
N_LRU_HEADS = 8
LRU_HEAD_DIM = W_LRU // N_LRU_HEADS
D_IN_PROJ = 3 * W_SHORT + 2 * W_LRU
N_EXPERTS = 32
TOP_K = 4
D_FF = 1024
RG_C = 8.0
SWIGLU_LIMIT = 7.0
SWIGLU_ALPHA = 1.702
EPS = 1e-6

SUBLANES = 8
MIX_ROWS = 256
TOK_TILE = 256
EXP_ROWS = 512
VMEM_LIMIT = 56 * 1024 * 1024

f32 = jnp.float32
bf16 = jnp.bfloat16


def _rmsnorm(x, g):
    return (x * lax.rsqrt(jnp.mean(x * x, axis=-1, keepdims=True) + EPS)) * g


LANES = 128
ROW_SUB = D_MODEL // LANES


def _row_tile(ref, r):
    return ref.at[pl.ds(pl.multiple_of(r * ROW_SUB, ROW_SUB), ROW_SUB)]


def _store_rows_as_tiles(ref, x):
    for j in range(ROW_SUB):
        ref[pl.ds(j, x.shape[0], stride=ROW_SUB), :] = x[:, j * LANES:(j + 1) * LANES]


def _load_tiles_as_rows(ref, first, n):
    return jnp.concatenate([ref[pl.ds(first * ROW_SUB + j, n, stride=ROW_SUB), :] for j in range(ROW_SUB)],
                           axis=1)


def _mixer_kernel(x_ref, cs0_ref, cl0_ref, h0_ref, g_ref, win_ref, csw_ref, clw_ref, clb_ref,
                  wg_ref, ba_ref, bx_ref, lam_ref, wout_ref,
                  h1_ref, ncs_ref, ncl_ref, nh_ref,
                  cvbuf, lxbuf, hcar, *, tl):
    j = pl.program_id(1)
    hd = SUBLANES

    @pl.when(j == 0)
    def _():
        cvbuf[0:hd, :] = jnp.zeros((hd, W_SHORT), f32)
        cvbuf[hd - 2:hd, :] = cs0_ref[0]
        lxbuf[0:hd, :] = jnp.zeros((hd, W_LRU), f32)
        lxbuf[hd - 3:hd, :] = cl0_ref[0]
        hcar[...] = h0_ref[0]

    x = x_ref[0]
    u = _rmsnorm(x, g_ref[...])
    proj = jnp.dot(u.astype(bf16), win_ref[...], preferred_element_type=f32)
    b_g = proj[:, 0:W_SHORT]
    c_g = proj[:, W_SHORT:2 * W_SHORT]
    v = proj[:, 2 * W_SHORT:3 * W_SHORT]
    lx = proj[:, 3 * W_SHORT:3 * W_SHORT + W_LRU]
    ly = proj[:, 3 * W_SHORT + W_LRU:]

    cv = c_g * v
    cvbuf[hd:hd + tl, :] = cv
    conv_a = (cvbuf[hd - 2:hd - 2 + tl, :] * csw_ref[0:1, :]
              + cvbuf[hd - 1:hd - 1 + tl, :] * csw_ref[1:2, :]
              + cv * csw_ref[2:3, :])
    out_a = b_g * conv_a

    lxbuf[hd:hd + tl, :] = lx
    xc = (lxbuf[hd - 3:hd - 3 + tl, :] * clw_ref[0:1, :]
          + lxbuf[hd - 2:hd - 2 + tl, :] * clw_ref[1:2, :]
          + lxbuf[hd - 1:hd - 1 + tl, :] * clw_ref[2:3, :]
          + lx * clw_ref[3:4, :]) + clb_ref[...]
    xcb = xc.astype(bf16)
    half = W_LRU // 2
    g0 = jnp.dot(xcb[:, :half], wg_ref[0], preferred_element_type=f32)
    g1 = jnp.dot(xcb[:, half:], wg_ref[1], preferred_element_type=f32)
    ga = jnp.concatenate([g0[:, :half], g1[:, :half]], axis=1) + ba_ref[...]
    gx = jnp.concatenate([g0[:, half:], g1[:, half:]], axis=1) + bx_ref[...]
    r = jax.nn.sigmoid(ga)
    ig = jax.nn.sigmoid(gx)
    z = -lam_ref[...]
    softplus = jnp.maximum(z, 0.0) + jnp.log1p(jnp.exp(-jnp.abs(z)))
    log_a = (-RG_C * r) * softplus
    a = jnp.exp(log_a)
    th = jnp.tanh(log_a)
    uu = jnp.sqrt((-2.0 * th) / (1.0 - th)) * (ig * xc)

    row = lax.broadcasted_iota(jnp.int32, (tl, W_LRU), 0)
    uu = uu + jnp.where(row == 0, a * hcar[...], 0.0)
    aa, hh = a, uu
    s = 1
    while s < tl:
        h_sh = jnp.where(row >= s, pltpu.roll(hh, s, 0), 0.0)
        hh = aa * h_sh + hh
        if 2 * s < tl:
            a_sh = jnp.where(row >= s, pltpu.roll(aa, s, 0), 1.0)
            aa = aa * a_sh
        s *= 2
    hcar[...] = hh[tl - 1:tl, :]

    gelu = ly * (0.5 * (1.0 + jnp.tanh(0.7978845608028654 * (ly + 0.044715 * (ly * ly * ly)))))
    out_b = hh * gelu
    y = jnp.concatenate([out_a, out_b], axis=1).astype(bf16)
    h1_ref[0] = x + jnp.dot(y, wout_ref[...], preferred_element_type=f32)

    @pl.when(j == pl.num_programs(1) - 1)
    def _():
        ncs_ref[0] = cvbuf[hd + tl - 2:hd + tl, :]
        ncl_ref[0] = lxbuf[hd + tl - 3:hd + tl, :]
        nh_ref[0] = hh[tl - 1:tl, :]

    cvbuf[0:hd, :] = cvbuf[tl:tl + hd, :]
    lxbuf[0:hd, :] = lxbuf[tl:tl + hd, :]


def _mixer(x, cs0, cl0, h0, mw, tl):
    nb, seq, _ = x.shape
    assert seq % tl == 0 and tl % SUBLANES == 0
    full = lambda a: pl.BlockSpec(a.shape, lambda b, j: (0,) * a.ndim)
    per_b = lambda a: pl.BlockSpec((1,) + a.shape[1:], lambda b, j: (b,) + (0,) * (a.ndim - 1))
    weights = (mw["g"], mw["w_in"], mw["csw"], mw["clw"], mw["clb"], mw["wg"], mw["ba"], mw["bx"],
               mw["lam"], mw["w_out"])
    return pl.pallas_call(
        functools.partial(_mixer_kernel, tl=tl),
        grid=(nb, seq // tl),
        in_specs=[pl.BlockSpec((1, tl, D_MODEL), lambda b, j: (b, j, 0)),
                  per_b(cs0), per_b(cl0), per_b(h0)] + [full(w) for w in weights],
        out_specs=[pl.BlockSpec((1, tl, D_MODEL), lambda b, j: (b, j, 0)),
                   per_b(cs0), per_b(cl0), per_b(h0)],
        out_shape=[jax.ShapeDtypeStruct(x.shape, f32),
                   jax.ShapeDtypeStruct(cs0.shape, f32),
                   jax.ShapeDtypeStruct(cl0.shape, f32),
                   jax.ShapeDtypeStruct(h0.shape, f32)],
        scratch_shapes=[pltpu.VMEM((tl + SUBLANES, W_SHORT), f32),
                        pltpu.VMEM((tl + SUBLANES, W_LRU), f32),
                        pltpu.VMEM((1, W_LRU), f32)],
        compiler_params=pltpu.CompilerParams(
            dimension_semantics=("arbitrary", "arbitrary"), vmem_limit_bytes=VMEM_LIMIT),
        name=f"mixer_{seq}",
    )(x, cs0, cl0, h0, *weights)


def _router_kernel(hp_ref, hs_ref, g_ref, wrt_ref, br_ref,
                   lpos_ref, gate_ref, tab_ref, cnt_ref, carry, *, n_prompt_tiles):
    i = pl.program_id(0)
    tr = hp_ref.shape[0]

    @pl.when(i == 0)
    def _():
        carry[...] = jnp.zeros(carry.shape, f32)

    h = jnp.where(i < n_prompt_tiles, hp_ref[...], hs_ref[...])
    xn = _rmsnorm(h, g_ref[...])
    logits = lax.dot_general(wrt_ref[...], xn, (((1,), (1,)), ((), ())),
                             precision=lax.Precision.HIGHEST,
                             preferred_element_type=f32) + br_ref[...]

    rows = lax.broadcasted_iota(jnp.int32, (N_EXPERTS, tr), 0)
    vals, idxs = [], []
    cur = logits
    for _ in range(TOP_K):
        m = jnp.max(cur, axis=0, keepdims=True)
        ik = jnp.min(jnp.where(cur == m, rows, N_EXPERTS), axis=0, keepdims=True)
        vals.append(m)
        idxs.append(ik)
        cur = jnp.where(rows == ik, -jnp.inf, cur)
    ex = [jnp.exp(v - vals[0]) for v in vals]
    denom = ex[0] + ex[1] + ex[2] + ex[3]

    onehot = jnp.zeros((N_EXPERTS, tr), f32)
    for ik in idxs:
        onehot = onehot + jnp.where(rows == ik, 1.0, 0.0)
    src = lax.broadcasted_iota(jnp.int32, (tr, tr), 0)
    dst = lax.broadcasted_iota(jnp.int32, (tr, tr), 1)
    tri = jnp.where(src < dst, 1.0, 0.0).astype(bf16)
    before = jnp.dot(onehot.astype(bf16), tri, preferred_element_type=f32)
    tile_cnt = jnp.broadcast_to(jnp.sum(onehot, axis=1, keepdims=True), carry.shape)
    e_row = lax.broadcasted_iota(jnp.int32, (N_EXPERTS, N_EXPERTS), 0)
    e_col = lax.broadcasted_iota(jnp.int32, (N_EXPERTS, N_EXPERTS), 1)
    lower = jnp.where(e_col < e_row, 1.0, 0.0).astype(bf16)
    first = jnp.dot(lower, tile_cnt.astype(bf16), preferred_element_type=f32)
    within = before + first[:, 0:1]

    for k in range(TOP_K):
        lpos_ref[k:k + 1, :] = jnp.sum(jnp.where(rows == idxs[k], within, 0.0), axis=0,
                                       keepdims=True).astype(jnp.int32)
        gate_ref[k:k + 1, :] = ex[k] / denom
    lane = lax.broadcasted_iota(jnp.int32, carry.shape, 1)
    tab_ref[0] = jnp.where(lane == 0, carry[...], jnp.where(lane == 1, tile_cnt, first)).astype(jnp.int32)
    carry[...] = carry[...] + tile_cnt
    cnt_ref[...] = carry[...].astype(jnp.int32)


def _router(h1p, h1s, g, w_router_t, b_router_col):
    n_prompt_tiles = h1p.shape[0] // TOK_TILE
    assert h1p.shape[0] % TOK_TILE == 0 and h1s.shape[0] == TOK_TILE
    n_tok = h1p.shape[0] + h1s.shape[0]
    n_tiles = n_tok // TOK_TILE
    const = lambda a: pl.BlockSpec(a.shape, lambda i: (0,) * a.ndim)
    lane_blk = lambda rows: pl.BlockSpec((rows, TOK_TILE), lambda i: (0, i))
    return pl.pallas_call(
        functools.partial(_router_kernel, n_prompt_tiles=n_prompt_tiles),
        grid=(n_tiles,),
        in_specs=[pl.BlockSpec((TOK_TILE, D_MODEL), lambda i: (jnp.minimum(i, n_prompt_tiles - 1), 0)),
                  const(h1s), const(g), const(w_router_t), const(b_router_col)],
        out_specs=[lane_blk(TOP_K), lane_blk(TOP_K),
                   pl.BlockSpec((1, N_EXPERTS, LANES), lambda i: (i, 0, 0)),
                   pl.BlockSpec((N_EXPERTS, LANES), lambda i: (0, 0))],
        out_shape=[jax.ShapeDtypeStruct((TOP_K, n_tok), jnp.int32),
                   jax.ShapeDtypeStruct((TOP_K, n_tok), f32),
                   jax.ShapeDtypeStruct((n_tiles, N_EXPERTS, LANES), jnp.int32),
                   jax.ShapeDtypeStruct((N_EXPERTS, LANES), jnp.int32)],
        scratch_shapes=[pltpu.VMEM((N_EXPERTS, 128), f32)],
        compiler_params=pltpu.CompilerParams(
            dimension_semantics=("arbitrary",), vmem_limit_bytes=VMEM_LIMIT),
        name="router",
    )(h1p, h1s, g, w_router_t, b_router_col)


def _wait_rows(n, src_ref, dst_ref, sem):
    pltpu.make_async_copy(src_ref.at[pl.ds(0, n * ROW_SUB)], dst_ref.at[pl.ds(0, n * ROW_SUB)], sem).wait()


def _rows(ref, first, n):
    return ref.at[pl.ds(pl.multiple_of(first * ROW_SUB, ROW_SUB), n * ROW_SUB)]


FILL_CHUNKS = tuple(1 << b for b in reversed(range(EXP_ROWS.bit_length() - 1)))
RUN_CHUNKS = tuple(1 << b for b in reversed(range(TOK_TILE.bit_length())))


def _start_runs(tab_ref, src_ref, dst_ref, sem, *, src_is_sorted_tile):
    for e in range(N_EXPERTS):
        hbm_first = tab_ref[0, 0, e]
        length = tab_ref[0, 0, N_EXPERTS + e]
        tile_first = tab_ref[0, 0, 2 * N_EXPERTS + e]
        src_first, dst_first = (tile_first, hbm_first) if src_is_sorted_tile else (hbm_first, tile_first)
        for chunk in RUN_CHUNKS:
            offset = length & ~(2 * chunk - 1)
            copy = pltpu.make_async_copy(_rows(src_ref, src_first + offset, chunk),
                                         _rows(dst_ref, dst_first + offset, chunk), sem)
            pl.when((length & chunk) != 0)(copy.start)


def _zero_fill_copies(fs_ref, fl_ref, zbuf, xs_hbm, sem):
    out = []
    for e in range(N_EXPERTS):
        start, length = fs_ref[e], fl_ref[e]
        for chunk in FILL_CHUNKS:
            offset = length & ~(2 * chunk - 1)
            first = pl.multiple_of((start + offset) * ROW_SUB, ROW_SUB)
            copy = pltpu.make_async_copy(zbuf.at[pl.ds(0, chunk * ROW_SUB)],
                                         xs_hbm.at[pl.ds(first, chunk * ROW_SUB)], sem)
            out.append(((length & chunk) != 0, copy))
    start, length = fs_ref[N_EXPERTS], fl_ref[N_EXPERTS]
    chunk = FILL_CHUNKS[0]
    for c in range(N_EXPERTS * EXP_ROWS // chunk):
        first = pl.multiple_of((start + c * chunk) * ROW_SUB, ROW_SUB)
        copy = pltpu.make_async_copy(zbuf.at[pl.ds(0, chunk * ROW_SUB)],
                                     xs_hbm.at[pl.ds(first, chunk * ROW_SUB)], sem)
        out.append((c * chunk < length, copy))
    return out


def _dispatch_kernel(fs_ref, fl_ref, lpos_ref, tab_ref, hp_ref, hs_ref, g_ref, xs_hbm,
                     xn_tiles, sorted_buf, zbuf, sem, zsem, *, n_prompt_tiles):
    i = pl.program_id(0)
    last = pl.num_programs(0) - 1
    tr = hp_ref.shape[0]
    n_rows = TOP_K * tr
    slot = i % 2

    @pl.when(i == 0)
    def _():
        zbuf[...] = jnp.zeros(zbuf.shape, f32)
        for present, copy in _zero_fill_copies(fs_ref, fl_ref, zbuf, xs_hbm, zsem):
            pl.when(present)(copy.start)

    h = jnp.where(i < n_prompt_tiles, hp_ref[...], hs_ref[...])
    _store_rows_as_tiles(xn_tiles, _rmsnorm(h, g_ref[...]))

    @pl.when(i >= 2)
    def _():
        _wait_rows(n_rows, sorted_buf.at[slot], xs_hbm, sem.at[slot])

    def place(t, carry):
        row = _row_tile(xn_tiles, t)[...]
        for k in range(TOP_K):
            _row_tile(sorted_buf.at[slot], lpos_ref[0, 0, k * tr + t])[...] = row
        return carry
    lax.fori_loop(0, tr, place, 0, unroll=4)

    _start_runs(tab_ref, sorted_buf.at[slot], xs_hbm, sem.at[slot], src_is_sorted_tile=True)

    @pl.when(i == last)
    def _():
        _wait_rows(n_rows, sorted_buf.at[1 - slot], xs_hbm, sem.at[1 - slot])
        _wait_rows(n_rows, sorted_buf.at[slot], xs_hbm, sem.at[slot])
        for present, copy in _zero_fill_copies(fs_ref, fl_ref, zbuf, xs_hbm, zsem):
            pl.when(present)(copy.wait)


def _dispatch(fill_start, fill_len, lpos_tiles, tab_tiles, h1p, h1s, g, n_rows):
    n_prompt_tiles = h1p.shape[0] // TOK_TILE
    n_tiles = lpos_tiles.shape[0]
    assert n_tiles >= 2
    smem_blk = lambda a: pl.BlockSpec((1, 1, a.shape[2]), lambda i, fs, fl: (i, 0, 0), memory_space=pltpu.SMEM)
    const = lambda a: pl.BlockSpec(a.shape, lambda i, fs, fl: (0,) * a.ndim)
    grid_spec = pltpu.PrefetchScalarGridSpec(
        num_scalar_prefetch=2,
        grid=(n_tiles,),
        in_specs=[smem_blk(lpos_tiles), smem_blk(tab_tiles),
                  pl.BlockSpec((TOK_TILE, D_MODEL), lambda i, fs, fl: (jnp.minimum(i, n_prompt_tiles - 1), 0)),
                  const(h1s), const(g)],
        out_specs=pl.BlockSpec(memory_space=pl.ANY),
        scratch_shapes=[pltpu.VMEM((TOK_TILE * ROW_SUB, LANES), f32),
                        pltpu.VMEM((2, TOP_K * TOK_TILE * ROW_SUB, LANES), f32),
                        pltpu.VMEM((FILL_CHUNKS[0] * ROW_SUB, LANES), f32),
                        pltpu.SemaphoreType.DMA((2,)), pltpu.SemaphoreType.DMA(())],
    )
    return pl.pallas_call(
        functools.partial(_dispatch_kernel, n_prompt_tiles=n_prompt_tiles),
        grid_spec=grid_spec,
        out_shape=jax.ShapeDtypeStruct((n_rows * ROW_SUB, LANES), f32),
        compiler_params=pltpu.CompilerParams(
            dimension_semantics=("arbitrary",), vmem_limit_bytes=VMEM_LIMIT),
        name="dispatch",
    )(fill_start, fill_len, lpos_tiles, tab_tiles, h1p, h1s, g)


def _expert_kernel(be_ref, nu_ref, xs_ref, wgu_ref, bgu_ref, wd_ref, bd_ref, ys_ref):
    i = pl.program_id(0)
    n_used = nu_ref[0]

    @pl.when(i < n_used)
    def _():
        x = _load_tiles_as_rows(xs_ref, 0, EXP_ROWS).astype(bf16)
        gu = jnp.dot(x, wgu_ref[0], preferred_element_type=f32) + bgu_ref[0]
        gate = jnp.minimum(gu[:, :D_FF], SWIGLU_LIMIT)
        up = jnp.clip(gu[:, D_FF:], -SWIGLU_LIMIT, SWIGLU_LIMIT)
        glu = gate * jax.nn.sigmoid(gate * SWIGLU_ALPHA)
        act = ((up + 1.0) * glu).astype(bf16)
        _store_rows_as_tiles(ys_ref, jnp.dot(act, wd_ref[0], preferred_element_type=f32) + bd_ref[0])

    @pl.when(i >= n_used)
    def _():
        ys_ref[...] = jnp.zeros(ys_ref.shape, f32)


def _experts(block_e, n_used, xs, w_gu, b_gu, w_down, b_down):
    n_blocks = xs.shape[0] // (EXP_ROWS * ROW_SUB)
    by_expert = lambda a: pl.BlockSpec((1,) + a.shape[1:], lambda i, be, nu: (be[i],) + (0,) * (a.ndim - 1))
    grid_spec = pltpu.PrefetchScalarGridSpec(
        num_scalar_prefetch=2,
        grid=(n_blocks,),
        in_specs=[pl.BlockSpec((EXP_ROWS * ROW_SUB, LANES), lambda i, be, nu: (jnp.minimum(i, nu[0] - 1), 0)),
                  by_expert(w_gu), by_expert(b_gu), by_expert(w_down), by_expert(b_down)],
        out_specs=pl.BlockSpec((EXP_ROWS * ROW_SUB, LANES), lambda i, be, nu: (i, 0)),
    )
    return pl.pallas_call(
        _expert_kernel,
        grid_spec=grid_spec,
        out_shape=jax.ShapeDtypeStruct(xs.shape, f32),
        compiler_params=pltpu.CompilerParams(
            dimension_semantics=("arbitrary",), vmem_limit_bytes=VMEM_LIMIT),
        name="experts",
    )(block_e, n_used, xs, w_gu, b_gu, w_down, b_down)


def _combine_kernel(lpos_ref, gate_ref, tab_cur, tab_nxt, hp_ref, hs_ref, ys_hbm, gf_ref,
                    yp_ref, ysm_ref, ybuf, moe_tiles, sem, *, n_prompt_tiles):
    i = pl.program_id(0)
    tr = hp_ref.shape[0]
    n_rows = TOP_K * tr
    slot = i % 2

    @pl.when(i == 0)
    def _():
        _start_runs(tab_cur, ys_hbm, ybuf.at[0], sem.at[0], src_is_sorted_tile=False)

    @pl.when(i + 1 < pl.num_programs(0))
    def _():
        _start_runs(tab_nxt, ys_hbm, ybuf.at[1 - slot], sem.at[1 - slot], src_is_sorted_tile=False)

    _wait_rows(n_rows, ys_hbm, ybuf.at[slot], sem.at[slot])

    def mix(t, carry):
        acc = gate_ref[0, 0, t] * _row_tile(ybuf.at[slot], lpos_ref[0, 0, t])[...]
        for k in range(1, TOP_K):
            acc = acc + gate_ref[0, 0, k * tr + t] * _row_tile(ybuf.at[slot], lpos_ref[0, 0, k * tr + t])[...]
        _row_tile(moe_tiles, t)[...] = acc
        return carry
    lax.fori_loop(0, tr, mix, 0, unroll=4)

    h = jnp.where(i < n_prompt_tiles, hp_ref[...], hs_ref[...])
    out = _rmsnorm(h + _load_tiles_as_rows(moe_tiles, 0, tr), gf_ref[...])

    @pl.when(i < n_prompt_tiles)
    def _():
        yp_ref[...] = out

    @pl.when(i >= n_prompt_tiles)
    def _():
        ysm_ref[...] = out


def _combine(lpos_tiles, gate_tiles, tab_tiles, h1p, h1s, ys, g_final):
    n_prompt_tiles = h1p.shape[0] // TOK_TILE
    n_tiles = lpos_tiles.shape[0]
    const = lambda a: pl.BlockSpec(a.shape, lambda i: (0,) * a.ndim)
    prompt_blk = pl.BlockSpec((TOK_TILE, D_MODEL), lambda i: (jnp.minimum(i, n_prompt_tiles - 1), 0))
    smem_blk = lambda a, ahead: pl.BlockSpec(
        (1, 1, a.shape[2]), lambda i: (jnp.minimum(i + ahead, n_tiles - 1), 0, 0), memory_space=pltpu.SMEM)
    return pl.pallas_call(
        functools.partial(_combine_kernel, n_prompt_tiles=n_prompt_tiles),
        grid=(n_tiles,),
        in_specs=[smem_blk(lpos_tiles, 0), smem_blk(gate_tiles, 0), smem_blk(tab_tiles, 0), smem_blk(tab_tiles, 1),
                  prompt_blk, const(h1s),
                  pl.BlockSpec(memory_space=pl.ANY),
                  const(g_final)],
        out_specs=[prompt_blk, const(h1s)],
        out_shape=[jax.ShapeDtypeStruct(h1p.shape, f32), jax.ShapeDtypeStruct(h1s.shape, f32)],
        scratch_shapes=[pltpu.VMEM((2, TOP_K * TOK_TILE * ROW_SUB, LANES), f32),
                        pltpu.VMEM((TOK_TILE * ROW_SUB, LANES), f32),
                        pltpu.SemaphoreType.DMA((2,))],
        compiler_params=pltpu.CompilerParams(
            dimension_semantics=("arbitrary",), vmem_limit_bytes=VMEM_LIMIT),
        name="combine",
    )(lpos_tiles, gate_tiles, tab_tiles, tab_tiles, h1p, h1s, ys, g_final)


def _block_diag_gate_weights(wa, wx):
    heads_per_half = N_LRU_HEADS // 2
    eye = jnp.eye(heads_per_half, dtype=wa.dtype)

    def bd(w):
        return jnp.einsum("hij,hg->higj", w, eye).reshape(heads_per_half * LRU_HEAD_DIM,
                                                          heads_per_half * LRU_HEAD_DIM)

    halves = [jnp.concatenate([bd(wa[s * heads_per_half:(s + 1) * heads_per_half]),
                               bd(wx[s * heads_per_half:(s + 1) * heads_per_half])], axis=1)
              for s in range(2)]
    return jnp.stack(halves).astype(bf16)


def kernel(x_prompt, x_sample, state_conv_short, state_conv_lru, state_lru_h, meta_tokens,
           norm_mix_g, w_in, conv_short_w, conv_lru_w, conv_lru_b, w_rg_a, b_rg_a, w_rg_x, b_rg_x,
           rg_lambda, w_out, norm_ffn_g, w_router, b_router, w_gate_up, b_gate_up, w_down, b_down,
           final_norm_g):
    assert norm_mix_g.shape[0] == 1, "single layer"
    bp, seq, _ = x_prompt.shape
    bs, seq_s, _ = x_sample.shape
    n_meta = meta_tokens.shape[0]
    row = lambda a: a.reshape(1, -1).astype(f32)

    mw = dict(g=row(norm_mix_g[0]), w_in=w_in[0].astype(bf16), csw=conv_short_w[0], clw=conv_lru_w[0],
              clb=row(conv_lru_b[0]), wg=_block_diag_gate_weights(w_rg_a[0], w_rg_x[0]),
              ba=row(b_rg_a[0]), bx=row(b_rg_x[0]), lam=row(rg_lambda[0]), w_out=w_out[0].astype(bf16))

    zeros = lambda *s: jnp.zeros(s, f32)
    _, cs_m, cl_m, h_m = _mixer(meta_tokens[None].astype(f32), zeros(1, 2, W_SHORT), zeros(1, 3, W_LRU),
                                zeros(1, 1, W_LRU), mw, n_meta)
    rep = lambda a: jnp.broadcast_to(a, (bp,) + a.shape[1:])
    h1p, cs_p, cl_p, h_p = _mixer(x_prompt, rep(cs_m), rep(cl_m), rep(h_m), mw, MIX_ROWS)
    h1s, cs_s, cl_s, h_s = _mixer(x_sample, state_conv_short[0], state_conv_lru[0],
                                  state_lru_h[0][:, None, :], mw, seq_s)
    h1p = h1p.reshape(bp * seq, D_MODEL)
    h1s = h1s.reshape(bs * seq_s, D_MODEL)
    n_tok = h1p.shape[0] + h1s.shape[0]

    g_ffn = row(norm_ffn_g[0])
    lpos, gates, tab, counts = _router(
        h1p, h1s, g_ffn, w_router[0].T.astype(f32), b_router[0].reshape(-1, 1).astype(f32))

    counts = counts[:, 0]
    padded = (counts + EXP_ROWS - 1) // EXP_ROWS * EXP_ROWS
    pad_end = jnp.cumsum(padded)
    pad_start = pad_end - padded
    n_blocks = -(-(n_tok * TOP_K + N_EXPERTS * (EXP_ROWS - 1)) // EXP_ROWS)
    block_first_row = jnp.arange(n_blocks, dtype=jnp.int32) * EXP_ROWS
    block_e = jnp.minimum(jnp.sum(pad_end[None, :] <= block_first_row[:, None], axis=1),
                          N_EXPERTS - 1).astype(jnp.int32)
    n_used = (pad_end[-1:] // EXP_ROWS).astype(jnp.int32)
    n_rows = n_blocks * EXP_ROWS
    fill_start = jnp.concatenate([pad_start + counts, pad_end[-1:]]).astype(jnp.int32)
    fill_len = jnp.concatenate([padded - counts, n_rows - pad_end[-1:]]).astype(jnp.int32)

    n_tiles = n_tok // TOK_TILE
    tab_tiles = jnp.concatenate([pad_start[None, :] + tab[:, :, 0], tab[:, :, 1], tab[:, :, 2],
                                 jnp.zeros((n_tiles, LANES - 3 * N_EXPERTS), jnp.int32)],
                                axis=1).reshape(n_tiles, 1, LANES)
    by_tile = lambda a: a.reshape(TOP_K, n_tiles, TOK_TILE).transpose(1, 0, 2).reshape(
        n_tiles, 1, TOP_K * TOK_TILE)

    xs = _dispatch(fill_start, fill_len, by_tile(lpos), tab_tiles, h1p, h1s, g_ffn, n_rows)
    ys = _experts(block_e, n_used, xs,
                  w_gate_up[0].astype(bf16), b_gate_up[0][:, None, :].astype(f32),
                  w_down[0].astype(bf16), b_down[0][:, None, :].astype(f32))
    yp, ysm = _combine(by_tile(lpos), by_tile(gates), tab_tiles, h1p, h1s, ys, row(final_norm_g))

    st = lambda a: a[None]
    return (yp.reshape(bp, seq, D_MODEL), ysm.reshape(bs, seq_s, D_MODEL),
            st(cs_p), st(cl_p), st(h_p[:, 0, :]), st(cs_s), st(cl_s), st(h_s[:, 0, :]))
```

```python
import functools

import jax
import jax.numpy as jnp
from jax import lax
from jax.experimental import pallas as pl
from jax.experimental.pallas import tpu as pltpu

D_MODEL = 1024
W_SHORT = 512
W_LRU = 512
N_LRU_HEADS = 8
LRU_HEAD_DIM = W_LRU // N_LRU_HEADS
D_IN_PROJ = 3 * W_SHORT + 2 * W_LRU
N_EXPERTS = 32
TOP_K = 4
D_FF = 1024
RG_C = 8.0
SWIGLU_LIMIT = 7.0
SWIGLU_ALPHA = 1.702
EPS = 1e-6

SUBLANES = 8
MIX_ROWS = 256
MIX_STREAMS = 2
TOK_TILE = 256
EXP_ROWS = 512
VMEM_LIMIT = 56 * 1024 * 1024
EXPERT_VMEM_LIMIT = 60 * 1024 * 1024

f32 = jnp.float32
bf16 = jnp.bfloat16


def _rmsnorm(x, g):
    return (x * lax.rsqrt(jnp.mean(x * x, axis=-1, keepdims=True) + EPS)) * g


LANES = 128
ROW_SUB = D_MODEL // LANES


def _row_tile(ref, r):
    return ref.at[pl.ds(pl.multiple_of(r * ROW_SUB, ROW_SUB), ROW_SUB)]


def _row_tile_at(ref, first_sub_row):
    return ref.at[pl.ds(pl.multiple_of(first_sub_row, ROW_SUB), ROW_SUB)]


def _store_rows_as_tiles(ref, x, first=0):
    for j in range(ROW_SUB):
        ref[pl.ds(first * ROW_SUB + j, x.shape[0], stride=ROW_SUB), :] = x[:, j * LANES:(j + 1) * LANES]


def _load_tiles_as_rows(ref, first, n):
    return jnp.concatenate([ref[pl.ds(first * ROW_SUB + j, n, stride=ROW_SUB), :] for j in range(ROW_SUB)],
                           axis=1)


def _mixer_kernel(x_ref, cs0_ref, cl0_ref, h0_ref, g_ref, win_ref, csw_ref, clw_ref, clb_ref,
                  wg_ref, ba_ref, bx_ref, lam_ref, wout_ref,
                  h1_ref, ncs_ref, ncl_ref, nh_ref,
                  cvbufs, lxbufs, hcars, *, tl):
    j = pl.program_id(1)
    n_streams = x_ref.shape[0]
    hd = SUBLANES

    @pl.when(j == 0)
    def _():
        for r in range(n_streams):
            cvbufs[r, 0:hd, :] = jnp.zeros((hd, W_SHORT), f32)
            cvbufs[r, hd - 2:hd, :] = cs0_ref[r]
            lxbufs[r, 0:hd, :] = jnp.zeros((hd, W_LRU), f32)
            lxbufs[r, hd - 3:hd, :] = cl0_ref[r]
            hcars[r] = h0_ref[r]

    last_h = [_mixer_chunk(x_ref.at[r], g_ref, win_ref, csw_ref, clw_ref, clb_ref, wg_ref, ba_ref, bx_ref,
                           lam_ref, wout_ref, h1_ref.at[r], cvbufs.at[r], lxbufs.at[r], hcars.at[r], tl=tl)
              for r in range(n_streams)]

    @pl.when(j == pl.num_programs(1) - 1)
    def _():
        for r in range(n_streams):
            ncs_ref[r] = cvbufs[r, hd + tl - 2:hd + tl, :]
            ncl_ref[r] = lxbufs[r, hd + tl - 3:hd + tl, :]
            nh_ref[r] = last_h[r]

    for r in range(n_streams):
        cvbufs[r, 0:hd, :] = cvbufs[r, tl:tl + hd, :]
        lxbufs[r, 0:hd, :] = lxbufs[r, tl:tl + hd, :]


def _mixer_chunk(x_ref, g_ref, win_ref, csw_ref, clw_ref, clb_ref, wg_ref, ba_ref, bx_ref, lam_ref, wout_ref,
                 h1_ref, cvbuf, lxbuf, hcar, *, tl):
    hd = SUBLANES
    x = x_ref[...]
    u = _rmsnorm(x, g_ref[...])
    proj = jnp.dot(u.astype(bf16), win_ref[...], preferred_element_type=f32)
    b_g = proj[:, 0:W_SHORT]
    c_g = proj[:, W_SHORT:2 * W_SHORT]
    v = proj[:, 2 * W_SHORT:3 * W_SHORT]
    lx = proj[:, 3 * W_SHORT:3 * W_SHORT + W_LRU]
    ly = proj[:, 3 * W_SHORT + W_LRU:]

    cv = c_g * v
    cvbuf[hd:hd + tl, :] = cv
    conv_a = (cvbuf[hd - 2:hd - 2 + tl, :] * csw_ref[0:1, :]
              + cvbuf[hd - 1:hd - 1 + tl, :] * csw_ref[1:2, :]
              + cv * csw_ref[2:3, :])
    out_a = b_g * conv_a

    lxbuf[hd:hd + tl, :] = lx
    xc = (lxbuf[hd - 3:hd - 3 + tl, :] * clw_ref[0:1, :]
          + lxbuf[hd - 2:hd - 2 + tl, :] * clw_ref[1:2, :]
          + lxbuf[hd - 1:hd - 1 + tl, :] * clw_ref[2:3, :]
          + lx * clw_ref[3:4, :]) + clb_ref[...]
    xcb = xc.astype(bf16)
    half = W_LRU // 2
    g0 = jnp.dot(xcb[:, :half], wg_ref[0], preferred_element_type=f32)
    g1 = jnp.dot(xcb[:, half:], wg_ref[1], preferred_element_type=f32)
    ga = jnp.concatenate([g0[:, :half], g1[:, :half]], axis=1) + ba_ref[...]
    gx = jnp.concatenate([g0[:, half:], g1[:, half:]], axis=1) + bx_ref[...]
    r = jax.nn.sigmoid(ga)
    ig = jax.nn.sigmoid(gx)
    z = -lam_ref[...]
    softplus = jnp.maximum(z, 0.0) + jnp.log1p(jnp.exp(-jnp.abs(z)))
    log_a = (-RG_C * r) * softplus
    a = jnp.exp(log_a)
    th = jnp.tanh(log_a)
    uu = jnp.sqrt((-2.0 * th) / (1.0 - th)) * (ig * xc)

    n_groups = tl // SUBLANES
    aa = a.reshape(n_groups, SUBLANES, W_LRU)
    hh = uu.reshape(n_groups, SUBLANES, W_LRU)
    sub = lax.broadcasted_iota(jnp.int32, aa.shape, 1)
    s = 1
    while s < SUBLANES:
        h_sh = jnp.where(sub >= s, pltpu.roll(hh, s, 1), 0.0)
        a_sh = jnp.where(sub >= s, pltpu.roll(aa, s, 1), 1.0)
        hh = aa * h_sh + hh
        aa = aa * a_sh
        s *= 2
    a_last = jnp.broadcast_to(aa[:, SUBLANES - 1:, :], aa.shape)
    h_last = jnp.broadcast_to(hh[:, SUBLANES - 1:, :], hh.shape)
    carry = jnp.broadcast_to(hcar[...], (SUBLANES, W_LRU))
    groups = []
    for gi in range(n_groups):
        groups.append(aa[gi] * carry + hh[gi])
        carry = a_last[gi] * carry + h_last[gi]
    hh = jnp.concatenate(groups, axis=0)
    hcar[...] = carry[0:1, :]

    gelu = ly * (0.5 * (1.0 + jnp.tanh(0.7978845608028654 * (ly + 0.044715 * (ly * ly * ly)))))
    out_b = hh * gelu
    y = jnp.concatenate([out_a, out_b], axis=1).astype(bf16)
    h1_ref[...] = x + jnp.dot(y, wout_ref[...], preferred_element_type=f32)
    return hh[tl - 1:tl, :]


def _mixer(x, cs0, cl0, h0, mw, tl, streams):
    nb, seq, _ = x.shape
    assert seq % tl == 0 and tl % SUBLANES == 0 and nb % streams == 0
    full = lambda a: pl.BlockSpec(a.shape, lambda b, j: (0,) * a.ndim)
    per_b = lambda a: pl.BlockSpec((streams,) + a.shape[1:], lambda b, j: (b,) + (0,) * (a.ndim - 1))
    weights = (mw["g"], mw["w_in"], mw["csw"], mw["clw"], mw["clb"], mw["wg"], mw["ba"], mw["bx"],
               mw["lam"], mw["w_out"])
    return pl.pallas_call(
        functools.partial(_mixer_kernel, tl=tl),
        grid=(nb // streams, seq // tl),
        in_specs=[pl.BlockSpec((streams, tl, D_MODEL), lambda b, j: (b, j, 0)),
                  per_b(cs0), per_b(cl0), per_b(h0)] + [full(w) for w in weights],
        out_specs=[pl.BlockSpec((streams, tl, D_MODEL), lambda b, j: (b, j, 0)),
                   per_b(cs0), per_b(cl0), per_b(h0)],
        out_shape=[jax.ShapeDtypeStruct(x.shape, f32),
                   jax.ShapeDtypeStruct(cs0.shape, f32),
                   jax.ShapeDtypeStruct(cl0.shape, f32),
                   jax.ShapeDtypeStruct(h0.shape, f32)],
        scratch_shapes=[pltpu.VMEM((streams, tl + SUBLANES, W_SHORT), f32),
                        pltpu.VMEM((streams, tl + SUBLANES, W_LRU), f32),
                        pltpu.VMEM((streams, 1, W_LRU), f32)],
        compiler_params=pltpu.CompilerParams(
            dimension_semantics=("arbitrary", "arbitrary"), vmem_limit_bytes=VMEM_LIMIT),
        name=f"mixer_{seq}",
    )(x, cs0, cl0, h0, *weights)


def _router_kernel(hp_ref, hs_ref, g_ref, wrt_ref, br_ref,
                   lpos_ref, gate_ref, tab_ref, cnt_ref, carry, *, n_prompt_tiles):
    i = pl.program_id(0)
    tr = hp_ref.shape[0]

    @pl.when(i == 0)
    def _():
        carry[...] = jnp.zeros(carry.shape, f32)

    h = jnp.where(i < n_prompt_tiles, hp_ref[...], hs_ref[...])
    xn = _rmsnorm(h, g_ref[...])
    logits = lax.dot_general(wrt_ref[...], xn, (((1,), (1,)), ((), ())),
                             precision=lax.Precision.HIGHEST,
                             preferred_element_type=f32) + br_ref[...]

    rows = lax.broadcasted_iota(jnp.int32, (N_EXPERTS, tr), 0)
    vals, idxs = [], []
    cur = logits
    for _ in range(TOP_K):
        m = jnp.max(cur, axis=0, keepdims=True)
        ik = jnp.min(jnp.where(cur == m, rows, N_EXPERTS), axis=0, keepdims=True)
        vals.append(m)
        idxs.append(ik)
        cur = jnp.where(rows == ik, -jnp.inf, cur)
    ex = [jnp.exp(v - vals[0]) for v in vals]
    denom = ex[0] + ex[1] + ex[2] + ex[3]

    onehot = jnp.zeros((N_EXPERTS, tr), f32)
    for ik in idxs:
        onehot = onehot + jnp.where(rows == ik, 1.0, 0.0)
    src = lax.broadcasted_iota(jnp.int32, (tr, tr), 0)
    dst = lax.broadcasted_iota(jnp.int32, (tr, tr), 1)
    tri = jnp.where(src < dst, 1.0, 0.0).astype(bf16)
    before = jnp.dot(onehot.astype(bf16), tri, preferred_element_type=f32)
    tile_cnt = jnp.broadcast_to(jnp.sum(onehot, axis=1, keepdims=True), carry.shape)
    e_row = lax.broadcasted_iota(jnp.int32, (N_EXPERTS, N_EXPERTS), 0)
    e_col = lax.broadcasted_iota(jnp.int32, (N_EXPERTS, N_EXPERTS), 1)
    lower = jnp.where(e_col < e_row, 1.0, 0.0).astype(bf16)
    first = jnp.dot(lower, tile_cnt.astype(bf16), preferred_element_type=f32)
    within = before + first[:, 0:1]

    for k in range(TOP_K):
        lpos_ref[k:k + 1, :] = ROW_SUB * jnp.sum(jnp.where(rows == idxs[k], within, 0.0), axis=0,
                                                 keepdims=True).astype(jnp.int32)
        gate_ref[k:k + 1, :] = ex[k] / denom
    lane = lax.broadcasted_iota(jnp.int32, carry.shape, 1)
    tab_ref[0] = jnp.where(lane == 0, carry[...], jnp.where(lane == 1, tile_cnt, first)).astype(jnp.int32)
    carry[...] = carry[...] + tile_cnt
    cnt_ref[...] = carry[...].astype(jnp.int32)


def _router(h1p, h1s, g, w_router_t, b_router_col):
    n_prompt_tiles = h1p.shape[0] // TOK_TILE
    assert h1p.shape[0] % TOK_TILE == 0 and h1s.shape[0] == TOK_TILE
    n_tok = h1p.shape[0] + h1s.shape[0]
    n_tiles = n_tok // TOK_TILE
    const = lambda a: pl.BlockSpec(a.shape, lambda i: (0,) * a.ndim)
    lane_blk = lambda rows: pl.BlockSpec((rows, TOK_TILE), lambda i: (0, i))
    return pl.pallas_call(
        functools.partial(_router_kernel, n_prompt_tiles=n_prompt_tiles),
        grid=(n_tiles,),
        in_specs=[pl.BlockSpec((TOK_TILE, D_MODEL), lambda i: (jnp.minimum(i, n_prompt_tiles - 1), 0)),
                  const(h1s), const(g), const(w_router_t), const(b_router_col)],
        out_specs=[lane_blk(TOP_K), lane_blk(TOP_K),
                   pl.BlockSpec((1, N_EXPERTS, LANES), lambda i: (i, 0, 0)),
                   pl.BlockSpec((N_EXPERTS, LANES), lambda i: (0, 0))],
        out_shape=[jax.ShapeDtypeStruct((TOP_K, n_tok), jnp.int32),
                   jax.ShapeDtypeStruct((TOP_K, n_tok), f32),
                   jax.ShapeDtypeStruct((n_tiles, N_EXPERTS, LANES), jnp.int32),
                   jax.ShapeDtypeStruct((N_EXPERTS, LANES), jnp.int32)],
        scratch_shapes=[pltpu.VMEM((N_EXPERTS, 128), f32)],
        compiler_params=pltpu.CompilerParams(
            dimension_semantics=("arbitrary",), vmem_limit_bytes=VMEM_LIMIT),
        name="router",
    )(h1p, h1s, g, w_router_t, b_router_col)


def _wait_rows(n, src_ref, dst_ref, sem):
    pltpu.make_async_copy(src_ref.at[pl.ds(0, n * ROW_SUB)], dst_ref.at[pl.ds(0, n * ROW_SUB)], sem).wait()


def _rows(ref, first, n):
    return ref.at[pl.ds(pl.multiple_of(first * ROW_SUB, ROW_SUB), n * ROW_SUB)]


FILL_CHUNKS = tuple(1 << b for b in reversed(range(EXP_ROWS.bit_length() - 1)))
RUN_CHUNKS = tuple(1 << b for b in reversed(range(TOK_TILE.bit_length())))
RUN_LARGE = 64


def _start_runs(tab_ref, src_ref, dst_ref, sem, *, src_is_sorted_tile):
    for e in range(N_EXPERTS):
        hbm_first = tab_ref[0, 0, e]
        length = tab_ref[0, 0, N_EXPERTS + e]
        tile_first = tab_ref[0, 0, 2 * N_EXPERTS + e]
        src_first, dst_first = (tile_first, hbm_first) if src_is_sorted_tile else (hbm_first, tile_first)

        def start_pieces(chunks):
            for chunk in chunks:
                offset = length & ~(2 * chunk - 1)
                copy = pltpu.make_async_copy(_rows(src_ref, src_first + offset, chunk),
                                             _rows(dst_ref, dst_first + offset, chunk), sem)
                pl.when((length & chunk) != 0)(copy.start)

        large = tuple(c for c in RUN_CHUNKS if c >= RUN_LARGE)
        pl.when(length >= RUN_LARGE)(functools.partial(start_pieces, large))
        start_pieces(tuple(c for c in RUN_CHUNKS if c < RUN_LARGE))


def _zero_fill_copies(fs_ref, fl_ref, zbuf, xs_hbm, sem):
    out = []
    for e in range(N_EXPERTS):
        start, length = fs_ref[e], fl_ref[e]
        for chunk in FILL_CHUNKS:
            offset = length & ~(2 * chunk - 1)
            first = pl.multiple_of((start + offset) * ROW_SUB, ROW_SUB)
            copy = pltpu.make_async_copy(zbuf.at[pl.ds(0, chunk * ROW_SUB)],
                                         xs_hbm.at[pl.ds(first, chunk * ROW_SUB)], sem)
            out.append(((length & chunk) != 0, copy))
    start, length = fs_ref[N_EXPERTS], fl_ref[N_EXPERTS]
    chunk = FILL_CHUNKS[0]
    for c in range(N_EXPERTS * EXP_ROWS // chunk):
        first = pl.multiple_of((start + c * chunk) * ROW_SUB, ROW_SUB)
        copy = pltpu.make_async_copy(zbuf.at[pl.ds(0, chunk * ROW_SUB)],
                                     xs_hbm.at[pl.ds(first, chunk * ROW_SUB)], sem)
        out.append((c * chunk < length, copy))
    return out


def _dispatch_kernel(fs_ref, fl_ref, lpos_ref, tab_ref, hp_ref, hs_ref, g_ref, xs_hbm,
                     xn_tiles, sorted_buf, zbuf, sem, zsem, *, n_prompt_tiles):
    i = pl.program_id(0)
    last = pl.num_programs(0) - 1
    tr = hp_ref.shape[0]
    n_rows = TOP_K * tr
    slot = i % 2

    @pl.when(i == 0)
    def _():
        zbuf[...] = jnp.zeros(zbuf.shape, f32)
        for present, copy in _zero_fill_copies(fs_ref, fl_ref, zbuf, xs_hbm, zsem):
            pl.when(present)(copy.start)

    h = jnp.where(i < n_prompt_tiles, hp_ref[...], hs_ref[...])
    _store_rows_as_tiles(xn_tiles, _rmsnorm(h, g_ref[...]))

    @pl.when(i >= 2)
    def _():
        _wait_rows(n_rows, sorted_buf.at[slot], xs_hbm, sem.at[slot])

    def place(t, carry):
        row = _row_tile(xn_tiles, t)[...]
        for k in range(TOP_K):
            _row_tile_at(sorted_buf.at[slot], lpos_ref[0, 0, k * tr + t])[...] = row
        return carry
    lax.fori_loop(0, tr, place, 0, unroll=4)

    _start_runs(tab_ref, sorted_buf.at[slot], xs_hbm, sem.at[slot], src_is_sorted_tile=True)

    @pl.when(i == last)
    def _():
        _wait_rows(n_rows, sorted_buf.at[1 - slot], xs_hbm, sem.at[1 - slot])
        _wait_rows(n_rows, sorted_buf.at[slot], xs_hbm, sem.at[slot])
        for present, copy in _zero_fill_copies(fs_ref, fl_ref, zbuf, xs_hbm, zsem):
            pl.when(present)(copy.wait)


def _dispatch(fill_start, fill_len, lpos_tiles, tab_tiles, h1p, h1s, g, n_rows):
    n_prompt_tiles = h1p.shape[0] // TOK_TILE
    n_tiles = lpos_tiles.shape[0]
    assert n_tiles >= 2
    smem_blk = lambda a: pl.BlockSpec((1, 1, a.shape[2]), lambda i, fs, fl: (i, 0, 0), memory_space=pltpu.SMEM)
    const = lambda a: pl.BlockSpec(a.shape, lambda i, fs, fl: (0,) * a.ndim)
    grid_spec = pltpu.PrefetchScalarGridSpec(
        num_scalar_prefetch=2,
        grid=(n_tiles,),
        in_specs=[smem_blk(lpos_tiles), smem_blk(tab_tiles),
                  pl.BlockSpec((TOK_TILE, D_MODEL), lambda i, fs, fl: (jnp.minimum(i, n_prompt_tiles - 1), 0)),
                  const(h1s), const(g)],
        out_specs=pl.BlockSpec(memory_space=pl.ANY),
        scratch_shapes=[pltpu.VMEM((TOK_TILE * ROW_SUB, LANES), f32),
                        pltpu.VMEM((2, TOP_K * TOK_TILE * ROW_SUB, LANES), f32),
                        pltpu.VMEM((FILL_CHUNKS[0] * ROW_SUB, LANES), f32),
                        pltpu.SemaphoreType.DMA((2,)), pltpu.SemaphoreType.DMA(())],
    )
    return pl.pallas_call(
        functools.partial(_dispatch_kernel, n_prompt_tiles=n_prompt_tiles),
        grid_spec=grid_spec,
        out_shape=jax.ShapeDtypeStruct((n_rows * ROW_SUB, LANES), f32),
        compiler_params=pltpu.CompilerParams(
            dimension_semantics=("arbitrary",), vmem_limit_bytes=VMEM_LIMIT),
        name="dispatch",
    )(fill_start, fill_len, lpos_tiles, tab_tiles, h1p, h1s, g)


def _expert_kernel(be_ref, nu_ref, xs_ref, wgu_ref, bgu_ref, wd_ref, bd_ref, ys_ref, wgu_bf, wd_bf):
    i = pl.program_id(0)
    n_used = nu_ref[0]
    new_expert = jnp.logical_or(i == 0, be_ref[i] != be_ref[jnp.maximum(i - 1, 0)])

    @pl.when(jnp.logical_and(i < n_used, new_expert))
    def _():
        wgu_bf[...] = wgu_ref[0].astype(bf16)
        wd_bf[...] = wd_ref[0].astype(bf16)

    @pl.when(i < n_used)
    def _():
        x = _load_tiles_as_rows(xs_ref, 0, EXP_ROWS).astype(bf16)
        gu = jnp.dot(x, wgu_bf[...], preferred_element_type=f32) + bgu_ref[0]
        gate = jnp.minimum(gu[:, :D_FF], SWIGLU_LIMIT)
        up = jnp.clip(gu[:, D_FF:], -SWIGLU_LIMIT, SWIGLU_LIMIT)
        glu = gate * jax.nn.sigmoid(gate * SWIGLU_ALPHA)
        act = ((up + 1.0) * glu).astype(bf16)
        _store_rows_as_tiles(ys_ref, jnp.dot(act, wd_bf[...], preferred_element_type=f32) + bd_ref[0])

    @pl.when(i >= n_used)
    def _():
        ys_ref[...] = jnp.zeros(ys_ref.shape, f32)


def _experts(block_e, n_used, xs, w_gu, b_gu, w_down, b_down):
    n_blocks = xs.shape[0] // (EXP_ROWS * ROW_SUB)
    by_expert = lambda a: pl.BlockSpec((1,) + a.shape[1:], lambda i, be, nu: (be[i],) + (0,) * (a.ndim - 1))
    grid_spec = pltpu.PrefetchScalarGridSpec(
        num_scalar_prefetch=2,
        grid=(n_blocks,),
        in_specs=[pl.BlockSpec((EXP_ROWS * ROW_SUB, LANES), lambda i, be, nu: (jnp.minimum(i, nu[0] - 1), 0)),
                  by_expert(w_gu), by_expert(b_gu), by_expert(w_down), by_expert(b_down)],
        out_specs=pl.BlockSpec((EXP_ROWS * ROW_SUB, LANES), lambda i, be, nu: (i, 0)),
        scratch_shapes=[pltpu.VMEM(w_gu.shape[1:], bf16), pltpu.VMEM(w_down.shape[1:], bf16)],
    )
    return pl.pallas_call(
        _expert_kernel,
        grid_spec=grid_spec,
        out_shape=jax.ShapeDtypeStruct(xs.shape, f32),
        compiler_params=pltpu.CompilerParams(
            dimension_semantics=("arbitrary",), vmem_limit_bytes=EXPERT_VMEM_LIMIT),
        name="experts",
    )(block_e, n_used, xs, w_gu, b_gu, w_down, b_down)


def _combine_kernel(lpos_ref, gate_ref, tab_cur, tab_nxt, hp_ref, hs_ref, ys_hbm, gf_ref,
                    yp_ref, ysm_ref, ybuf, moe_tiles, sem, *, n_prompt_tiles):
    i = pl.program_id(0)
    tr = hp_ref.shape[0]
    n_rows = TOP_K * tr
    slot = i % 2

    @pl.when(i == 0)
    def _():
        _start_runs(tab_cur, ys_hbm, ybuf.at[0], sem.at[0], src_is_sorted_tile=False)

    @pl.when(i + 1 < pl.num_programs(0))
    def _():
        _start_runs(tab_nxt, ys_hbm, ybuf.at[1 - slot], sem.at[1 - slot], src_is_sorted_tile=False)

    _wait_rows(n_rows, ys_hbm, ybuf.at[slot], sem.at[slot])

    def mix(t, carry):
        acc = gate_ref[0, 0, t] * _row_tile_at(ybuf.at[slot], lpos_ref[0, 0, t])[...]
        for k in range(1, TOP_K):
            acc = acc + gate_ref[0, 0, k * tr + t] * _row_tile_at(ybuf.at[slot], lpos_ref[0, 0, k * tr + t])[...]
        _row_tile(moe_tiles, t)[...] = acc
        return carry
    lax.fori_loop(0, tr, mix, 0, unroll=4)

    h = jnp.where(i < n_prompt_tiles, hp_ref[...], hs_ref[...])
    out = _rmsnorm(h + _load_tiles_as_rows(moe_tiles, 0, tr), gf_ref[...])

    @pl.when(i < n_prompt_tiles)
    def _():
        yp_ref[...] = out

    @pl.when(i >= n_prompt_tiles)
    def _():
        ysm_ref[...] = out


def _combine(lpos_tiles, gate_tiles, tab_tiles, h1p, h1s, ys, g_final):
    n_prompt_tiles = h1p.shape[0] // TOK_TILE
    n_tiles = lpos_tiles.shape[0]
    const = lambda a: pl.BlockSpec(a.shape, lambda i: (0,) * a.ndim)
    prompt_blk = pl.BlockSpec((TOK_TILE, D_MODEL), lambda i: (jnp.minimum(i, n_prompt_tiles - 1), 0))
    smem_blk = lambda a, ahead: pl.BlockSpec(
        (1, 1, a.shape[2]), lambda i: (jnp.minimum(i + ahead, n_tiles - 1), 0, 0), memory_space=pltpu.SMEM)
    return pl.pallas_call(
        functools.partial(_combine_kernel, n_prompt_tiles=n_prompt_tiles),
        grid=(n_tiles,),
        in_specs=[smem_blk(lpos_tiles, 0), smem_blk(gate_tiles, 0), smem_blk(tab_tiles, 0), smem_blk(tab_tiles, 1),
                  prompt_blk, const(h1s),
                  pl.BlockSpec(memory_space=pl.ANY),
                  const(g_final)],
        out_specs=[prompt_blk, const(h1s)],
        out_shape=[jax.ShapeDtypeStruct(h1p.shape, f32), jax.ShapeDtypeStruct(h1s.shape, f32)],
        scratch_shapes=[pltpu.VMEM((2, TOP_K * TOK_TILE * ROW_SUB, LANES), f32),
                        pltpu.VMEM((TOK_TILE * ROW_SUB, LANES), f32),
                        pltpu.SemaphoreType.DMA((2,))],
        compiler_params=pltpu.CompilerParams(
            dimension_semantics=("arbitrary",), vmem_limit_bytes=VMEM_LIMIT),
        name="combine",
    )(lpos_tiles, gate_tiles, tab_tiles, tab_tiles, h1p, h1s, ys, g_final)


def _block_diag_gate_weights(wa, wx):
    heads_per_half = N_LRU_HEADS // 2
    eye = jnp.eye(heads_per_half, dtype=wa.dtype)

    def bd(w):
        return jnp.einsum("hij,hg->higj", w, eye).reshape(heads_per_half * LRU_HEAD_DIM,
                                                          heads_per_half * LRU_HEAD_DIM)

    halves = [jnp.concatenate([bd(wa[s * heads_per_half:(s + 1) * heads_per_half]),
                               bd(wx[s * heads_per_half:(s + 1) * heads_per_half])], axis=1)
              for s in range(2)]
    return jnp.stack(halves).astype(bf16)


def kernel(x_prompt, x_sample, state_conv_short, state_conv_lru, state_lru_h, meta_tokens,
           norm_mix_g, w_in, conv_short_w, conv_lru_w, conv_lru_b, w_rg_a, b_rg_a, w_rg_x, b_rg_x,
           rg_lambda, w_out, norm_ffn_g, w_router, b_router, w_gate_up, b_gate_up, w_down, b_down,
           final_norm_g):
    assert norm_mix_g.shape[0] == 1, "single layer"
    bp, seq, _ = x_prompt.shape
    bs, seq_s, _ = x_sample.shape
    n_meta = meta_tokens.shape[0]
    row = lambda a: a.reshape(1, -1).astype(f32)

    mw = dict(g=row(norm_mix_g[0]), w_in=w_in[0].astype(bf16), csw=conv_short_w[0], clw=conv_lru_w[0],
              clb=row(conv_lru_b[0]), wg=_block_diag_gate_weights(w_rg_a[0], w_rg_x[0]),
              ba=row(b_rg_a[0]), bx=row(b_rg_x[0]), lam=row(rg_lambda[0]), w_out=w_out[0].astype(bf16))

    zeros = lambda *s: jnp.zeros(s, f32)
    _, cs_m, cl_m, h_m = _mixer(meta_tokens[None].astype(f32), zeros(1, 2, W_SHORT), zeros(1, 3, W_LRU),
                                zeros(1, 1, W_LRU), mw, n_meta, 1)
    rep = lambda a: jnp.broadcast_to(a, (bp,) + a.shape[1:])
    h1p, cs_p, cl_p, h_p = _mixer(x_prompt, rep(cs_m), rep(cl_m), rep(h_m), mw, MIX_ROWS, MIX_STREAMS)
    h1s, cs_s, cl_s, h_s = _mixer(x_sample, state_conv_short[0], state_conv_lru[0],
                                  state_lru_h[0][:, None, :], mw, seq_s, MIX_STREAMS)
    h1p = h1p.reshape(bp * seq, D_MODEL)
    h1s = h1s.reshape(bs * seq_s, D_MODEL)
    n_tok = h1p.shape[0] + h1s.shape[0]

    g_ffn = row(norm_ffn_g[0])
    lpos, gates, tab, counts = _router(
        h1p, h1s, g_ffn, w_router[0].T.astype(f32), b_router[0].reshape(-1, 1).astype(f32))

    counts = counts[:, 0]
    padded = (counts + EXP_ROWS - 1) // EXP_ROWS * EXP_ROWS
    pad_end = jnp.cumsum(padded)
    pad_start = pad_end - padded
    n_blocks = -(-(n_tok * TOP_K + N_EXPERTS * (EXP_ROWS - 1)) // EXP_ROWS)
    block_first_row = jnp.arange(n_blocks, dtype=jnp.int32) * EXP_ROWS
    block_e = jnp.minimum(jnp.sum(pad_end[None, :] <= block_first_row[:, None], axis=1),
                          N_EXPERTS - 1).astype(jnp.int32)
    n_used = (pad_end[-1:] // EXP_ROWS).astype(jnp.int32)
    n_rows = n_blocks * EXP_ROWS
    fill_start = jnp.concatenate([pad_start + counts, pad_end[-1:]]).astype(jnp.int32)
    fill_len = jnp.concatenate([padded - counts, n_rows - pad_end[-1:]]).astype(jnp.int32)

    n_tiles = n_tok // TOK_TILE
    tab_tiles = jnp.concatenate([pad_start[None, :] + tab[:, :, 0], tab[:, :, 1], tab[:, :, 2],
                                 jnp.zeros((n_tiles, LANES - 3 * N_EXPERTS), jnp.int32)],
                                axis=1).reshape(n_tiles, 1, LANES)
    by_tile = lambda a: a.reshape(TOP_K, n_tiles, TOK_TILE).transpose(1, 0, 2).reshape(
        n_tiles, 1, TOP_K * TOK_TILE)

    xs = _dispatch(fill_start, fill_len, by_tile(lpos), tab_tiles, h1p, h1s, g_ffn, n_rows)
    ys = _experts(block_e, n_used, xs,
                  w_gate_up[0].astype(f32), b_gate_up[0][:, None, :].astype(f32),
                  w_down[0].astype(f32), b_down[0][:, None, :].astype(f32))
    yp, ysm = _combine(by_tile(lpos), by_tile(gates), tab_tiles, h1p, h1s, ys, row(final_norm_g))

    st = lambda a: a[None]
    return (yp.reshape(bp, seq, D_MODEL), ysm.reshape(bs, seq_s, D_MODEL),
            st(cs_p), st(cl_p), st(h_p[:, 0, :]), st(cs_s), st(cl_s), st(h_s[:, 0, :]))
```

```python
import functools

import jax
import jax.numpy as jnp
from jax import lax
from jax.experimental import pallas as pl
from jax.experimental.pallas import tpu as pltpu

D_MODEL = 1024
W_SHORT = 512
W_LRU = 512
N_LRU_HEADS = 8
LRU_HEAD_DIM = W_LRU // N_LRU_HEADS
D_IN_PROJ = 3 * W_SHORT + 2 * W_LRU
N_EXPERTS = 32
TOP_K = 4
D_FF = 1024
RG_C = 8.0
SWIGLU_LIMIT = 7.0
SWIGLU_ALPHA = 1.702
EPS = 1e-6

SUBLANES = 8
MIX_ROWS = 256
MIX_STREAMS = 2
TOK_TILE = 256
ROUTER_TILES = 2
EXP_ROWS = 512
VMEM_LIMIT = 56 * 1024 * 1024
EXPERT_VMEM_LIMIT = 60 * 1024 * 1024

f32 = jnp.float32
bf16 = jnp.bfloat16


def _rmsnorm(x, g):
    return (x * lax.rsqrt(jnp.mean(x * x, axis=-1, keepdims=True) + EPS)) * g


LANES = 128
ROW_SUB = D_MODEL // LANES


def _row_tile(ref, r):
    return ref.at[pl.ds(pl.multiple_of(r * ROW_SUB, ROW_SUB), ROW_SUB)]


def _row_tile_at(ref, first_sub_row):
    return ref.at[pl.ds(pl.multiple_of(first_sub_row, ROW_SUB), ROW_SUB)]


def _store_rows_as_tiles(ref, x, first=0):
    for j in range(ROW_SUB):
        ref[pl.ds(first * ROW_SUB + j, x.shape[0], stride=ROW_SUB), :] = x[:, j * LANES:(j + 1) * LANES]


def _load_tiles_as_rows(ref, first, n):
    return jnp.concatenate([ref[pl.ds(first * ROW_SUB + j, n, stride=ROW_SUB), :] for j in range(ROW_SUB)],
                           axis=1)


def _mixer_kernel(x_ref, cs0_ref, cl0_ref, h0_ref, g_ref, win_ref, csw_ref, clw_ref, clb_ref,
                  wg_ref, ba_ref, bx_ref, lam_ref, wout_ref,
                  h1_ref, ncs_ref, ncl_ref, nh_ref,
                  cvtails, lxtails, hcars, *, tl):
    j = pl.program_id(1)
    n_streams = x_ref.shape[0]
    n_cs, n_cl = cs0_ref.shape[1], cl0_ref.shape[1]

    @pl.when(j == 0)
    def _():
        for r in range(n_streams):
            cvtails[r] = jnp.zeros((SUBLANES, W_SHORT), f32)
            cvtails[r, SUBLANES - n_cs:, :] = cs0_ref[r]
            lxtails[r] = jnp.zeros((SUBLANES, W_LRU), f32)
            lxtails[r, SUBLANES - n_cl:, :] = cl0_ref[r]
            hcars[r] = h0_ref[r]

    ends = [_mixer_chunk(x_ref.at[r], g_ref, win_ref, csw_ref, clw_ref, clb_ref, wg_ref, ba_ref, bx_ref,
                         lam_ref, wout_ref, h1_ref.at[r], cvtails.at[r], lxtails.at[r], hcars.at[r], tl=tl)
            for r in range(n_streams)]

    @pl.when(j == pl.num_programs(1) - 1)
    def _():
        for r, (cv_end, lx_end, h_end) in enumerate(ends):
            ncs_ref[r] = cv_end[SUBLANES - n_cs:, :]
            ncl_ref[r] = lx_end[SUBLANES - n_cl:, :]
            nh_ref[r] = h_end


def _delayed(x3, tail, s, sub):
    rolled = pltpu.roll(x3, s, 1)
    before = jnp.concatenate([pltpu.roll(tail, s, 0)[None], rolled[:-1]], axis=0)
    return jnp.where(sub < s, before, rolled)


def _mixer_chunk(x_ref, g_ref, win_ref, csw_ref, clw_ref, clb_ref, wg_ref, ba_ref, bx_ref, lam_ref, wout_ref,
                 h1_ref, cvtail, lxtail, hcar, *, tl):
    n_groups = tl // SUBLANES
    grouped = lambda a: a.reshape(n_groups, SUBLANES, a.shape[-1])
    sub = lax.broadcasted_iota(jnp.int32, (n_groups, SUBLANES, W_LRU), 1)
    x = x_ref[...]
    u = _rmsnorm(x, g_ref[...])
    proj = jnp.dot(u.astype(bf16), win_ref[...], preferred_element_type=f32)
    b_g = proj[:, 0:W_SHORT]
    c_g = proj[:, W_SHORT:2 * W_SHORT]
    v = proj[:, 2 * W_SHORT:3 * W_SHORT]
    lx = proj[:, 3 * W_SHORT:3 * W_SHORT + W_LRU]
    ly = proj[:, 3 * W_SHORT + W_LRU:]

    cv = grouped(c_g * v)
    cv_tail = cvtail[...]
    conv_a = (_delayed(cv, cv_tail, 2, sub) * csw_ref[0:1, :]
              + _delayed(cv, cv_tail, 1, sub) * csw_ref[1:2, :]
              + cv * csw_ref[2:3, :])
    out_a = b_g * conv_a.reshape(tl, W_SHORT)
    cv_end = cv[n_groups - 1]
    cvtail[...] = cv_end

    lx = grouped(lx)
    lx_tail = lxtail[...]
    xc = ((_delayed(lx, lx_tail, 3, sub) * clw_ref[0:1, :]
           + _delayed(lx, lx_tail, 2, sub) * clw_ref[1:2, :]
           + _delayed(lx, lx_tail, 1, sub) * clw_ref[2:3, :]
           + lx * clw_ref[3:4, :]) + clb_ref[...]).reshape(tl, W_LRU)
    lx_end = lx[n_groups - 1]
    lxtail[...] = lx_end
    xcb = xc.astype(bf16)
    half = W_LRU // 2
    g0 = jnp.dot(xcb[:, :half], wg_ref[0], preferred_element_type=f32)
    g1 = jnp.dot(xcb[:, half:], wg_ref[1], preferred_element_type=f32)
    ga = jnp.concatenate([g0[:, :half], g1[:, :half]], axis=1) + ba_ref[...]
    gx = jnp.concatenate([g0[:, half:], g1[:, half:]], axis=1) + bx_ref[...]
    r = jax.nn.sigmoid(ga)
    ig = jax.nn.sigmoid(gx)
    z = -lam_ref[...]
    softplus = jnp.maximum(z, 0.0) + jnp.log1p(jnp.exp(-jnp.abs(z)))
    log_a = (-RG_C * r) * softplus
    a = jnp.exp(log_a)
    th = jnp.tanh(log_a)
    uu = jnp.sqrt((-2.0 * th) / (1.0 - th)) * (ig * xc)

    aa = grouped(a)
    hh = grouped(uu)
    s = 1
    while s < SUBLANES:
        h_sh = jnp.where(sub >= s, pltpu.roll(hh, s, 1), 0.0)
        a_sh = jnp.where(sub >= s, pltpu.roll(aa, s, 1), 1.0)
        hh = aa * h_sh + hh
        aa = aa * a_sh
        s *= 2
    a_last = jnp.broadcast_to(aa[:, SUBLANES - 1:, :], aa.shape)
    h_last = jnp.broadcast_to(hh[:, SUBLANES - 1:, :], hh.shape)
    carry = jnp.broadcast_to(hcar[...], (SUBLANES, W_LRU))
    groups = []
    for gi in range(n_groups):
        groups.append(aa[gi] * carry + hh[gi])
        carry = a_last[gi] * carry + h_last[gi]
    hh = jnp.concatenate(groups, axis=0)
    hcar[...] = carry[0:1, :]

    gelu = ly * (0.5 * (1.0 + jnp.tanh(0.7978845608028654 * (ly + 0.044715 * (ly * ly * ly)))))
    out_b = hh * gelu
    y = jnp.concatenate([out_a, out_b], axis=1).astype(bf16)
    h1_ref[...] = x + jnp.dot(y, wout_ref[...], preferred_element_type=f32)
    return cv_end, lx_end, hh[tl - 1:tl, :]


def _mixer(x, cs0, cl0, h0, mw, tl, streams):
    nb, seq, _ = x.shape
    assert seq % tl == 0 and tl % SUBLANES == 0 and nb % streams == 0
    full = lambda a: pl.BlockSpec(a.shape, lambda b, j: (0,) * a.ndim)
    per_b = lambda a: pl.BlockSpec((streams,) + a.shape[1:], lambda b, j: (b,) + (0,) * (a.ndim - 1))
    weights = (mw["g"], mw["w_in"], mw["csw"], mw["clw"], mw["clb"], mw["wg"], mw["ba"], mw["bx"],
               mw["lam"], mw["w_out"])
    return pl.pallas_call(
        functools.partial(_mixer_kernel, tl=tl),
        grid=(nb // streams, seq // tl),
        in_specs=[pl.BlockSpec((streams, tl, D_MODEL), lambda b, j: (b, j, 0)),
                  per_b(cs0), per_b(cl0), per_b(h0)] + [full(w) for w in weights],
        out_specs=[pl.BlockSpec((streams, tl, D_MODEL), lambda b, j: (b, j, 0)),
                   per_b(cs0), per_b(cl0), per_b(h0)],
        out_shape=[jax.ShapeDtypeStruct(x.shape, f32),
                   jax.ShapeDtypeStruct(cs0.shape, f32),
                   jax.ShapeDtypeStruct(cl0.shape, f32),
                   jax.ShapeDtypeStruct(h0.shape, f32)],
        scratch_shapes=[pltpu.VMEM((streams, SUBLANES, W_SHORT), f32),
                        pltpu.VMEM((streams, SUBLANES, W_LRU), f32),
                        pltpu.VMEM((streams, 1, W_LRU), f32)],
        compiler_params=pltpu.CompilerParams(
            dimension_semantics=("arbitrary", "arbitrary"), vmem_limit_bytes=VMEM_LIMIT),
        name=f"mixer_{seq}",
    )(x, cs0, cl0, h0, *weights)


def _route_tile(h, g, w_router_t, b_router_col):
    tr = h.shape[0]
    xn = _rmsnorm(h, g)
    logits = lax.dot_general(w_router_t.astype(bf16), xn.astype(bf16), (((1,), (1,)), ((), ())),
                             preferred_element_type=f32) + b_router_col

    rows = lax.broadcasted_iota(jnp.int32, (N_EXPERTS, tr), 0)
    vals, idxs = [], []
    cur = logits
    for _ in range(TOP_K):
        m = jnp.max(cur, axis=0, keepdims=True)
        ik = jnp.min(jnp.where(cur == m, rows, N_EXPERTS), axis=0, keepdims=True)
        vals.append(m)
        idxs.append(ik)
        cur = jnp.where(rows == ik, -jnp.inf, cur)
    ex = [jnp.exp(v - vals[0]) for v in vals]
    denom = ex[0] + ex[1] + ex[2] + ex[3]

    onehot = jnp.zeros((N_EXPERTS, tr), f32)
    for ik in idxs:
        onehot = onehot + jnp.where(rows == ik, 1.0, 0.0)
    src = lax.broadcasted_iota(jnp.int32, (tr, tr), 0)
    dst = lax.broadcasted_iota(jnp.int32, (tr, tr), 1)
    tri = jnp.where(src < dst, 1.0, 0.0).astype(bf16)
    before = jnp.dot(onehot.astype(bf16), tri, preferred_element_type=f32)
    tile_cnt = jnp.broadcast_to(jnp.sum(onehot, axis=1, keepdims=True), (N_EXPERTS, LANES))
    e_row = lax.broadcasted_iota(jnp.int32, (N_EXPERTS, N_EXPERTS), 0)
    e_col = lax.broadcasted_iota(jnp.int32, (N_EXPERTS, N_EXPERTS), 1)
    lower = jnp.where(e_col < e_row, 1.0, 0.0).astype(bf16)
    first = jnp.dot(lower, tile_cnt.astype(bf16), preferred_element_type=f32)
    within = before + first[:, 0:1]
    lpos = [ROW_SUB * jnp.sum(jnp.where(rows == ik, within, 0.0), axis=0, keepdims=True).astype(jnp.int32)
            for ik in idxs]
    return lpos, [e / denom for e in ex], tile_cnt, first


def _router_kernel(hp_ref, hs_ref, g_ref, wrt_ref, br_ref,
                   lpos_ref, gate_ref, tab_ref, cnt_ref, carry, *, n_prompt_steps):
    s = pl.program_id(0)
    tr = TOK_TILE

    @pl.when(s == 0)
    def _():
        carry[...] = jnp.zeros(carry.shape, f32)

    on_prompt = s < n_prompt_steps
    tiles = ([jnp.where(on_prompt, hp_ref[0:tr, :], hs_ref[...])]
             + [hp_ref[r * tr:(r + 1) * tr, :] for r in range(1, ROUTER_TILES)])
    routed = [_route_tile(h, g_ref[...], wrt_ref[...], br_ref[...]) for h in tiles]

    lane = lax.broadcasted_iota(jnp.int32, carry.shape, 1)
    earlier = carry[...]
    for r, (lpos, gates, tile_cnt, first) in enumerate(routed):
        for k in range(TOP_K):
            lpos_ref[k:k + 1, r * tr:(r + 1) * tr] = lpos[k]
            gate_ref[k:k + 1, r * tr:(r + 1) * tr] = gates[k]
        tab_ref[r] = jnp.where(lane == 0, earlier, jnp.where(lane == 1, tile_cnt, first)).astype(jnp.int32)
        earlier = earlier + (tile_cnt if r == 0 else jnp.where(on_prompt, tile_cnt, 0.0))
    carry[...] = earlier
    cnt_ref[...] = earlier.astype(jnp.int32)


def _router(h1p, h1s, g, w_router_t, b_router_col):
    pair = ROUTER_TILES * TOK_TILE
    assert h1p.shape[0] % pair == 0 and h1s.shape[0] == TOK_TILE
    n_prompt_steps = h1p.shape[0] // pair
    n_steps = n_prompt_steps + 1
    const = lambda a: pl.BlockSpec(a.shape, lambda s: (0,) * a.ndim)
    lane_blk = pl.BlockSpec((TOP_K, pair), lambda s: (0, s))
    return pl.pallas_call(
        functools.partial(_router_kernel, n_prompt_steps=n_prompt_steps),
        grid=(n_steps,),
        in_specs=[pl.BlockSpec((pair, D_MODEL), lambda s: (jnp.minimum(s, n_prompt_steps - 1), 0)),
                  const(h1s), const(g), const(w_router_t), const(b_router_col)],
        out_specs=[lane_blk, lane_blk,
                   pl.BlockSpec((ROUTER_TILES, N_EXPERTS, LANES), lambda s: (s, 0, 0)),
                   pl.BlockSpec((N_EXPERTS, LANES), lambda s: (0, 0))],
        out_shape=[jax.ShapeDtypeStruct((TOP_K, n_steps * pair), jnp.int32),
                   jax.ShapeDtypeStruct((TOP_K, n_steps * pair), f32),
                   jax.ShapeDtypeStruct((ROUTER_TILES * n_steps, N_EXPERTS, LANES), jnp.int32),
                   jax.ShapeDtypeStruct((N_EXPERTS, LANES), jnp.int32)],
        scratch_shapes=[pltpu.VMEM((N_EXPERTS, LANES), f32)],
        compiler_params=pltpu.CompilerParams(
            dimension_semantics=("arbitrary",), vmem_limit_bytes=VMEM_LIMIT),
        name="router",
    )(h1p, h1s, g, w_router_t, b_router_col)


def _wait_rows(n, src_ref, dst_ref, sem):
    pltpu.make_async_copy(src_ref.at[pl.ds(0, n * ROW_SUB)], dst_ref.at[pl.ds(0, n * ROW_SUB)], sem).wait()


def _rows(ref, first, n):
    return ref.at[pl.ds(pl.multiple_of(first * ROW_SUB, ROW_SUB), n * ROW_SUB)]


FILL_CHUNKS = tuple(1 << b for b in reversed(range(EXP_ROWS.bit_length() - 1)))
RUN_CHUNKS = tuple(1 << b for b in reversed(range(TOK_TILE.bit_length())))
RUN_LARGE = 64


def _start_runs(tab_ref, src_ref, dst_ref, sem, *, src_is_sorted_tile):
    for e in range(N_EXPERTS):
        hbm_first = tab_ref[0, 0, e]
        length = tab_ref[0, 0, N_EXPERTS + e]
        tile_first = tab_ref[0, 0, 2 * N_EXPERTS + e]
        src_first, dst_first = (tile_first, hbm_first) if src_is_sorted_tile else (hbm_first, tile_first)

        def start_pieces(chunks):
            for chunk in chunks:
                offset = length & ~(2 * chunk - 1)
                copy = pltpu.make_async_copy(_rows(src_ref, src_first + offset, chunk),
                                             _rows(dst_ref, dst_first + offset, chunk), sem)
                pl.when((length & chunk) != 0)(copy.start)

        large = tuple(c for c in RUN_CHUNKS if c >= RUN_LARGE)
        pl.when(length >= RUN_LARGE)(functools.partial(start_pieces, large))
        start_pieces(tuple(c for c in RUN_CHUNKS if c < RUN_LARGE))


def _zero_fill_copies(fs_ref, fl_ref, zbuf, xs_hbm, sem):
    out = []
    for e in range(N_EXPERTS):
        start, length = fs_ref[e], fl_ref[e]
        for chunk in FILL_CHUNKS:
            offset = length & ~(2 * chunk - 1)
            first = pl.multiple_of((start + offset) * ROW_SUB, ROW_SUB)
            copy = pltpu.make_async_copy(zbuf.at[pl.ds(0, chunk * ROW_SUB)],
                                         xs_hbm.at[pl.ds(first, chunk * ROW_SUB)], sem)
            out.append(((length & chunk) != 0, copy))
    start, length = fs_ref[N_EXPERTS], fl_ref[N_EXPERTS]
    chunk = FILL_CHUNKS[0]
    for c in range(N_EXPERTS * EXP_ROWS // chunk):
        first = pl.multiple_of((start + c * chunk) * ROW_SUB, ROW_SUB)
        copy = pltpu.make_async_copy(zbuf.at[pl.ds(0, chunk * ROW_SUB)],
                                     xs_hbm.at[pl.ds(first, chunk * ROW_SUB)], sem)
        out.append((c * chunk < length, copy))
    return out


def _dispatch_kernel(fs_ref, fl_ref, lpos_ref, tab_ref, hp_ref, hs_ref, g_ref, xs_hbm,
                     xn_tiles, sorted_buf, zbuf, sem, zsem, *, n_prompt_tiles):
    i = pl.program_id(0)
    last = pl.num_programs(0) - 1
    tr = hp_ref.shape[0]
    n_rows = TOP_K * tr
    slot = i % 2

    @pl.when(i == 0)
    def _():
        zbuf[...] = jnp.zeros(zbuf.shape, f32)
        for present, copy in _zero_fill_copies(fs_ref, fl_ref, zbuf, xs_hbm, zsem):
            pl.when(present)(copy.start)

    h = jnp.where(i < n_prompt_tiles, hp_ref[...], hs_ref[...])
    _store_rows_as_tiles(xn_tiles, _rmsnorm(h, g_ref[...]))

    @pl.when(i >= 2)
    def _():
        _wait_rows(n_rows, sorted_buf.at[slot], xs_hbm, sem.at[slot])

    def place(t, carry):
        row = _row_tile(xn_tiles, t)[...]
        for k in range(TOP_K):
            _row_tile_at(sorted_buf.at[slot], lpos_ref[0, 0, k * tr + t])[...] = row
        return carry
    lax.fori_loop(0, tr, place, 0, unroll=4)

    _start_runs(tab_ref, sorted_buf.at[slot], xs_hbm, sem.at[slot], src_is_sorted_tile=True)

    @pl.when(i == last)
    def _():
        _wait_rows(n_rows, sorted_buf.at[1 - slot], xs_hbm, sem.at[1 - slot])
        _wait_rows(n_rows, sorted_buf.at[slot], xs_hbm, sem.at[slot])
        for present, copy in _zero_fill_copies(fs_ref, fl_ref, zbuf, xs_hbm, zsem):
            pl.when(present)(copy.wait)


def _dispatch(fill_start, fill_len, lpos_tiles, tab_tiles, h1p, h1s, g, n_rows):
    n_prompt_tiles = h1p.shape[0] // TOK_TILE
    n_tiles = lpos_tiles.shape[0]
    assert n_tiles >= 2
    smem_blk = lambda a: pl.BlockSpec((1, 1, a.shape[2]), lambda i, fs, fl: (i, 0, 0), memory_space=pltpu.SMEM)
    const = lambda a: pl.BlockSpec(a.shape, lambda i, fs, fl: (0,) * a.ndim)
    grid_spec = pltpu.PrefetchScalarGridSpec(
        num_scalar_prefetch=2,
        grid=(n_tiles,),
        in_specs=[smem_blk(lpos_tiles), smem_blk(tab_tiles),
                  pl.BlockSpec((TOK_TILE, D_MODEL), lambda i, fs, fl: (jnp.minimum(i, n_prompt_tiles - 1), 0)),
                  const(h1s), const(g)],
        out_specs=pl.BlockSpec(memory_space=pl.ANY),
        scratch_shapes=[pltpu.VMEM((TOK_TILE * ROW_SUB, LANES), f32),
                        pltpu.VMEM((2, TOP_K * TOK_TILE * ROW_SUB, LANES), f32),
                        pltpu.VMEM((FILL_CHUNKS[0] * ROW_SUB, LANES), f32),
                        pltpu.SemaphoreType.DMA((2,)), pltpu.SemaphoreType.DMA(())],
    )
    return pl.pallas_call(
        functools.partial(_dispatch_kernel, n_prompt_tiles=n_prompt_tiles),
        grid_spec=grid_spec,
        out_shape=jax.ShapeDtypeStruct((n_rows * ROW_SUB, LANES), f32),
        compiler_params=pltpu.CompilerParams(
            dimension_semantics=("arbitrary",), vmem_limit_bytes=VMEM_LIMIT),
        name="dispatch",
    )(fill_start, fill_len, lpos_tiles, tab_tiles, h1p, h1s, g)


def _expert_kernel(be_ref, nu_ref, xs_ref, wgu_ref, bgu_ref, wd_ref, bd_ref, ys_ref, wgu_bf, wd_bf):
    i = pl.program_id(0)
    n_used = nu_ref[0]
    new_expert = jnp.logical_or(i == 0, be_ref[i] != be_ref[jnp.maximum(i - 1, 0)])

    @pl.when(jnp.logical_and(i < n_used, new_expert))
    def _():
        wgu_bf[...] = wgu_ref[0].astype(bf16)
        wd_bf[...] = wd_ref[0].astype(bf16)

    @pl.when(i < n_used)
    def _():
        x = _load_tiles_as_rows(xs_ref, 0, EXP_ROWS).astype(bf16)
        gu = jnp.dot(x, wgu_bf[...], preferred_element_type=f32) + bgu_ref[0]
        gate = jnp.minimum(gu[:, :D_FF], SWIGLU_LIMIT)
        up = jnp.clip(gu[:, D_FF:], -SWIGLU_LIMIT, SWIGLU_LIMIT)
        glu = gate * jax.nn.sigmoid(gate * SWIGLU_ALPHA)
        act = ((up + 1.0) * glu).astype(bf16)
        _store_rows_as_tiles(ys_ref, jnp.dot(act, wd_bf[...], preferred_element_type=f32) + bd_ref[0])

    @pl.when(i >= n_used)
    def _():
        ys_ref[...] = jnp.zeros(ys_ref.shape, f32)


def _experts(block_e, n_used, xs, w_gu, b_gu, w_down, b_down):
    n_blocks = xs.shape[0] // (EXP_ROWS * ROW_SUB)
    by_expert = lambda a: pl.BlockSpec((1,) + a.shape[1:], lambda i, be, nu: (be[i],) + (0,) * (a.ndim - 1))
    grid_spec = pltpu.PrefetchScalarGridSpec(
        num_scalar_prefetch=2,
        grid=(n_blocks,),
        in_specs=[pl.BlockSpec((EXP_ROWS * ROW_SUB, LANES),
                               lambda i, be, nu: (jnp.minimum(i, jnp.maximum(nu[0] - 1, 0)), 0)),
                  by_expert(w_gu), by_expert(b_gu), by_expert(w_down), by_expert(b_down)],
        out_specs=pl.BlockSpec((EXP_ROWS * ROW_SUB, LANES), lambda i, be, nu: (i, 0)),
        scratch_shapes=[pltpu.VMEM(w_gu.shape[1:], bf16), pltpu.VMEM(w_down.shape[1:], bf16)],
    )
    return pl.pallas_call(
        _expert_kernel,
        grid_spec=grid_spec,
        out_shape=jax.ShapeDtypeStruct(xs.shape, f32),
        compiler_params=pltpu.CompilerParams(
            dimension_semantics=("arbitrary",), vmem_limit_bytes=EXPERT_VMEM_LIMIT),
        name="experts",
    )(block_e, n_used, xs, w_gu, b_gu, w_down, b_down)


def _combine_kernel(lpos_ref, gate_ref, tab_cur, tab_nxt, hp_ref, hs_ref, ys_hbm, gf_ref,
                    yp_ref, ysm_ref, ybuf, moe_tiles, sem, *, n_prompt_tiles):
    i = pl.program_id(0)
    tr = hp_ref.shape[0]
    n_rows = TOP_K * tr
    slot = i % 2

    @pl.when(i == 0)
    def _():
        _start_runs(tab_cur, ys_hbm, ybuf.at[0], sem.at[0], src_is_sorted_tile=False)

    @pl.when(i + 1 < pl.num_programs(0))
    def _():
        _start_runs(tab_nxt, ys_hbm, ybuf.at[1 - slot], sem.at[1 - slot], src_is_sorted_tile=False)

    _wait_rows(n_rows, ys_hbm, ybuf.at[slot], sem.at[slot])

    def mix(t, carry):
        acc = gate_ref[0, 0, t] * _row_tile_at(ybuf.at[slot], lpos_ref[0, 0, t])[...]
        for k in range(1, TOP_K):
            acc = acc + gate_ref[0, 0, k * tr + t] * _row_tile_at(ybuf.at[slot], lpos_ref[0, 0, k * tr + t])[...]
        _row_tile(moe_tiles, t)[...] = acc
        return carry
    lax.fori_loop(0, tr, mix, 0, unroll=4)

    h = jnp.where(i < n_prompt_tiles, hp_ref[...], hs_ref[...])
    out = _rmsnorm(h + _load_tiles_as_rows(moe_tiles, 0, tr), gf_ref[...])

    @pl.when(i < n_prompt_tiles)
    def _():
        yp_ref[...] = out

    @pl.when(i >= n_prompt_tiles)
    def _():
        ysm_ref[...] = out


def _combine(lpos_tiles, gate_tiles, tab_tiles, h1p, h1s, ys, g_final):
    n_prompt_tiles = h1p.shape[0] // TOK_TILE
    n_tiles = lpos_tiles.shape[0]
    const = lambda a: pl.BlockSpec(a.shape, lambda i: (0,) * a.ndim)
    prompt_blk = pl.BlockSpec((TOK_TILE, D_MODEL), lambda i: (jnp.minimum(i, n_prompt_tiles - 1), 0))
    smem_blk = lambda a, ahead: pl.BlockSpec(
        (1, 1, a.shape[2]), lambda i: (jnp.minimum(i + ahead, n_tiles - 1), 0, 0), memory_space=pltpu.SMEM)
    return pl.pallas_call(
        functools.partial(_combine_kernel, n_prompt_tiles=n_prompt_tiles),
        grid=(n_tiles,),
        in_specs=[smem_blk(lpos_tiles, 0), smem_blk(gate_tiles, 0), smem_blk(tab_tiles, 0), smem_blk(tab_tiles, 1),
                  prompt_blk, const(h1s),
                  pl.BlockSpec(memory_space=pl.ANY),
                  const(g_final)],
        out_specs=[prompt_blk, const(h1s)],
        out_shape=[jax.ShapeDtypeStruct(h1p.shape, f32), jax.ShapeDtypeStruct(h1s.shape, f32)],
        scratch_shapes=[pltpu.VMEM((2, TOP_K * TOK_TILE * ROW_SUB, LANES), f32),
                        pltpu.VMEM((TOK_TILE * ROW_SUB, LANES), f32),
                        pltpu.SemaphoreType.DMA((2,))],
        compiler_params=pltpu.CompilerParams(
            dimension_semantics=("arbitrary",), vmem_limit_bytes=VMEM_LIMIT),
        name="combine",
    )(lpos_tiles, gate_tiles, tab_tiles, tab_tiles, h1p, h1s, ys, g_final)


def _block_diag_gate_weights(wa, wx):
    heads_per_half = N_LRU_HEADS // 2
    eye = jnp.eye(heads_per_half, dtype=wa.dtype)

    def bd(w):
        return jnp.einsum("hij,hg->higj", w, eye).reshape(heads_per_half * LRU_HEAD_DIM,
                                                          heads_per_half * LRU_HEAD_DIM)

    halves = [jnp.concatenate([bd(wa[s * heads_per_half:(s + 1) * heads_per_half]),
                               bd(wx[s * heads_per_half:(s + 1) * heads_per_half])], axis=1)
              for s in range(2)]
    return jnp.stack(halves).astype(bf16)


def kernel(x_prompt, x_sample, state_conv_short, state_conv_lru, state_lru_h, meta_tokens,
           norm_mix_g, w_in, conv_short_w, conv_lru_w, conv_lru_b, w_rg_a, b_rg_a, w_rg_x, b_rg_x,
           rg_lambda, w_out, norm_ffn_g, w_router, b_router, w_gate_up, b_gate_up, w_down, b_down,
           final_norm_g):
    assert norm_mix_g.shape[0] == 1, "single layer"
    bp, seq, _ = x_prompt.shape
    bs, seq_s, _ = x_sample.shape
    n_meta = meta_tokens.shape[0]
    row = lambda a: a.reshape(1, -1).astype(f32)

    mw = dict(g=row(norm_mix_g[0]), w_in=w_in[0].astype(bf16), csw=conv_short_w[0], clw=conv_lru_w[0],
              clb=row(conv_lru_b[0]), wg=_block_diag_gate_weights(w_rg_a[0], w_rg_x[0]),
              ba=row(b_rg_a[0]), bx=row(b_rg_x[0]), lam=row(rg_lambda[0]), w_out=w_out[0].astype(bf16))

    zeros = lambda *s: jnp.zeros(s, f32)
    _, cs_m, cl_m, h_m = _mixer(meta_tokens[None].astype(f32), zeros(1, 2, W_SHORT), zeros(1, 3, W_LRU),
                                zeros(1, 1, W_LRU), mw, n_meta, 1)
    rep = lambda a: jnp.broadcast_to(a, (bp,) + a.shape[1:])
    h1p, cs_p, cl_p, h_p = _mixer(x_prompt, rep(cs_m), rep(cl_m), rep(h_m), mw, MIX_ROWS, MIX_STREAMS)
    h1s, cs_s, cl_s, h_s = _mixer(x_sample, state_conv_short[0], state_conv_lru[0],
                                  state_lru_h[0][:, None, :], mw, seq_s, MIX_STREAMS)
    h1p = h1p.reshape(bp * seq, D_MODEL)
    h1s = h1s.reshape(bs * seq_s, D_MODEL)
    n_tok = h1p.shape[0] + h1s.shape[0]

    g_ffn = row(norm_ffn_g[0])
    lpos, gates, tab, counts = _router(
        h1p, h1s, g_ffn, w_router[0].T.astype(f32), b_router[0].reshape(-1, 1).astype(f32))
    lpos, gates, tab = lpos[:, :n_tok], gates[:, :n_tok], tab[:n_tok // TOK_TILE]

    counts = counts[:, 0]
    padded = (counts + EXP_ROWS - 1) // EXP_ROWS * EXP_ROWS
    pad_end = jnp.cumsum(padded)
    pad_start = pad_end - padded
    n_blocks = -(-(n_tok * TOP_K + N_EXPERTS * (EXP_ROWS - 1)) // EXP_ROWS)
    block_first_row = jnp.arange(n_blocks, dtype=jnp.int32) * EXP_ROWS
    block_e = jnp.minimum(jnp.sum(pad_end[None, :] <= block_first_row[:, None], axis=1),
                          N_EXPERTS - 1).astype(jnp.int32)
    n_used = (pad_end[-1:] // EXP_ROWS).astype(jnp.int32)
    n_rows = n_blocks * EXP_ROWS
    fill_start = jnp.concatenate([pad_start + counts, pad_end[-1:]]).astype(jnp.int32)
    fill_len = jnp.concatenate([padded - counts, n_rows - pad_end[-1:]]).astype(jnp.int32)

    n_tiles = n_tok // TOK_TILE
    tab_tiles = jnp.concatenate([pad_start[None, :] + tab[:, :, 0], tab[:, :, 1], tab[:, :, 2],
                                 jnp.zeros((n_tiles, LANES - 3 * N_EXPERTS), jnp.int32)],
                                axis=1).reshape(n_tiles, 1, LANES)
    by_tile = lambda a: a.reshape(TOP_K, n_tiles, TOK_TILE).transpose(1, 0, 2).reshape(
        n_tiles, 1, TOP_K * TOK_TILE)

    xs = _dispatch(fill_start, fill_len, by_tile(lpos), tab_tiles, h1p, h1s, g_ffn, n_rows)
    ys = _experts(block_e, n_used, xs,
                  w_gate_up[0].astype(f32), b_gate_up[0][:, None, :].astype(f32),
                  w_down[0].astype(f32), b_down[0][:, None, :].astype(f32))
    yp, ysm = _combine(by_tile(lpos), by_tile(gates), tab_tiles, h1p, h1s, ys, row(final_norm_g))

    st = lambda a: a[None]
    return (yp.reshape(bp, seq, D_MODEL), ysm.reshape(bs, seq_s, D_MODEL),
            st(cs_p), st(cl_p), st(h_p[:, 0, :]), st(cs_s), st(cl_s), st(h_s[:, 0, :]))
```

```python
import functools

import jax
import jax.numpy as jnp
from jax import lax
from jax.experimental import pallas as pl
from jax.experimental.pallas import tpu as pltpu

D_MODEL = 1024
W_SHORT = 512
W_LRU = 512
N_LRU_HEADS = 8
LRU_HEAD_DIM = W_LRU // N_LRU_HEADS
D_IN_PROJ = 3 * W_SHORT + 2 * W_LRU
N_EXPERTS = 32
TOP_K = 4
D_FF = 1024
RG_C = 8.0
SWIGLU_LIMIT = 7.0
SWIGLU_ALPHA = 1.702
EPS = 1e-6

SUBLANES = 8
MIX_ROWS = 512
MIX_STREAMS = 2
TOK_TILE = 256
ROUTER_TILES = 2
EXP_ROWS = 768
VMEM_LIMIT = 56 * 1024 * 1024
EXPERT_VMEM_LIMIT = 60 * 1024 * 1024

f32 = jnp.float32
bf16 = jnp.bfloat16


def _rmsnorm(x, g):
    return (x * lax.rsqrt(jnp.mean(x * x, axis=-1, keepdims=True) + EPS)) * g


LANES = 128
ROW_SUB = D_MODEL // LANES


def _row_tile(ref, r):
    return ref.at[pl.ds(pl.multiple_of(r * ROW_SUB, ROW_SUB), ROW_SUB)]


def _row_tile_at(ref, first_sub_row):
    return ref.at[pl.ds(pl.multiple_of(first_sub_row, ROW_SUB), ROW_SUB)]


def _store_rows_as_tiles(ref, x, first=0):
    for j in range(ROW_SUB):
        ref[pl.ds(first * ROW_SUB + j, x.shape[0], stride=ROW_SUB), :] = x[:, j * LANES:(j + 1) * LANES]


def _load_tiles_as_rows(ref, first, n):
    return jnp.concatenate([ref[pl.ds(first * ROW_SUB + j, n, stride=ROW_SUB), :] for j in range(ROW_SUB)],
                           axis=1)


def _mixer_kernel(x_ref, cs0_ref, cl0_ref, h0_ref, g_ref, win_ref, csw_ref, clw_ref, clb_ref,
                  wg_ref, ba_ref, bx_ref, lam_ref, wout_ref,
                  h1_ref, ncs_ref, ncl_ref, nh_ref,
                  cvtails, lxtails, hcars, *, tl):
    j = pl.program_id(1)
    n_streams = x_ref.shape[0]
    n_cs, n_cl = cs0_ref.shape[1], cl0_ref.shape[1]

    @pl.when(j == 0)
    def _():
        for r in range(n_streams):
            cvtails[r] = jnp.zeros((SUBLANES, W_SHORT), f32)
            cvtails[r, SUBLANES - n_cs:, :] = cs0_ref[r]
            lxtails[r] = jnp.zeros((SUBLANES, W_LRU), f32)
            lxtails[r, SUBLANES - n_cl:, :] = cl0_ref[r]
            hcars[r] = h0_ref[r]

    ends = [_mixer_chunk(x_ref.at[r], g_ref, win_ref, csw_ref, clw_ref, clb_ref, wg_ref, ba_ref, bx_ref,
                         lam_ref, wout_ref, h1_ref.at[r], cvtails.at[r], lxtails.at[r], hcars.at[r], tl=tl)
            for r in range(n_streams)]

    @pl.when(j == pl.num_programs(1) - 1)
    def _():
        for r, (cv_end, lx_end, h_end) in enumerate(ends):
            ncs_ref[r] = cv_end[SUBLANES - n_cs:, :]
            ncl_ref[r] = lx_end[SUBLANES - n_cl:, :]
            nh_ref[r] = h_end


def _delayed(x3, tail, s, sub):
    rolled = pltpu.roll(x3, s, 1)
    before = jnp.concatenate([pltpu.roll(tail, s, 0)[None], rolled[:-1]], axis=0)
    return jnp.where(sub < s, before, rolled)


def _mixer_chunk(x_ref, g_ref, win_ref, csw_ref, clw_ref, clb_ref, wg_ref, ba_ref, bx_ref, lam_ref, wout_ref,
                 h1_ref, cvtail, lxtail, hcar, *, tl):
    n_groups = tl // SUBLANES
    grouped = lambda a: a.reshape(n_groups, SUBLANES, a.shape[-1])
    sub = lax.broadcasted_iota(jnp.int32, (n_groups, SUBLANES, W_LRU), 1)
    x = x_ref[...]
    u = _rmsnorm(x, g_ref[...])
    proj = jnp.dot(u.astype(bf16), win_ref[...], preferred_element_type=f32)
    b_g = proj[:, 0:W_SHORT]
    c_g = proj[:, W_SHORT:2 * W_SHORT]
    v = proj[:, 2 * W_SHORT:3 * W_SHORT]
    lx = proj[:, 3 * W_SHORT:3 * W_SHORT + W_LRU]
    ly = proj[:, 3 * W_SHORT + W_LRU:]

    cv = grouped(c_g * v)
    cv_tail = cvtail[...]
    conv_a = (_delayed(cv, cv_tail, 2, sub) * csw_ref[0:1, :]
              + _delayed(cv, cv_tail, 1, sub) * csw_ref[1:2, :]
              + cv * csw_ref[2:3, :])
    out_a = b_g * conv_a.reshape(tl, W_SHORT)
    cv_end = cv[n_groups - 1]
    cvtail[...] = cv_end

    lx = grouped(lx)
    lx_tail = lxtail[...]
    xc = ((_delayed(lx, lx_tail, 3, sub) * clw_ref[0:1, :]
           + _delayed(lx, lx_tail, 2, sub) * clw_ref[1:2, :]
           + _delayed(lx, lx_tail, 1, sub) * clw_ref[2:3, :]
           + lx * clw_ref[3:4, :]) + clb_ref[...]).reshape(tl, W_LRU)
    lx_end = lx[n_groups - 1]
    lxtail[...] = lx_end
    xcb = xc.astype(bf16)
    half = W_LRU // 2
    g0 = jnp.dot(xcb[:, :half], wg_ref[0], preferred_element_type=f32)
    g1 = jnp.dot(xcb[:, half:], wg_ref[1], preferred_element_type=f32)
    ga = jnp.concatenate([g0[:, :half], g1[:, :half]], axis=1) + ba_ref[...]
    gx = jnp.concatenate([g0[:, half:], g1[:, half:]], axis=1) + bx_ref[...]
    r = jax.nn.sigmoid(ga)
    ig = jax.nn.sigmoid(gx)
    z = -lam_ref[...]
    softplus = jnp.maximum(z, 0.0) + jnp.log1p(jnp.exp(-jnp.abs(z)))
    log_a = (-RG_C * r) * softplus
    a = jnp.exp(log_a)
    th = jnp.tanh(log_a)
    uu = jnp.sqrt((-2.0 * th) / (1.0 - th)) * (ig * xc)

    aa = grouped(a)
    hh = grouped(uu)
    s = 1
    while s < SUBLANES:
        h_sh = jnp.where(sub >= s, pltpu.roll(hh, s, 1), 0.0)
        a_sh = jnp.where(sub >= s, pltpu.roll(aa, s, 1), 1.0)
        hh = aa * h_sh + hh
        aa = aa * a_sh
        s *= 2
    a_last = jnp.broadcast_to(aa[:, SUBLANES - 1:, :], aa.shape)
    h_last = jnp.broadcast_to(hh[:, SUBLANES - 1:, :], hh.shape)
    carry = jnp.broadcast_to(hcar[...], (SUBLANES, W_LRU))
    groups = []
    for gi in range(n_groups):
        groups.append(aa[gi] * carry + hh[gi])
        carry = a_last[gi] * carry + h_last[gi]
    hh = jnp.concatenate(groups, axis=0)
    hcar[...] = carry[0:1, :]

    gelu = ly * (0.5 * (1.0 + jnp.tanh(0.7978845608028654 * (ly + 0.044715 * (ly * ly * ly)))))
    out_b = hh * gelu
    y = jnp.concatenate([out_a, out_b], axis=1).astype(bf16)
    h1_ref[...] = x + jnp.dot(y, wout_ref[...], preferred_element_type=f32)
    return cv_end, lx_end, hh[tl - 1:tl, :]


def _mixer(x, cs0, cl0, h0, mw, tl, streams):
    nb, seq, _ = x.shape
    assert seq % tl == 0 and tl % SUBLANES == 0 and nb % streams == 0
    full = lambda a: pl.BlockSpec(a.shape, lambda b, j: (0,) * a.ndim)
    per_b = lambda a: pl.BlockSpec((streams,) + a.shape[1:], lambda b, j: (b,) + (0,) * (a.ndim - 1))
    weights = (mw["g"], mw["w_in"], mw["csw"], mw["clw"], mw["clb"], mw["wg"], mw["ba"], mw["bx"],
               mw["lam"], mw["w_out"])
    return pl.pallas_call(
        functools.partial(_mixer_kernel, tl=tl),
        grid=(nb // streams, seq // tl),
        in_specs=[pl.BlockSpec((streams, tl, D_MODEL), lambda b, j: (b, j, 0)),
                  per_b(cs0), per_b(cl0), per_b(h0)] + [full(w) for w in weights],
        out_specs=[pl.BlockSpec((streams, tl, D_MODEL), lambda b, j: (b, j, 0)),
                   per_b(cs0), per_b(cl0), per_b(h0)],
        out_shape=[jax.ShapeDtypeStruct(x.shape, f32),
                   jax.ShapeDtypeStruct(cs0.shape, f32),
                   jax.ShapeDtypeStruct(cl0.shape, f32),
                   jax.ShapeDtypeStruct(h0.shape, f32)],
        scratch_shapes=[pltpu.VMEM((streams, SUBLANES, W_SHORT), f32),
                        pltpu.VMEM((streams, SUBLANES, W_LRU), f32),
                        pltpu.VMEM((streams, 1, W_LRU), f32)],
        compiler_params=pltpu.CompilerParams(
            dimension_semantics=("arbitrary", "arbitrary"), vmem_limit_bytes=VMEM_LIMIT),
        name=f"mixer_{seq}",
    )(x, cs0, cl0, h0, *weights)


def _route_tile(h, g, w_router_t, b_router_col):
    tr = h.shape[0]
    xn = _rmsnorm(h, g)
    logits = lax.dot_general(w_router_t.astype(bf16), xn.astype(bf16), (((1,), (1,)), ((), ())),
                             preferred_element_type=f32) + b_router_col

    rows = lax.broadcasted_iota(jnp.int32, (N_EXPERTS, tr), 0)
    vals, idxs = [], []
    cur = logits
    for _ in range(TOP_K):
        m = jnp.max(cur, axis=0, keepdims=True)
        ik = jnp.min(jnp.where(cur == m, rows, N_EXPERTS), axis=0, keepdims=True)
        vals.append(m)
        idxs.append(ik)
        cur = jnp.where(rows == ik, -jnp.inf, cur)
    ex = [jnp.exp(v - vals[0]) for v in vals]
    denom = ex[0] + ex[1] + ex[2] + ex[3]

    onehot = jnp.zeros((N_EXPERTS, tr), f32)
    for ik in idxs:
        onehot = onehot + jnp.where(rows == ik, 1.0, 0.0)
    src = lax.broadcasted_iota(jnp.int32, (tr, tr), 0)
    dst = lax.broadcasted_iota(jnp.int32, (tr, tr), 1)
    tri = jnp.where(src < dst, 1.0, 0.0).astype(bf16)
    before = jnp.dot(onehot.astype(bf16), tri, preferred_element_type=f32)
    tile_cnt = jnp.broadcast_to(jnp.sum(onehot, axis=1, keepdims=True), (N_EXPERTS, LANES))
    e_row = lax.broadcasted_iota(jnp.int32, (N_EXPERTS, N_EXPERTS), 0)
    e_col = lax.broadcasted_iota(jnp.int32, (N_EXPERTS, N_EXPERTS), 1)
    lower = jnp.where(e_col < e_row, 1.0, 0.0).astype(bf16)
    first = jnp.dot(lower, tile_cnt.astype(bf16), preferred_element_type=f32)
    within = before + first[:, 0:1]
    lpos = [ROW_SUB * jnp.sum(jnp.where(rows == ik, within, 0.0), axis=0, keepdims=True).astype(jnp.int32)
            for ik in idxs]
    return lpos, [e / denom for e in ex], tile_cnt, first


def _router_kernel(hp_ref, hs_ref, g_ref, wrt_ref, br_ref,
                   lpos_ref, gate_ref, tab_ref, cnt_ref, carry, *, n_prompt_steps):
    s = pl.program_id(0)
    tr = TOK_TILE

    @pl.when(s == 0)
    def _():
        carry[...] = jnp.zeros(carry.shape, f32)

    on_prompt = s < n_prompt_steps
    tiles = ([jnp.where(on_prompt, hp_ref[0:tr, :], hs_ref[...])]
             + [hp_ref[r * tr:(r + 1) * tr, :] for r in range(1, ROUTER_TILES)])
    routed = [_route_tile(h, g_ref[...], wrt_ref[...], br_ref[...]) for h in tiles]

    lane = lax.broadcasted_iota(jnp.int32, carry.shape, 1)
    earlier = carry[...]
    for r, (lpos, gates, tile_cnt, first) in enumerate(routed):
        for k in range(TOP_K):
            lpos_ref[k:k + 1, r * tr:(r + 1) * tr] = lpos[k]
            gate_ref[k:k + 1, r * tr:(r + 1) * tr] = gates[k]
        tab_ref[r] = jnp.where(lane == 0, earlier, jnp.where(lane == 1, tile_cnt, first)).astype(jnp.int32)
        earlier = earlier + (tile_cnt if r == 0 else jnp.where(on_prompt, tile_cnt, 0.0))
    carry[...] = earlier
    cnt_ref[...] = earlier.astype(jnp.int32)


def _router(h1p, h1s, g, w_router_t, b_router_col):
    pair = ROUTER_TILES * TOK_TILE
    assert h1p.shape[0] % pair == 0 and h1s.shape[0] == TOK_TILE
    n_prompt_steps = h1p.shape[0] // pair
    n_steps = n_prompt_steps + 1
    const = lambda a: pl.BlockSpec(a.shape, lambda s: (0,) * a.ndim)
    lane_blk = pl.BlockSpec((TOP_K, pair), lambda s: (0, s))
    return pl.pallas_call(
        functools.partial(_router_kernel, n_prompt_steps=n_prompt_steps),
        grid=(n_steps,),
        in_specs=[pl.BlockSpec((pair, D_MODEL), lambda s: (jnp.minimum(s, n_prompt_steps - 1), 0)),
                  const(h1s), const(g), const(w_router_t), const(b_router_col)],
        out_specs=[lane_blk, lane_blk,
                   pl.BlockSpec((ROUTER_TILES, N_EXPERTS, LANES), lambda s: (s, 0, 0)),
                   pl.BlockSpec((N_EXPERTS, LANES), lambda s: (0, 0))],
        out_shape=[jax.ShapeDtypeStruct((TOP_K, n_steps * pair), jnp.int32),
                   jax.ShapeDtypeStruct((TOP_K, n_steps * pair), f32),
                   jax.ShapeDtypeStruct((ROUTER_TILES * n_steps, N_EXPERTS, LANES), jnp.int32),
                   jax.ShapeDtypeStruct((N_EXPERTS, LANES), jnp.int32)],
        scratch_shapes=[pltpu.VMEM((N_EXPERTS, LANES), f32)],
        compiler_params=pltpu.CompilerParams(
            dimension_semantics=("arbitrary",), vmem_limit_bytes=VMEM_LIMIT),
        name="router",
    )(h1p, h1s, g, w_router_t, b_router_col)


def _wait_rows(n, src_ref, dst_ref, sem):
    pltpu.make_async_copy(src_ref.at[pl.ds(0, n * ROW_SUB)], dst_ref.at[pl.ds(0, n * ROW_SUB)], sem).wait()


def _rows(ref, first, n):
    return ref.at[pl.ds(pl.multiple_of(first * ROW_SUB, ROW_SUB), n * ROW_SUB)]


FILL_CHUNKS = tuple(c for c in (1 << b for b in reversed(range(EXP_ROWS.bit_length()))) if c < EXP_ROWS)
TAIL_CHUNK = min(EXP_ROWS & -EXP_ROWS, FILL_CHUNKS[0])
RUN_CHUNKS = tuple(1 << b for b in reversed(range(TOK_TILE.bit_length())))
RUN_LARGE = 64


def _start_runs(tab_ref, src_ref, dst_ref, sem, *, src_is_sorted_tile):
    for e in range(N_EXPERTS):
        hbm_first = tab_ref[0, 0, e]
        length = tab_ref[0, 0, N_EXPERTS + e]
        tile_first = tab_ref[0, 0, 2 * N_EXPERTS + e]
        src_first, dst_first = (tile_first, hbm_first) if src_is_sorted_tile else (hbm_first, tile_first)

        def start_pieces(chunks):
            for chunk in chunks:
                offset = length & ~(2 * chunk - 1)
                copy = pltpu.make_async_copy(_rows(src_ref, src_first + offset, chunk),
                                             _rows(dst_ref, dst_first + offset, chunk), sem)
                pl.when((length & chunk) != 0)(copy.start)

        large = tuple(c for c in RUN_CHUNKS if c >= RUN_LARGE)
        pl.when(length >= RUN_LARGE)(functools.partial(start_pieces, large))
        start_pieces(tuple(c for c in RUN_CHUNKS if c < RUN_LARGE))


def _zero_fill_copies(fs_ref, fl_ref, zbuf, xs_hbm, sem):
    out = []
    for e in range(N_EXPERTS):
        start, length = fs_ref[e], fl_ref[e]
        for chunk in FILL_CHUNKS:
            offset = length & ~(2 * chunk - 1)
            first = pl.multiple_of((start + offset) * ROW_SUB, ROW_SUB)
            copy = pltpu.make_async_copy(zbuf.at[pl.ds(0, chunk * ROW_SUB)],
                                         xs_hbm.at[pl.ds(first, chunk * ROW_SUB)], sem)
            out.append(((length & chunk) != 0, copy))
    start, length = fs_ref[N_EXPERTS], fl_ref[N_EXPERTS]
    chunk = TAIL_CHUNK
    for c in range(N_EXPERTS * EXP_ROWS // chunk):
        first = pl.multiple_of((start + c * chunk) * ROW_SUB, ROW_SUB)
        copy = pltpu.make_async_copy(zbuf.at[pl.ds(0, chunk * ROW_SUB)],
                                     xs_hbm.at[pl.ds(first, chunk * ROW_SUB)], sem)
        out.append((c * chunk < length, copy))
    return out


def _dispatch_kernel(fs_ref, fl_ref, lpos_ref, tab_ref, hp_ref, hs_ref, g_ref, xs_hbm,
                     xn_tiles, sorted_buf, zbuf, sem, zsem, *, n_prompt_tiles):
    i = pl.program_id(0)
    last = pl.num_programs(0) - 1
    tr = hp_ref.shape[0]
    n_rows = TOP_K * tr
    slot = i % 2

    @pl.when(i == 0)
    def _():
        zbuf[...] = jnp.zeros(zbuf.shape, f32)
        for present, copy in _zero_fill_copies(fs_ref, fl_ref, zbuf, xs_hbm, zsem):
            pl.when(present)(copy.start)

    h = jnp.where(i < n_prompt_tiles, hp_ref[...], hs_ref[...])
    _store_rows_as_tiles(xn_tiles, _rmsnorm(h, g_ref[...]))

    @pl.when(i >= 2)
    def _():
        _wait_rows(n_rows, sorted_buf.at[slot], xs_hbm, sem.at[slot])

    def place(t, carry):
        row = _row_tile(xn_tiles, t)[...]
        for k in range(TOP_K):
            _row_tile_at(sorted_buf.at[slot], lpos_ref[0, 0, k * tr + t])[...] = row
        return carry
    lax.fori_loop(0, tr, place, 0, unroll=4)

    _start_runs(tab_ref, sorted_buf.at[slot], xs_hbm, sem.at[slot], src_is_sorted_tile=True)

    @pl.when(i == last)
    def _():
        _wait_rows(n_rows, sorted_buf.at[1 - slot], xs_hbm, sem.at[1 - slot])
        _wait_rows(n_rows, sorted_buf.at[slot], xs_hbm, sem.at[slot])
        for present, copy in _zero_fill_copies(fs_ref, fl_ref, zbuf, xs_hbm, zsem):
            pl.when(present)(copy.wait)


def _dispatch(fill_start, fill_len, lpos_tiles, tab_tiles, h1p, h1s, g, n_rows):
    n_prompt_tiles = h1p.shape[0] // TOK_TILE
    n_tiles = lpos_tiles.shape[0]
    assert n_tiles >= 2
    smem_blk = lambda a: pl.BlockSpec((1, 1, a.shape[2]), lambda i, fs, fl: (i, 0, 0), memory_space=pltpu.SMEM)
    const = lambda a: pl.BlockSpec(a.shape, lambda i, fs, fl: (0,) * a.ndim)
    grid_spec = pltpu.PrefetchScalarGridSpec(
        num_scalar_prefetch=2,
        grid=(n_tiles,),
        in_specs=[smem_blk(lpos_tiles), smem_blk(tab_tiles),
                  pl.BlockSpec((TOK_TILE, D_MODEL), lambda i, fs, fl: (jnp.minimum(i, n_prompt_tiles - 1), 0)),
                  const(h1s), const(g)],
        out_specs=pl.BlockSpec(memory_space=pl.ANY),
        scratch_shapes=[pltpu.VMEM((TOK_TILE * ROW_SUB, LANES), f32),
                        pltpu.VMEM((2, TOP_K * TOK_TILE * ROW_SUB, LANES), f32),
                        pltpu.VMEM((FILL_CHUNKS[0] * ROW_SUB, LANES), f32),
                        pltpu.SemaphoreType.DMA((2,)), pltpu.SemaphoreType.DMA(())],
    )
    return pl.pallas_call(
        functools.partial(_dispatch_kernel, n_prompt_tiles=n_prompt_tiles),
        grid_spec=grid_spec,
        out_shape=jax.ShapeDtypeStruct((n_rows * ROW_SUB, LANES), f32),
        compiler_params=pltpu.CompilerParams(
            dimension_semantics=("arbitrary",), vmem_limit_bytes=VMEM_LIMIT),
        name="dispatch",
    )(fill_start, fill_len, lpos_tiles, tab_tiles, h1p, h1s, g)


def _expert_kernel(be_ref, nu_ref, xs_ref, wgu_ref, bgu_ref, wd_ref, bd_ref, ys_ref, wgu_bf, wd_bf):
    i = pl.program_id(0)
    n_used = nu_ref[0]
    new_expert = jnp.logical_or(i == 0, be_ref[i] != be_ref[jnp.maximum(i - 1, 0)])

    @pl.when(jnp.logical_and(i < n_used, new_expert))
    def _():
        wgu_bf[...] = wgu_ref[0].astype(bf16)
        wd_bf[...] = wd_ref[0].astype(bf16)

    @pl.when(i < n_used)
    def _():
        x = _load_tiles_as_rows(xs_ref, 0, EXP_ROWS).astype(bf16)
        gu = jnp.dot(x, wgu_bf[...], preferred_element_type=f32) + bgu_ref[0]
        gate = jnp.minimum(gu[:, :D_FF], SWIGLU_LIMIT)
        up = jnp.clip(gu[:, D_FF:], -SWIGLU_LIMIT, SWIGLU_LIMIT)
        glu = gate * jax.nn.sigmoid(gate * SWIGLU_ALPHA)
        act = ((up + 1.0) * glu).astype(bf16)
        _store_rows_as_tiles(ys_ref, jnp.dot(act, wd_bf[...], preferred_element_type=f32) + bd_ref[0])

    @pl.when(i >= n_used)
    def _():
        ys_ref[...] = jnp.zeros(ys_ref.shape, f32)


def _experts(block_e, n_used, xs, w_gu, b_gu, w_down, b_down):
    n_blocks = xs.shape[0] // (EXP_ROWS * ROW_SUB)
    by_expert = lambda a: pl.BlockSpec((1,) + a.shape[1:], lambda i, be, nu: (be[i],) + (0,) * (a.ndim - 1))
    grid_spec = pltpu.PrefetchScalarGridSpec(
        num_scalar_prefetch=2,
        grid=(n_blocks,),
        in_specs=[pl.BlockSpec((EXP_ROWS * ROW_SUB, LANES),
                               lambda i, be, nu: (jnp.minimum(i, jnp.maximum(nu[0] - 1, 0)), 0)),
                  by_expert(w_gu), by_expert(b_gu), by_expert(w_down), by_expert(b_down)],
        out_specs=pl.BlockSpec((EXP_ROWS * ROW_SUB, LANES), lambda i, be, nu: (i, 0)),
        scratch_shapes=[pltpu.VMEM(w_gu.shape[1:], bf16), pltpu.VMEM(w_down.shape[1:], bf16)],
    )
    return pl.pallas_call(
        _expert_kernel,
        grid_spec=grid_spec,
        out_shape=jax.ShapeDtypeStruct(xs.shape, f32),
        compiler_params=pltpu.CompilerParams(
            dimension_semantics=("arbitrary",), vmem_limit_bytes=EXPERT_VMEM_LIMIT),
        name="experts",
    )(block_e, n_used, xs, w_gu, b_gu, w_down, b_down)


def _combine_kernel(lpos_ref, gate_ref, tab_cur, tab_nxt, hp_ref, hs_ref, ys_hbm, gf_ref,
                    yp_ref, ysm_ref, ybuf, moe_tiles, sem, *, n_prompt_tiles):
    i = pl.program_id(0)
    tr = hp_ref.shape[0]
    n_rows = TOP_K * tr
    slot = i % 2

    @pl.when(i == 0)
    def _():
        _start_runs(tab_cur, ys_hbm, ybuf.at[0], sem.at[0], src_is_sorted_tile=False)

    @pl.when(i + 1 < pl.num_programs(0))
    def _():
        _start_runs(tab_nxt, ys_hbm, ybuf.at[1 - slot], sem.at[1 - slot], src_is_sorted_tile=False)

    _wait_rows(n_rows, ys_hbm, ybuf.at[slot], sem.at[slot])

    def mix(t, carry):
        acc = gate_ref[0, 0, t] * _row_tile_at(ybuf.at[slot], lpos_ref[0, 0, t])[...]
        for k in range(1, TOP_K):
            acc = acc + gate_ref[0, 0, k * tr + t] * _row_tile_at(ybuf.at[slot], lpos_ref[0, 0, k * tr + t])[...]
        _row_tile(moe_tiles, t)[...] = acc
        return carry
    lax.fori_loop(0, tr, mix, 0, unroll=4)

    h = jnp.where(i < n_prompt_tiles, hp_ref[...], hs_ref[...])
    out = _rmsnorm(h + _load_tiles_as_rows(moe_tiles, 0, tr), gf_ref[...])

    @pl.when(i < n_prompt_tiles)
    def _():
        yp_ref[...] = out

    @pl.when(i >= n_prompt_tiles)
    def _():
        ysm_ref[...] = out


def _combine(lpos_tiles, gate_tiles, tab_tiles, h1p, h1s, ys, g_final):
    n_prompt_tiles = h1p.shape[0] // TOK_TILE
    n_tiles = lpos_tiles.shape[0]
    const = lambda a: pl.BlockSpec(a.shape, lambda i: (0,) * a.ndim)
    prompt_blk = pl.BlockSpec((TOK_TILE, D_MODEL), lambda i: (jnp.minimum(i, n_prompt_tiles - 1), 0))
    smem_blk = lambda a, ahead: pl.BlockSpec(
        (1, 1, a.shape[2]), lambda i: (jnp.minimum(i + ahead, n_tiles - 1), 0, 0), memory_space=pltpu.SMEM)
    return pl.pallas_call(
        functools.partial(_combine_kernel, n_prompt_tiles=n_prompt_tiles),
        grid=(n_tiles,),
        in_specs=[smem_blk(lpos_tiles, 0), smem_blk(gate_tiles, 0), smem_blk(tab_tiles, 0), smem_blk(tab_tiles, 1),
                  prompt_blk, const(h1s),
                  pl.BlockSpec(memory_space=pl.ANY),
                  const(g_final)],
        out_specs=[prompt_blk, const(h1s)],
        out_shape=[jax.ShapeDtypeStruct(h1p.shape, f32), jax.ShapeDtypeStruct(h1s.shape, f32)],
        scratch_shapes=[pltpu.VMEM((2, TOP_K * TOK_TILE * ROW_SUB, LANES), f32),
                        pltpu.VMEM((TOK_TILE * ROW_SUB, LANES), f32),
                        pltpu.SemaphoreType.DMA((2,))],
        compiler_params=pltpu.CompilerParams(
            dimension_semantics=("arbitrary",), vmem_limit_bytes=VMEM_LIMIT),
        name="combine",
    )(lpos_tiles, gate_tiles, tab_tiles, tab_tiles, h1p, h1s, ys, g_final)


def _block_diag_gate_weights(wa, wx):
    heads_per_half = N_LRU_HEADS // 2
    eye = jnp.eye(heads_per_half, dtype=wa.dtype)

    def bd(w):
        return jnp.einsum("hij,hg->higj", w, eye).reshape(heads_per_half * LRU_HEAD_DIM,
                                                          heads_per_half * LRU_HEAD_DIM)

    halves = [jnp.concatenate([bd(wa[s * heads_per_half:(s + 1) * heads_per_half]),
                               bd(wx[s * heads_per_half:(s + 1) * heads_per_half])], axis=1)
              for s in range(2)]
    return jnp.stack(halves).astype(bf16)


def kernel(x_prompt, x_sample, state_conv_short, state_conv_lru, state_lru_h, meta_tokens,
           norm_mix_g, w_in, conv_short_w, conv_lru_w, conv_lru_b, w_rg_a, b_rg_a, w_rg_x, b_rg_x,
           rg_lambda, w_out, norm_ffn_g, w_router, b_router, w_gate_up, b_gate_up, w_down, b_down,
           final_norm_g):
    assert norm_mix_g.shape[0] == 1, "single layer"
    bp, seq, _ = x_prompt.shape
    bs, seq_s, _ = x_sample.shape
    n_meta = meta_tokens.shape[0]
    row = lambda a: a.reshape(1, -1).astype(f32)

    mw = dict(g=row(norm_mix_g[0]), w_in=w_in[0].astype(bf16), csw=conv_short_w[0], clw=conv_lru_w[0],
              clb=row(conv_lru_b[0]), wg=_block_diag_gate_weights(w_rg_a[0], w_rg_x[0]),
              ba=row(b_rg_a[0]), bx=row(b_rg_x[0]), lam=row(rg_lambda[0]), w_out=w_out[0].astype(bf16))

    zeros = lambda *s: jnp.zeros(s, f32)
    _, cs_m, cl_m, h_m = _mixer(meta_tokens[None].astype(f32), zeros(1, 2, W_SHORT), zeros(1, 3, W_LRU),
                                zeros(1, 1, W_LRU), mw, n_meta, 1)
    rep = lambda a: jnp.broadcast_to(a, (bp,) + a.shape[1:])
    h1p, cs_p, cl_p, h_p = _mixer(x_prompt, rep(cs_m), rep(cl_m), rep(h_m), mw, MIX_ROWS, MIX_STREAMS)
    h1s, cs_s, cl_s, h_s = _mixer(x_sample, state_conv_short[0], state_conv_lru[0],
                                  state_lru_h[0][:, None, :], mw, seq_s, MIX_STREAMS)
    h1p = h1p.reshape(bp * seq, D_MODEL)
    h1s = h1s.reshape(bs * seq_s, D_MODEL)
    n_tok = h1p.shape[0] + h1s.shape[0]

    g_ffn = row(norm_ffn_g[0])
    lpos, gates, tab, counts = _router(
        h1p, h1s, g_ffn, w_router[0].T.astype(f32), b_router[0].reshape(-1, 1).astype(f32))
    lpos, gates, tab = lpos[:, :n_tok], gates[:, :n_tok], tab[:n_tok // TOK_TILE]

    counts = counts[:, 0]
    padded = (counts + EXP_ROWS - 1) // EXP_ROWS * EXP_ROWS
    pad_end = jnp.cumsum(padded)
    pad_start = pad_end - padded
    n_blocks = -(-(n_tok * TOP_K + N_EXPERTS * (EXP_ROWS - 1)) // EXP_ROWS)
    block_first_row = jnp.arange(n_blocks, dtype=jnp.int32) * EXP_ROWS
    block_e = jnp.minimum(jnp.sum(pad_end[None, :] <= block_first_row[:, None], axis=1),
                          N_EXPERTS - 1).astype(jnp.int32)
    n_used = (pad_end[-1:] // EXP_ROWS).astype(jnp.int32)
    n_rows = n_blocks * EXP_ROWS
    fill_start = jnp.concatenate([pad_start + counts, pad_end[-1:]]).astype(jnp.int32)
    fill_len = jnp.concatenate([padded - counts, n_rows - pad_end[-1:]]).astype(jnp.int32)

    n_tiles = n_tok // TOK_TILE
    tab_tiles = jnp.concatenate([pad_start[None, :] + tab[:, :, 0], tab[:, :, 1], tab[:, :, 2],
                                 jnp.zeros((n_tiles, LANES - 3 * N_EXPERTS), jnp.int32)],
                                axis=1).reshape(n_tiles, 1, LANES)
    by_tile = lambda a: a.reshape(TOP_K, n_tiles, TOK_TILE).transpose(1, 0, 2).reshape(
        n_tiles, 1, TOP_K * TOK_TILE)

    xs = _dispatch(fill_start, fill_len, by_tile(lpos), tab_tiles, h1p, h1s, g_ffn, n_rows)
    ys = _experts(block_e, n_used, xs,
                  w_gate_up[0].astype(f32), b_gate_up[0][:, None, :].astype(f32),
                  w_down[0].astype(f32), b_down[0][:, None, :].astype(f32))
    yp, ysm = _combine(by_tile(lpos), by_tile(gates), tab_tiles, h1p, h1s, ys, row(final_norm_g))

    st = lambda a: a[None]
    return (yp.reshape(bp, seq, D_MODEL), ysm.reshape(bs, seq_s, D_MODEL),
            st(cs_p), st(cl_p), st(h_p[:, 0, :]), st(cs_s), st(cl_s), st(h_s[:, 0, :]))
```

```python
import functools

import jax
import jax.numpy as jnp
from jax import lax
from jax.experimental import pallas as pl
from jax.experimental.pallas import tpu as pltpu

D_MODEL = 1024
W_SHORT = 512
W_LRU = 512
N_LRU_HEADS = 8
LRU_HEAD_DIM = W_LRU // N_LRU_HEADS
D_IN_PROJ = 3 * W_SHORT + 2 * W_LRU
N_EXPERTS = 32
TOP_K = 4
D_FF = 1024
RG_C = 8.0
SWIGLU_LIMIT = 7.0
SWIGLU_ALPHA = 1.702
EPS = 1e-6

SUBLANES = 8
MIX_ROWS = 512
MIX_STREAMS = 2
TOK_TILE = 512
LOOP_UNROLL = 4
EXP_ROWS = 768
VMEM_LIMIT = 56 * 1024 * 1024
EXPERT_VMEM_LIMIT = 60 * 1024 * 1024

f32 = jnp.float32
bf16 = jnp.bfloat16


def _rmsnorm(x, g):
    return (x * lax.rsqrt(jnp.mean(x * x, axis=-1, keepdims=True) + EPS)) * g


LANES = 128
ROW_SUB = D_MODEL // LANES


def _row_tile(ref, r):
    return ref.at[pl.ds(pl.multiple_of(r * ROW_SUB, ROW_SUB), ROW_SUB)]


def _row_tile_at(ref, first_sub_row):
    return ref.at[pl.ds(pl.multiple_of(first_sub_row, ROW_SUB), ROW_SUB)]


def _store_rows_as_tiles(ref, x, first=0):
    for j in range(ROW_SUB):
        ref[pl.ds(first * ROW_SUB + j, x.shape[0], stride=ROW_SUB), :] = x[:, j * LANES:(j + 1) * LANES]


def _load_tiles_as_rows(ref, first, n):
    return jnp.concatenate([ref[pl.ds(first * ROW_SUB + j, n, stride=ROW_SUB), :] for j in range(ROW_SUB)],
                           axis=1)


def _mixer_kernel(x_ref, cs0_ref, cl0_ref, h0_ref, g_ref, win_ref, csw_ref, clw_ref, clb_ref,
                  wg_ref, ba_ref, bx_ref, lam_ref, wout_ref,
                  h1_ref, ncs_ref, ncl_ref, nh_ref,
                  cvtails, lxtails, hcars, *, tl):
    j = pl.program_id(1)
    n_streams = x_ref.shape[0]
    n_cs, n_cl = cs0_ref.shape[1], cl0_ref.shape[1]

    @pl.when(j == 0)
    def _():
        for r in range(n_streams):
            cvtails[r] = jnp.zeros((SUBLANES, W_SHORT), f32)
            cvtails[r, SUBLANES - n_cs:, :] = cs0_ref[r]
            lxtails[r] = jnp.zeros((SUBLANES, W_LRU), f32)
            lxtails[r, SUBLANES - n_cl:, :] = cl0_ref[r]
            hcars[r] = h0_ref[r]

    ends = [_mixer_chunk(x_ref.at[r], g_ref, win_ref, csw_ref, clw_ref, clb_ref, wg_ref, ba_ref, bx_ref,
                         lam_ref, wout_ref, h1_ref.at[r], cvtails.at[r], lxtails.at[r], hcars.at[r], tl=tl)
            for r in range(n_streams)]

    @pl.when(j == pl.num_programs(1) - 1)
    def _():
        for r, (cv_end, lx_end, h_end) in enumerate(ends):
            ncs_ref[r] = cv_end[SUBLANES - n_cs:, :]
            ncl_ref[r] = lx_end[SUBLANES - n_cl:, :]
            nh_ref[r] = h_end


def _delayed(x3, tail, s, sub):
    rolled = pltpu.roll(x3, s, 1)
    before = jnp.concatenate([pltpu.roll(tail, s, 0)[None], rolled[:-1]], axis=0)
    return jnp.where(sub < s, before, rolled)


def _mixer_chunk(x_ref, g_ref, win_ref, csw_ref, clw_ref, clb_ref, wg_ref, ba_ref, bx_ref, lam_ref, wout_ref,
                 h1_ref, cvtail, lxtail, hcar, *, tl):
    n_groups = tl // SUBLANES
    grouped = lambda a: a.reshape(n_groups, SUBLANES, a.shape[-1])
    sub = lax.broadcasted_iota(jnp.int32, (n_groups, SUBLANES, W_LRU), 1)
    x = x_ref[...]
    u = _rmsnorm(x, g_ref[...])
    proj = jnp.dot(u.astype(bf16), win_ref[...], preferred_element_type=f32)
    b_g = proj[:, 0:W_SHORT]
    c_g = proj[:, W_SHORT:2 * W_SHORT]
    v = proj[:, 2 * W_SHORT:3 * W_SHORT]
    lx = proj[:, 3 * W_SHORT:3 * W_SHORT + W_LRU]
    ly = proj[:, 3 * W_SHORT + W_LRU:]

    cv = grouped(c_g * v)
    cv_tail = cvtail[...]
    conv_a = (_delayed(cv, cv_tail, 2, sub) * csw_ref[0:1, :]
              + _delayed(cv, cv_tail, 1, sub) * csw_ref[1:2, :]
              + cv * csw_ref[2:3, :])
    out_a = b_g * conv_a.reshape(tl, W_SHORT)
    cv_end = cv[n_groups - 1]
    cvtail[...] = cv_end

    lx = grouped(lx)
    lx_tail = lxtail[...]
    xc = ((_delayed(lx, lx_tail, 3, sub) * clw_ref[0:1, :]
           + _delayed(lx, lx_tail, 2, sub) * clw_ref[1:2, :]
           + _delayed(lx, lx_tail, 1, sub) * clw_ref[2:3, :]
           + lx * clw_ref[3:4, :]) + clb_ref[...]).reshape(tl, W_LRU)
    lx_end = lx[n_groups - 1]
    lxtail[...] = lx_end
    xcb = xc.astype(bf16)
    half = W_LRU // 2
    g0 = jnp.dot(xcb[:, :half], wg_ref[0], preferred_element_type=f32)
    g1 = jnp.dot(xcb[:, half:], wg_ref[1], preferred_element_type=f32)
    ga = jnp.concatenate([g0[:, :half], g1[:, :half]], axis=1) + ba_ref[...]
    gx = jnp.concatenate([g0[:, half:], g1[:, half:]], axis=1) + bx_ref[...]
    r = jax.nn.sigmoid(ga)
    ig = jax.nn.sigmoid(gx)
    z = -lam_ref[...]
    softplus = jnp.maximum(z, 0.0) + jnp.log1p(jnp.exp(-jnp.abs(z)))
    log_a = (-RG_C * r) * softplus
    a = jnp.exp(log_a)
    th = jnp.tanh(log_a)
    uu = jnp.sqrt((-2.0 * th) / (1.0 - th)) * (ig * xc)

    aa = grouped(a)
    hh = grouped(uu)
    s = 1
    while s < SUBLANES:
        h_sh = jnp.where(sub >= s, pltpu.roll(hh, s, 1), 0.0)
        a_sh = jnp.where(sub >= s, pltpu.roll(aa, s, 1), 1.0)
        hh = aa * h_sh + hh
        aa = aa * a_sh
        s *= 2
    a_last = jnp.broadcast_to(aa[:, SUBLANES - 1:, :], aa.shape)
    h_last = jnp.broadcast_to(hh[:, SUBLANES - 1:, :], hh.shape)
    carry = jnp.broadcast_to(hcar[...], (SUBLANES, W_LRU))
    groups = []
    for gi in range(n_groups):
        groups.append(aa[gi] * carry + hh[gi])
        carry = a_last[gi] * carry + h_last[gi]
    hh = jnp.concatenate(groups, axis=0)
    hcar[...] = carry[0:1, :]

    gelu = ly * (0.5 * (1.0 + jnp.tanh(0.7978845608028654 * (ly + 0.044715 * (ly * ly * ly)))))
    out_b = hh * gelu
    y = jnp.concatenate([out_a, out_b], axis=1).astype(bf16)
    h1_ref[...] = x + jnp.dot(y, wout_ref[...], preferred_element_type=f32)
    return cv_end, lx_end, hh[tl - 1:tl, :]


def _mixer(x, cs0, cl0, h0, mw, tl, streams):
    nb, seq, _ = x.shape
    assert seq % tl == 0 and tl % SUBLANES == 0 and nb % streams == 0
    full = lambda a: pl.BlockSpec(a.shape, lambda b, j: (0,) * a.ndim)
    per_b = lambda a: pl.BlockSpec((streams,) + a.shape[1:], lambda b, j: (b,) + (0,) * (a.ndim - 1))
    weights = (mw["g"], mw["w_in"], mw["csw"], mw["clw"], mw["clb"], mw["wg"], mw["ba"], mw["bx"],
               mw["lam"], mw["w_out"])
    return pl.pallas_call(
        functools.partial(_mixer_kernel, tl=tl),
        grid=(nb // streams, seq // tl),
        in_specs=[pl.BlockSpec((streams, tl, D_MODEL), lambda b, j: (b, j, 0)),
                  per_b(cs0), per_b(cl0), per_b(h0)] + [full(w) for w in weights],
        out_specs=[pl.BlockSpec((streams, tl, D_MODEL), lambda b, j: (b, j, 0)),
                   per_b(cs0), per_b(cl0), per_b(h0)],
        out_shape=[jax.ShapeDtypeStruct(x.shape, f32),
                   jax.ShapeDtypeStruct(cs0.shape, f32),
                   jax.ShapeDtypeStruct(cl0.shape, f32),
                   jax.ShapeDtypeStruct(h0.shape, f32)],
        scratch_shapes=[pltpu.VMEM((streams, SUBLANES, W_SHORT), f32),
                        pltpu.VMEM((streams, SUBLANES, W_LRU), f32),
                        pltpu.VMEM((streams, 1, W_LRU), f32)],
        compiler_params=pltpu.CompilerParams(
            dimension_semantics=("arbitrary", "arbitrary"), vmem_limit_bytes=VMEM_LIMIT),
        name=f"mixer_{seq}",
    )(x, cs0, cl0, h0, *weights)


def _route_tile(h, n_valid, g, w_router_t, b_router_col):
    tr = h.shape[0]
    xn = _rmsnorm(h, g)
    logits = lax.dot_general(w_router_t.astype(bf16), xn.astype(bf16), (((1,), (1,)), ((), ())),
                             preferred_element_type=f32) + b_router_col

    rows = lax.broadcasted_iota(jnp.int32, (N_EXPERTS, tr), 0)
    vals, idxs = [], []
    cur = logits
    for _ in range(TOP_K):
        m = jnp.max(cur, axis=0, keepdims=True)
        ik = jnp.min(jnp.where(cur == m, rows, N_EXPERTS), axis=0, keepdims=True)
        vals.append(m)
        idxs.append(ik)
        cur = jnp.where(rows == ik, -jnp.inf, cur)
    ex = [jnp.exp(v - vals[0]) for v in vals]
    denom = ex[0] + ex[1] + ex[2] + ex[3]

    routed = lax.broadcasted_iota(jnp.int32, (N_EXPERTS, tr), 1) < n_valid
    onehot = jnp.zeros((N_EXPERTS, tr), f32)
    for ik in idxs:
        onehot = onehot + jnp.where(jnp.logical_and(rows == ik, routed), 1.0, 0.0)
    src = lax.broadcasted_iota(jnp.int32, (tr, tr), 0)
    dst = lax.broadcasted_iota(jnp.int32, (tr, tr), 1)
    tri = jnp.where(src < dst, 1.0, 0.0).astype(bf16)
    before = jnp.dot(onehot.astype(bf16), tri, preferred_element_type=f32)
    tile_cnt = jnp.broadcast_to(jnp.sum(onehot, axis=1, keepdims=True), (N_EXPERTS, LANES))
    e_row = lax.broadcasted_iota(jnp.int32, (N_EXPERTS, LANES), 0)
    upto = tile_cnt
    s = 1
    while s < N_EXPERTS:
        upto = upto + jnp.where(e_row >= s, pltpu.roll(upto, s, 0), 0.0)
        s *= 2
    first = upto - tile_cnt
    within = before + first[:, 0:1]
    lpos = [ROW_SUB * jnp.sum(jnp.where(rows == ik, within, 0.0), axis=0, keepdims=True).astype(jnp.int32)
            for ik in idxs]
    return lpos, [e / denom for e in ex], tile_cnt, first


def _token_tile(i, n_prompt_tiles, hp_ref, hs_ref):
    n_sample = hs_ref.shape[0]
    sample = jnp.concatenate([hs_ref[...]] * (TOK_TILE // n_sample), axis=0)
    on_prompt = i < n_prompt_tiles
    return jnp.where(on_prompt, hp_ref[...], sample), jnp.where(on_prompt, TOK_TILE, n_sample)


def _router_kernel(hp_ref, hs_ref, g_ref, wrt_ref, br_ref,
                   lpos_ref, gate_ref, tab_ref, cnt_ref, carry, *, n_prompt_tiles):
    i = pl.program_id(0)

    @pl.when(i == 0)
    def _():
        carry[...] = jnp.zeros(carry.shape, f32)

    h, n_valid = _token_tile(i, n_prompt_tiles, hp_ref, hs_ref)
    lpos, gates, tile_cnt, first = _route_tile(h, n_valid, g_ref[...], wrt_ref[...], br_ref[...])
    for k in range(TOP_K):
        lpos_ref[k:k + 1, :] = lpos[k]
        gate_ref[k:k + 1, :] = gates[k]
    lane = lax.broadcasted_iota(jnp.int32, carry.shape, 1)
    tab_ref[0] = jnp.where(lane == 0, carry[...], jnp.where(lane == 1, tile_cnt, first)).astype(jnp.int32)
    carry[...] = carry[...] + tile_cnt
    cnt_ref[...] = carry[...].astype(jnp.int32)


def _router(h1p, h1s, g, w_router_t, b_router_col):
    assert h1p.shape[0] % TOK_TILE == 0 and TOK_TILE % h1s.shape[0] == 0 and h1s.shape[0] % 4 == 0
    n_prompt_tiles = h1p.shape[0] // TOK_TILE
    n_steps = n_prompt_tiles + 1
    const = lambda a: pl.BlockSpec(a.shape, lambda s: (0,) * a.ndim)
    lane_blk = pl.BlockSpec((TOP_K, TOK_TILE), lambda s: (0, s))
    return pl.pallas_call(
        functools.partial(_router_kernel, n_prompt_tiles=n_prompt_tiles),
        grid=(n_steps,),
        in_specs=[pl.BlockSpec((TOK_TILE, D_MODEL), lambda s: (jnp.minimum(s, n_prompt_tiles - 1), 0)),
                  const(h1s), const(g), const(w_router_t), const(b_router_col)],
        out_specs=[lane_blk, lane_blk,
                   pl.BlockSpec((1, N_EXPERTS, LANES), lambda s: (s, 0, 0)),
                   pl.BlockSpec((N_EXPERTS, LANES), lambda s: (0, 0))],
        out_shape=[jax.ShapeDtypeStruct((TOP_K, n_steps * TOK_TILE), jnp.int32),
                   jax.ShapeDtypeStruct((TOP_K, n_steps * TOK_TILE), f32),
                   jax.ShapeDtypeStruct((n_steps, N_EXPERTS, LANES), jnp.int32),
                   jax.ShapeDtypeStruct((N_EXPERTS, LANES), jnp.int32)],
        scratch_shapes=[pltpu.VMEM((N_EXPERTS, LANES), f32)],
        compiler_params=pltpu.CompilerParams(
            dimension_semantics=("arbitrary",), vmem_limit_bytes=VMEM_LIMIT),
        name="router",
    )(h1p, h1s, g, w_router_t, b_router_col)


def _wait_rows(n, src_ref, dst_ref, sem):
    pltpu.make_async_copy(src_ref.at[pl.ds(0, n * ROW_SUB)], dst_ref.at[pl.ds(0, n * ROW_SUB)], sem).wait()


def _rows(ref, first, n):
    return ref.at[pl.ds(pl.multiple_of(first * ROW_SUB, ROW_SUB), n * ROW_SUB)]


FILL_CHUNKS = tuple(c for c in (1 << b for b in reversed(range(EXP_ROWS.bit_length()))) if c < EXP_ROWS)
TAIL_CHUNK = min(EXP_ROWS & -EXP_ROWS, FILL_CHUNKS[0])
RUN_CHUNKS = tuple(1 << b for b in reversed(range(TOK_TILE.bit_length())))
RUN_LARGE = 64


def _start_runs(tab_ref, src_ref, dst_ref, sem, *, src_is_sorted_tile):
    for e in range(N_EXPERTS):
        hbm_first = tab_ref[0, 0, e]
        length = tab_ref[0, 0, N_EXPERTS + e]
        tile_first = tab_ref[0, 0, 2 * N_EXPERTS + e]
        src_first, dst_first = (tile_first, hbm_first) if src_is_sorted_tile else (hbm_first, tile_first)

        def start_pieces(chunks):
            for chunk in chunks:
                offset = length & ~(2 * chunk - 1)
                copy = pltpu.make_async_copy(_rows(src_ref, src_first + offset, chunk),
                                             _rows(dst_ref, dst_first + offset, chunk), sem)
                pl.when((length & chunk) != 0)(copy.start)

        large = tuple(c for c in RUN_CHUNKS if c >= RUN_LARGE)
        pl.when(length >= RUN_LARGE)(functools.partial(start_pieces, large))
        start_pieces(tuple(c for c in RUN_CHUNKS if c < RUN_LARGE))


def _zero_fill_copies(fs_ref, fl_ref, zbuf, xs_hbm, sem):
    out = []
    for e in range(N_EXPERTS):
        start, length = fs_ref[e], fl_ref[e]
        for chunk in FILL_CHUNKS:
            offset = length & ~(2 * chunk - 1)
            first = pl.multiple_of((start + offset) * ROW_SUB, ROW_SUB)
            copy = pltpu.make_async_copy(zbuf.at[pl.ds(0, chunk * ROW_SUB)],
                                         xs_hbm.at[pl.ds(first, chunk * ROW_SUB)], sem)
            out.append(((length & chunk) != 0, copy))
    start, length = fs_ref[N_EXPERTS], fl_ref[N_EXPERTS]
    chunk = TAIL_CHUNK
    for c in range(N_EXPERTS * EXP_ROWS // chunk):
        first = pl.multiple_of((start + c * chunk) * ROW_SUB, ROW_SUB)
        copy = pltpu.make_async_copy(zbuf.at[pl.ds(0, chunk * ROW_SUB)],
                                     xs_hbm.at[pl.ds(first, chunk * ROW_SUB)], sem)
        out.append((c * chunk < length, copy))
    return out


def _dispatch_kernel(fs_ref, fl_ref, lpos_ref, tab_ref, hp_ref, hs_ref, g_ref, xs_hbm,
                     xn_tiles, sorted_buf, zbuf, sem, zsem, *, n_prompt_tiles):
    i = pl.program_id(0)
    last = pl.num_programs(0) - 1
    tr = hp_ref.shape[0]
    n_rows = TOP_K * tr
    slot = i % 2

    @pl.when(i == 0)
    def _():
        zbuf[...] = jnp.zeros(zbuf.shape, f32)
        for present, copy in _zero_fill_copies(fs_ref, fl_ref, zbuf, xs_hbm, zsem):
            pl.when(present)(copy.start)

    h, n_valid = _token_tile(i, n_prompt_tiles, hp_ref, hs_ref)
    _store_rows_as_tiles(xn_tiles, _rmsnorm(h, g_ref[...]))

    @pl.when(i >= 2)
    def _():
        _wait_rows(n_rows, sorted_buf.at[slot], xs_hbm, sem.at[slot])

    def place(group, carry):
        for u in range(LOOP_UNROLL):
            t = group * LOOP_UNROLL + u
            row = _row_tile(xn_tiles, t)[...]
            for k in range(TOP_K):
                _row_tile_at(sorted_buf.at[slot], lpos_ref[0, 0, k * tr + t])[...] = row
        return carry
    lax.fori_loop(0, n_valid // LOOP_UNROLL, place, 0)

    _start_runs(tab_ref, sorted_buf.at[slot], xs_hbm, sem.at[slot], src_is_sorted_tile=True)

    @pl.when(i == last)
    def _():
        _wait_rows(n_rows, sorted_buf.at[1 - slot], xs_hbm, sem.at[1 - slot])
        _wait_rows(TOP_K * hs_ref.shape[0], sorted_buf.at[slot], xs_hbm, sem.at[slot])
        for present, copy in _zero_fill_copies(fs_ref, fl_ref, zbuf, xs_hbm, zsem):
            pl.when(present)(copy.wait)


def _dispatch(fill_start, fill_len, lpos_tiles, tab_tiles, h1p, h1s, g, n_rows):
    n_prompt_tiles = h1p.shape[0] // TOK_TILE
    n_tiles = lpos_tiles.shape[0]
    assert n_tiles >= 2
    smem_blk = lambda a: pl.BlockSpec((1, 1, a.shape[2]), lambda i, fs, fl: (i, 0, 0), memory_space=pltpu.SMEM)
    const = lambda a: pl.BlockSpec(a.shape, lambda i, fs, fl: (0,) * a.ndim)
    grid_spec = pltpu.PrefetchScalarGridSpec(
        num_scalar_prefetch=2,
        grid=(n_tiles,),
        in_specs=[smem_blk(lpos_tiles), smem_blk(tab_tiles),
                  pl.BlockSpec((TOK_TILE, D_MODEL), lambda i, fs, fl: (jnp.minimum(i, n_prompt_tiles - 1), 0)),
                  const(h1s), const(g)],
        out_specs=pl.BlockSpec(memory_space=pl.ANY),
        scratch_shapes=[pltpu.VMEM((TOK_TILE * ROW_SUB, LANES), f32),
                        pltpu.VMEM((2, TOP_K * TOK_TILE * ROW_SUB, LANES), f32),
                        pltpu.VMEM((FILL_CHUNKS[0] * ROW_SUB, LANES), f32),
                        pltpu.SemaphoreType.DMA((2,)), pltpu.SemaphoreType.DMA(())],
    )
    return pl.pallas_call(
        functools.partial(_dispatch_kernel, n_prompt_tiles=n_prompt_tiles),
        grid_spec=grid_spec,
        out_shape=jax.ShapeDtypeStruct((n_rows * ROW_SUB, LANES), f32),
        compiler_params=pltpu.CompilerParams(
            dimension_semantics=("arbitrary",), vmem_limit_bytes=VMEM_LIMIT),
        name="dispatch",
    )(fill_start, fill_len, lpos_tiles, tab_tiles, h1p, h1s, g)


def _expert_kernel(be_ref, nu_ref, xs_ref, wgu_ref, bgu_ref, wd_ref, bd_ref, ys_ref, wgu_bf, wd_bf):
    i = pl.program_id(0)
    n_used = nu_ref[0]
    new_expert = jnp.logical_or(i == 0, be_ref[i] != be_ref[jnp.maximum(i - 1, 0)])

    @pl.when(jnp.logical_and(i < n_used, new_expert))
    def _():
        wgu_bf[...] = wgu_ref[0].astype(bf16)
        wd_bf[...] = wd_ref[0].astype(bf16)

    @pl.when(i < n_used)
    def _():
        x = _load_tiles_as_rows(xs_ref, 0, EXP_ROWS).astype(bf16)
        gu = jnp.dot(x, wgu_bf[...], preferred_element_type=f32) + bgu_ref[0]
        gate = jnp.minimum(gu[:, :D_FF], SWIGLU_LIMIT)
        up = jnp.clip(gu[:, D_FF:], -SWIGLU_LIMIT, SWIGLU_LIMIT)
        glu = gate * jax.nn.sigmoid(gate * SWIGLU_ALPHA)
        act = ((up + 1.0) * glu).astype(bf16)
        _store_rows_as_tiles(ys_ref, jnp.dot(act, wd_bf[...], preferred_element_type=f32) + bd_ref[0])

    @pl.when(i >= n_used)
    def _():
        ys_ref[...] = jnp.zeros(ys_ref.shape, f32)


def _experts(block_e, n_used, xs, w_gu, b_gu, w_down, b_down):
    n_blocks = xs.shape[0] // (EXP_ROWS * ROW_SUB)
    by_expert = lambda a: pl.BlockSpec((1,) + a.shape[1:], lambda i, be, nu: (be[i],) + (0,) * (a.ndim - 1))
    grid_spec = pltpu.PrefetchScalarGridSpec(
        num_scalar_prefetch=2,
        grid=(n_blocks,),
        in_specs=[pl.BlockSpec((EXP_ROWS * ROW_SUB, LANES),
                               lambda i, be, nu: (jnp.minimum(i, jnp.maximum(nu[0] - 1, 0)), 0)),
                  by_expert(w_gu), by_expert(b_gu), by_expert(w_down), by_expert(b_down)],
        out_specs=pl.BlockSpec((EXP_ROWS * ROW_SUB, LANES), lambda i, be, nu: (i, 0)),
        scratch_shapes=[pltpu.VMEM(w_gu.shape[1:], bf16), pltpu.VMEM(w_down.shape[1:], bf16)],
    )
    return pl.pallas_call(
        _expert_kernel,
        grid_spec=grid_spec,
        out_shape=jax.ShapeDtypeStruct(xs.shape, f32),
        compiler_params=pltpu.CompilerParams(
            dimension_semantics=("arbitrary",), vmem_limit_bytes=EXPERT_VMEM_LIMIT),
        name="experts",
    )(block_e, n_used, xs, w_gu, b_gu, w_down, b_down)


def _combine_kernel(lpos_ref, gate_ref, tab_cur, tab_nxt, hp_ref, hs_ref, ys_hbm, gf_ref,
                    yp_ref, ysm_ref, ybuf, moe_tiles, sem, *, n_prompt_tiles):
    i = pl.program_id(0)
    tr = hp_ref.shape[0]
    n_rows = TOP_K * tr
    slot = i % 2

    @pl.when(i == 0)
    def _():
        _start_runs(tab_cur, ys_hbm, ybuf.at[0], sem.at[0], src_is_sorted_tile=False)

    @pl.when(i + 1 < pl.num_programs(0))
    def _():
        _start_runs(tab_nxt, ys_hbm, ybuf.at[1 - slot], sem.at[1 - slot], src_is_sorted_tile=False)

    n_sample = hs_ref.shape[0]
    h, n_valid = _token_tile(i, n_prompt_tiles, hp_ref, hs_ref)

    @pl.when(i < n_prompt_tiles)
    def _():
        _wait_rows(n_rows, ys_hbm, ybuf.at[slot], sem.at[slot])

    @pl.when(i >= n_prompt_tiles)
    def _():
        _wait_rows(TOP_K * n_sample, ys_hbm, ybuf.at[slot], sem.at[slot])

    def mix(group, carry):
        for u in range(LOOP_UNROLL):
            t = group * LOOP_UNROLL + u
            acc = gate_ref[0, 0, t] * _row_tile_at(ybuf.at[slot], lpos_ref[0, 0, t])[...]
            for k in range(1, TOP_K):
                acc = acc + (gate_ref[0, 0, k * tr + t]
                             * _row_tile_at(ybuf.at[slot], lpos_ref[0, 0, k * tr + t])[...])
            _row_tile(moe_tiles, t)[...] = acc
        return carry
    lax.fori_loop(0, n_valid // LOOP_UNROLL, mix, 0)

    out = _rmsnorm(h + _load_tiles_as_rows(moe_tiles, 0, tr), gf_ref[...])

    @pl.when(i < n_prompt_tiles)
    def _():
        yp_ref[...] = out

    @pl.when(i >= n_prompt_tiles)
    def _():
        ysm_ref[...] = out[0:n_sample, :]


def _combine(lpos_tiles, gate_tiles, tab_tiles, h1p, h1s, ys, g_final):
    n_prompt_tiles = h1p.shape[0] // TOK_TILE
    n_tiles = lpos_tiles.shape[0]
    const = lambda a: pl.BlockSpec(a.shape, lambda i: (0,) * a.ndim)
    prompt_blk = pl.BlockSpec((TOK_TILE, D_MODEL), lambda i: (jnp.minimum(i, n_prompt_tiles - 1), 0))
    smem_blk = lambda a, ahead: pl.BlockSpec(
        (1, 1, a.shape[2]), lambda i: (jnp.minimum(i + ahead, n_tiles - 1), 0, 0), memory_space=pltpu.SMEM)
    return pl.pallas_call(
        functools.partial(_combine_kernel, n_prompt_tiles=n_prompt_tiles),
        grid=(n_tiles,),
        in_specs=[smem_blk(lpos_tiles, 0), smem_blk(gate_tiles, 0), smem_blk(tab_tiles, 0), smem_blk(tab_tiles, 1),
                  prompt_blk, const(h1s),
                  pl.BlockSpec(memory_space=pl.ANY),
                  const(g_final)],
        out_specs=[prompt_blk, const(h1s)],
        out_shape=[jax.ShapeDtypeStruct(h1p.shape, f32), jax.ShapeDtypeStruct(h1s.shape, f32)],
        scratch_shapes=[pltpu.VMEM((2, TOP_K * TOK_TILE * ROW_SUB, LANES), f32),
                        pltpu.VMEM((TOK_TILE * ROW_SUB, LANES), f32),
                        pltpu.SemaphoreType.DMA((2,))],
        compiler_params=pltpu.CompilerParams(
            dimension_semantics=("arbitrary",), vmem_limit_bytes=VMEM_LIMIT),
        name="combine",
    )(lpos_tiles, gate_tiles, tab_tiles, tab_tiles, h1p, h1s, ys, g_final)


def _block_diag_gate_weights(wa, wx):
    heads_per_half = N_LRU_HEADS // 2
    eye = jnp.eye(heads_per_half, dtype=wa.dtype)

    def bd(w):
        return jnp.einsum("hij,hg->higj", w, eye).reshape(heads_per_half * LRU_HEAD_DIM,
                                                          heads_per_half * LRU_HEAD_DIM)

    halves = [jnp.concatenate([bd(wa[s * heads_per_half:(s + 1) * heads_per_half]),
                               bd(wx[s * heads_per_half:(s + 1) * heads_per_half])], axis=1)
              for s in range(2)]
    return jnp.stack(halves).astype(bf16)


def kernel(x_prompt, x_sample, state_conv_short, state_conv_lru, state_lru_h, meta_tokens,
           norm_mix_g, w_in, conv_short_w, conv_lru_w, conv_lru_b, w_rg_a, b_rg_a, w_rg_x, b_rg_x,
           rg_lambda, w_out, norm_ffn_g, w_router, b_router, w_gate_up, b_gate_up, w_down, b_down,
           final_norm_g):
    assert norm_mix_g.shape[0] == 1, "single layer"
    bp, seq, _ = x_prompt.shape
    bs, seq_s, _ = x_sample.shape
    n_meta = meta_tokens.shape[0]
    row = lambda a: a.reshape(1, -1).astype(f32)

    mw = dict(g=row(norm_mix_g[0]), w_in=w_in[0].astype(bf16), csw=conv_short_w[0], clw=conv_lru_w[0],
              clb=row(conv_lru_b[0]), wg=_block_diag_gate_weights(w_rg_a[0], w_rg_x[0]),
              ba=row(b_rg_a[0]), bx=row(b_rg_x[0]), lam=row(rg_lambda[0]), w_out=w_out[0].astype(bf16))

    zeros = lambda *s: jnp.zeros(s, f32)
    _, cs_m, cl_m, h_m = _mixer(meta_tokens[None].astype(f32), zeros(1, 2, W_SHORT), zeros(1, 3, W_LRU),
                                zeros(1, 1, W_LRU), mw, n_meta, 1)
    rep = lambda a: jnp.broadcast_to(a, (bp,) + a.shape[1:])
    h1p, cs_p, cl_p, h_p = _mixer(x_prompt, rep(cs_m), rep(cl_m), rep(h_m), mw, MIX_ROWS, MIX_STREAMS)
    h1s, cs_s, cl_s, h_s = _mixer(x_sample, state_conv_short[0], state_conv_lru[0],
                                  state_lru_h[0][:, None, :], mw, seq_s, MIX_STREAMS)
    h1p = h1p.reshape(bp * seq, D_MODEL)
    h1s = h1s.reshape(bs * seq_s, D_MODEL)
    n_tok = h1p.shape[0] + h1s.shape[0]

    g_ffn = row(norm_ffn_g[0])
    lpos, gates, tab, counts = _router(
        h1p, h1s, g_ffn, w_router[0].T.astype(f32), b_router[0].reshape(-1, 1).astype(f32))
    n_tiles = tab.shape[0]

    counts = counts[:, 0]
    padded = (counts + EXP_ROWS - 1) // EXP_ROWS * EXP_ROWS
    pad_end = jnp.cumsum(padded)
    pad_start = pad_end - padded
    n_blocks = -(-(n_tok * TOP_K + N_EXPERTS * (EXP_ROWS - 1)) // EXP_ROWS)
    block_first_row = jnp.arange(n_blocks, dtype=jnp.int32) * EXP_ROWS
    block_e = jnp.minimum(jnp.sum(pad_end[None, :] <= block_first_row[:, None], axis=1),
                          N_EXPERTS - 1).astype(jnp.int32)
    n_used = (pad_end[-1:] // EXP_ROWS).astype(jnp.int32)
    n_rows = n_blocks * EXP_ROWS
    fill_start = jnp.concatenate([pad_start + counts, pad_end[-1:]]).astype(jnp.int32)
    fill_len = jnp.concatenate([padded - counts, n_rows - pad_end[-1:]]).astype(jnp.int32)

    tab_tiles = jnp.concatenate([pad_start[None, :] + tab[:, :, 0], tab[:, :, 1], tab[:, :, 2],
                                 jnp.zeros((n_tiles, LANES - 3 * N_EXPERTS), jnp.int32)],
                                axis=1).reshape(n_tiles, 1, LANES)
    by_tile = lambda a: a.reshape(TOP_K, n_tiles, TOK_TILE).transpose(1, 0, 2).reshape(
        n_tiles, 1, TOP_K * TOK_TILE)

    xs = _dispatch(fill_start, fill_len, by_tile(lpos), tab_tiles, h1p, h1s, g_ffn, n_rows)
    ys = _experts(block_e, n_used, xs,
                  w_gate_up[0].astype(f32), b_gate_up[0][:, None, :].astype(f32),
                  w_down[0].astype(f32), b_down[0][:, None, :].astype(f32))
    yp, ysm = _combine(by_tile(lpos), by_tile(gates), tab_tiles, h1p, h1s, ys, row(final_norm_g))

    st = lambda a: a[None]
    return (yp.reshape(bp, seq, D_MODEL), ysm.reshape(bs, seq_s, D_MODEL),
            st(cs_p), st(cl_p), st(h_p[:, 0, :]), st(cs_s), st(cl_s), st(h_s[:, 0, :]))
```

```python
import functools

import jax
import jax.numpy as jnp
from jax import lax
from jax.experimental import pallas as pl
from jax.experimental.pallas import tpu as pltpu

D_MODEL = 1024
W_SHORT = 512
W_LRU = 512
N_LRU_HEADS = 8
LRU_HEAD_DIM = W_LRU // N_LRU_HEADS
D_IN_PROJ = 3 * W_SHORT + 2 * W_LRU
N_EXPERTS = 32
TOP_K = 4
D_FF = 1024
RG_C = 8.0
SWIGLU_LIMIT = 7.0
SWIGLU_ALPHA = 1.702
EPS = 1e-6

SUBLANES = 8
MIX_ROWS = 256
MIX_STREAMS = 4
TOK_TILE = 512
LOOP_UNROLL = 4
EXP_ROWS = 768
VMEM_LIMIT = 56 * 1024 * 1024
EXPERT_VMEM_LIMIT = 60 * 1024 * 1024

f32 = jnp.float32
bf16 = jnp.bfloat16


def _rmsnorm(x, g):
    return (x * lax.rsqrt(jnp.mean(x * x, axis=-1, keepdims=True) + EPS)) * g


LANES = 128
ROW_SUB = D_MODEL // LANES


def _row_tile(ref, r):
    return ref.at[pl.ds(pl.multiple_of(r * ROW_SUB, ROW_SUB), ROW_SUB)]


def _row_tile_at(ref, first_sub_row):
    return ref.at[pl.ds(pl.multiple_of(first_sub_row, ROW_SUB), ROW_SUB)]


def _store_rows_as_tiles(ref, x, first=0):
    for j in range(ROW_SUB):
        ref[pl.ds(first * ROW_SUB + j, x.shape[0], stride=ROW_SUB), :] = x[:, j * LANES:(j + 1) * LANES]


def _load_tiles_as_rows(ref, first, n):
    return jnp.concatenate([ref[pl.ds(first * ROW_SUB + j, n, stride=ROW_SUB), :] for j in range(ROW_SUB)],
                           axis=1)


def _mixer_kernel(x_ref, cs0_ref, cl0_ref, h0_ref, g_ref, win_ref, csw_ref, clw_ref, clb_ref,
                  wg_ref, ba_ref, bx_ref, lam_ref, wout_ref,
                  h1_ref, ncs_ref, ncl_ref, nh_ref,
                  cvtails, lxtails, hcars, *, tl):
    j = pl.program_id(1)
    n_streams = x_ref.shape[0]
    n_cs, n_cl = cs0_ref.shape[1], cl0_ref.shape[1]

    @pl.when(j == 0)
    def _():
        for r in range(n_streams):
            cvtails[r] = jnp.zeros((SUBLANES, W_SHORT), f32)
            cvtails[r, SUBLANES - n_cs:, :] = cs0_ref[r]
            lxtails[r] = jnp.zeros((SUBLANES, W_LRU), f32)
            lxtails[r, SUBLANES - n_cl:, :] = cl0_ref[r]
            hcars[r] = h0_ref[r]

    ends = _staggered([_mixer_chunk(x_ref.at[r], g_ref, win_ref, csw_ref, clw_ref, clb_ref, wg_ref, ba_ref,
                                    bx_ref, lam_ref, wout_ref, h1_ref.at[r], cvtails.at[r], lxtails.at[r],
                                    hcars.at[r], tl=tl) for r in range(n_streams)])

    @pl.when(j == pl.num_programs(1) - 1)
    def _():
        for r, (cv_end, lx_end, h_end) in enumerate(ends):
            ncs_ref[r] = cv_end[SUBLANES - n_cs:, :]
            ncl_ref[r] = lx_end[SUBLANES - n_cl:, :]
            nh_ref[r] = h_end


def _staggered(chains):
    results = [None] * len(chains)
    step = 0
    while any(r is None for r in results):
        for r, chain in enumerate(chains):
            if step >= r and results[r] is None:
                try:
                    next(chain)
                except StopIteration as done:
                    results[r] = done.value
        step += 1
    return results


def _delayed(x3, tail, s, sub):
    rolled = pltpu.roll(x3, s, 1)
    before = jnp.concatenate([pltpu.roll(tail, s, 0)[None], rolled[:-1]], axis=0)
    return jnp.where(sub < s, before, rolled)


def _mixer_chunk(x_ref, g_ref, win_ref, csw_ref, clw_ref, clb_ref, wg_ref, ba_ref, bx_ref, lam_ref, wout_ref,
                 h1_ref, cvtail, lxtail, hcar, *, tl):
    n_groups = tl // SUBLANES
    grouped = lambda a: a.reshape(n_groups, SUBLANES, a.shape[-1])
    sub = lax.broadcasted_iota(jnp.int32, (n_groups, SUBLANES, W_LRU), 1)
    x = x_ref[...]
    u = _rmsnorm(x, g_ref[...])
    proj = jnp.dot(u.astype(bf16), win_ref[...], preferred_element_type=f32)
    b_g = proj[:, 0:W_SHORT]
    c_g = proj[:, W_SHORT:2 * W_SHORT]
    v = proj[:, 2 * W_SHORT:3 * W_SHORT]
    lx = proj[:, 3 * W_SHORT:3 * W_SHORT + W_LRU]
    ly = proj[:, 3 * W_SHORT + W_LRU:]
    yield

    cv = grouped(c_g * v)
    cv_tail = cvtail[...]
    conv_a = (_delayed(cv, cv_tail, 2, sub) * csw_ref[0:1, :]
              + _delayed(cv, cv_tail, 1, sub) * csw_ref[1:2, :]
              + cv * csw_ref[2:3, :])
    out_a = b_g * conv_a.reshape(tl, W_SHORT)
    cv_end = cv[n_groups - 1]
    cvtail[...] = cv_end

    lx = grouped(lx)
    lx_tail = lxtail[...]
    xc = ((_delayed(lx, lx_tail, 3, sub) * clw_ref[0:1, :]
           + _delayed(lx, lx_tail, 2, sub) * clw_ref[1:2, :]
           + _delayed(lx, lx_tail, 1, sub) * clw_ref[2:3, :]
           + lx * clw_ref[3:4, :]) + clb_ref[...]).reshape(tl, W_LRU)
    lx_end = lx[n_groups - 1]
    lxtail[...] = lx_end
    xcb = xc.astype(bf16)
    half = W_LRU // 2
    g0 = jnp.dot(xcb[:, :half], wg_ref[0], preferred_element_type=f32)
    g1 = jnp.dot(xcb[:, half:], wg_ref[1], preferred_element_type=f32)
    ga = jnp.concatenate([g0[:, :half], g1[:, :half]], axis=1) + ba_ref[...]
    gx = jnp.concatenate([g0[:, half:], g1[:, half:]], axis=1) + bx_ref[...]
    r = jax.nn.sigmoid(ga)
    ig = jax.nn.sigmoid(gx)
    z = -lam_ref[...]
    softplus = jnp.maximum(z, 0.0) + jnp.log1p(jnp.exp(-jnp.abs(z)))
    log_a = (-RG_C * r) * softplus
    a = jnp.exp(log_a)
    th = jnp.tanh(log_a)
    uu = jnp.sqrt((-2.0 * th) / (1.0 - th)) * (ig * xc)
    yield

    aa = grouped(a)
    hh = grouped(uu)
    s = 1
    while s < SUBLANES:
        h_sh = jnp.where(sub >= s, pltpu.roll(hh, s, 1), 0.0)
        a_sh = jnp.where(sub >= s, pltpu.roll(aa, s, 1), 1.0)
        hh = aa * h_sh + hh
        aa = aa * a_sh
        s *= 2
    a_last = jnp.broadcast_to(aa[:, SUBLANES - 1:, :], aa.shape)
    h_last = jnp.broadcast_to(hh[:, SUBLANES - 1:, :], hh.shape)
    carry = jnp.broadcast_to(hcar[...], (SUBLANES, W_LRU))
    groups = []
    for gi in range(n_groups):
        groups.append(aa[gi] * carry + hh[gi])
        carry = a_last[gi] * carry + h_last[gi]
    hh = jnp.concatenate(groups, axis=0)
    hcar[...] = carry[0:1, :]
    yield

    gelu = ly *(0.5 * (1.0 + jnp.tanh(0.7978845608028654 * (ly + 0.044715 * (ly * ly * ly)))))
    out_b = hh * gelu
    y = jnp.concatenate([out_a, out_b], axis=1).astype(bf16)
    h1_ref[...] = x + jnp.dot(y, wout_ref[...], preferred_element_type=f32)
    return cv_end, lx_end, hh[tl - 1:tl, :]


def _mixer(x, cs0, cl0, h0, mw, tl, streams):
    nb, seq, _ = x.shape
    assert seq % tl == 0 and tl % SUBLANES == 0 and nb % streams == 0
    full = lambda a: pl.BlockSpec(a.shape, lambda b, j: (0,) * a.ndim)
    per_b = lambda a: pl.BlockSpec((streams,) + a.shape[1:], lambda b, j: (b,) + (0,) * (a.ndim - 1))
    weights = (mw["g"], mw["w_in"], mw["csw"], mw["clw"], mw["clb"], mw["wg"], mw["ba"], mw["bx"],
               mw["lam"], mw["w_out"])
    return pl.pallas_call(
        functools.partial(_mixer_kernel, tl=tl),
        grid=(nb // streams, seq // tl),
        in_specs=[pl.BlockSpec((streams, tl, D_MODEL), lambda b, j: (b, j, 0)),
                  per_b(cs0), per_b(cl0), per_b(h0)] + [full(w) for w in weights],
        out_specs=[pl.BlockSpec((streams, tl, D_MODEL), lambda b, j: (b, j, 0)),
                   per_b(cs0), per_b(cl0), per_b(h0)],
        out_shape=[jax.ShapeDtypeStruct(x.shape, f32),
                   jax.ShapeDtypeStruct(cs0.shape, f32),
                   jax.ShapeDtypeStruct(cl0.shape, f32),
                   jax.ShapeDtypeStruct(h0.shape, f32)],
        scratch_shapes=[pltpu.VMEM((streams, SUBLANES, W_SHORT), f32),
                        pltpu.VMEM((streams, SUBLANES, W_LRU), f32),
                        pltpu.VMEM((streams, 1, W_LRU), f32)],
        compiler_params=pltpu.CompilerParams(
            dimension_semantics=("arbitrary", "arbitrary"), vmem_limit_bytes=VMEM_LIMIT),
        name=f"mixer_{seq}",
    )(x, cs0, cl0, h0, *weights)


def _route_tile(h, n_valid, g, w_router_t, b_router_col):
    tr = h.shape[0]
    xn = _rmsnorm(h, g)
    logits = lax.dot_general(w_router_t.astype(bf16), xn.astype(bf16), (((1,), (1,)), ((), ())),
                             preferred_element_type=f32) + b_router_col

    rows = lax.broadcasted_iota(jnp.int32, (N_EXPERTS, tr), 0)
    vals, idxs = [], []
    cur = logits
    for _ in range(TOP_K):
        m = jnp.max(cur, axis=0, keepdims=True)
        ik = jnp.min(jnp.where(cur == m, rows, N_EXPERTS), axis=0, keepdims=True)
        vals.append(m)
        idxs.append(ik)
        cur = jnp.where(rows == ik, -jnp.inf, cur)
    ex = [jnp.exp(v - vals[0]) for v in vals]
    denom = ex[0] + ex[1] + ex[2] + ex[3]

    routed = lax.broadcasted_iota(jnp.int32, (N_EXPERTS, tr), 1) < n_valid
    onehot = jnp.zeros((N_EXPERTS, tr), f32)
    for ik in idxs:
        onehot = onehot + jnp.where(jnp.logical_and(rows == ik, routed), 1.0, 0.0)
    src = lax.broadcasted_iota(jnp.int32, (tr, tr), 0)
    dst = lax.broadcasted_iota(jnp.int32, (tr, tr), 1)
    tri = jnp.where(src < dst, 1.0, 0.0).astype(bf16)
    before = jnp.dot(onehot.astype(bf16), tri, preferred_element_type=f32)
    tile_cnt = jnp.broadcast_to(jnp.sum(onehot, axis=1, keepdims=True), (N_EXPERTS, LANES))
    e_row = lax.broadcasted_iota(jnp.int32, (N_EXPERTS, LANES), 0)
    upto = tile_cnt
    s = 1
    while s < N_EXPERTS:
        upto = upto + jnp.where(e_row >= s, pltpu.roll(upto, s, 0), 0.0)
        s *= 2
    first = upto - tile_cnt
    within = before + first[:, 0:1]
    lpos = [ROW_SUB * jnp.sum(jnp.where(rows == ik, within, 0.0), axis=0, keepdims=True).astype(jnp.int32)
            for ik in idxs]
    return lpos, [e / denom for e in ex], tile_cnt, first


def _token_tile(i, n_prompt_tiles, hp_ref, hs_ref):
    n_sample = hs_ref.shape[0]
    sample = jnp.concatenate([hs_ref[...]] * (TOK_TILE // n_sample), axis=0)
    on_prompt = i < n_prompt_tiles
    return jnp.where(on_prompt, hp_ref[...], sample), jnp.where(on_prompt, TOK_TILE, n_sample)


def _router_kernel(hp_ref, hs_ref, g_ref, wrt_ref, br_ref,
                   lpos_ref, gate_ref, tab_ref, cnt_ref, carry, *, n_prompt_tiles):
    i = pl.program_id(0)

    @pl.when(i == 0)
    def _():
        carry[...] = jnp.zeros(carry.shape, f32)

    h, n_valid = _token_tile(i, n_prompt_tiles, hp_ref, hs_ref)
    lpos, gates, tile_cnt, first = _route_tile(h, n_valid, g_ref[...], wrt_ref[...], br_ref[...])
    for k in range(TOP_K):
        lpos_ref[k:k + 1, :] = lpos[k]
        gate_ref[k:k + 1, :] = gates[k]
    lane = lax.broadcasted_iota(jnp.int32, carry.shape, 1)
    tab_ref[0] = jnp.where(lane == 0, carry[...], jnp.where(lane == 1, tile_cnt, first)).astype(jnp.int32)
    carry[...] = carry[...] + tile_cnt
    cnt_ref[...] = carry[...].astype(jnp.int32)


def _router(h1p, h1s, g, w_router_t, b_router_col):
    assert h1p.shape[0] % TOK_TILE == 0 and TOK_TILE % h1s.shape[0] == 0 and h1s.shape[0] % 4 == 0
    n_prompt_tiles = h1p.shape[0] // TOK_TILE
    n_steps = n_prompt_tiles + 1
    const = lambda a: pl.BlockSpec(a.shape, lambda s: (0,) * a.ndim)
    lane_blk = pl.BlockSpec((TOP_K, TOK_TILE), lambda s: (0, s))
    return pl.pallas_call(
        functools.partial(_router_kernel, n_prompt_tiles=n_prompt_tiles),
        grid=(n_steps,),
        in_specs=[pl.BlockSpec((TOK_TILE, D_MODEL), lambda s: (jnp.minimum(s, n_prompt_tiles - 1), 0)),
                  const(h1s), const(g), const(w_router_t), const(b_router_col)],
        out_specs=[lane_blk, lane_blk,
                   pl.BlockSpec((1, N_EXPERTS, LANES), lambda s: (s, 0, 0)),
                   pl.BlockSpec((N_EXPERTS, LANES), lambda s: (0, 0))],
        out_shape=[jax.ShapeDtypeStruct((TOP_K, n_steps * TOK_TILE), jnp.int32),
                   jax.ShapeDtypeStruct((TOP_K, n_steps * TOK_TILE), f32),
                   jax.ShapeDtypeStruct((n_steps, N_EXPERTS, LANES), jnp.int32),
                   jax.ShapeDtypeStruct((N_EXPERTS, LANES), jnp.int32)],
        scratch_shapes=[pltpu.VMEM((N_EXPERTS, LANES), f32)],
        compiler_params=pltpu.CompilerParams(
            dimension_semantics=("arbitrary",), vmem_limit_bytes=VMEM_LIMIT),
        name="router",
    )(h1p, h1s, g, w_router_t, b_router_col)


def _wait_rows(n, src_ref, dst_ref, sem):
    pltpu.make_async_copy(src_ref.at[pl.ds(0, n * ROW_SUB)], dst_ref.at[pl.ds(0, n * ROW_SUB)], sem).wait()


def _rows(ref, first, n):
    return ref.at[pl.ds(pl.multiple_of(first * ROW_SUB, ROW_SUB), n * ROW_SUB)]


FILL_CHUNKS = tuple(c for c in (1 << b for b in reversed(range(EXP_ROWS.bit_length()))) if c < EXP_ROWS)
TAIL_CHUNK = min(EXP_ROWS & -EXP_ROWS, FILL_CHUNKS[0])
RUN_CHUNKS = tuple(1 << b for b in reversed(range(TOK_TILE.bit_length())))
RUN_LARGE = 64


def _start_runs(tab_ref, src_ref, dst_ref, sem, *, src_is_sorted_tile):
    for e in range(N_EXPERTS):
        hbm_first = tab_ref[0, 0, e]
        length = tab_ref[0, 0, N_EXPERTS + e]
        tile_first = tab_ref[0, 0, 2 * N_EXPERTS + e]
        src_first, dst_first = (tile_first, hbm_first) if src_is_sorted_tile else (hbm_first, tile_first)

        def start_pieces(chunks):
            for chunk in chunks:
                offset = length & ~(2 * chunk - 1)
                copy = pltpu.make_async_copy(_rows(src_ref, src_first + offset, chunk),
                                             _rows(dst_ref, dst_first + offset, chunk), sem)
                pl.when((length & chunk) != 0)(copy.start)

        large = tuple(c for c in RUN_CHUNKS if c >= RUN_LARGE)
        pl.when(length >= RUN_LARGE)(functools.partial(start_pieces, large))
        start_pieces(tuple(c for c in RUN_CHUNKS if c < RUN_LARGE))


def _zero_fill_copies(fs_ref, fl_ref, zbuf, xs_hbm, sem):
    out = []
    for e in range(N_EXPERTS):
        start, length = fs_ref[e], fl_ref[e]
        for chunk in FILL_CHUNKS:
            offset = length & ~(2 * chunk - 1)
            first = pl.multiple_of((start + offset) * ROW_SUB, ROW_SUB)
            copy = pltpu.make_async_copy(zbuf.at[pl.ds(0, chunk * ROW_SUB)],
                                         xs_hbm.at[pl.ds(first, chunk * ROW_SUB)], sem)
            out.append(((length & chunk) != 0, copy))
    start, length = fs_ref[N_EXPERTS], fl_ref[N_EXPERTS]
    chunk = TAIL_CHUNK
    for c in range(N_EXPERTS * EXP_ROWS // chunk):
        first = pl.multiple_of((start + c * chunk) * ROW_SUB, ROW_SUB)
        copy = pltpu.make_async_copy(zbuf.at[pl.ds(0, chunk * ROW_SUB)],
                                     xs_hbm.at[pl.ds(first, chunk * ROW_SUB)], sem)
        out.append((c * chunk < length, copy))
    return out


def _dispatch_kernel(fs_ref, fl_ref, lpos_ref, tab_ref, hp_ref, hs_ref, g_ref, xs_hbm,
                     xn_tiles, sorted_buf, zbuf, sem, zsem, *, n_prompt_tiles):
    i = pl.program_id(0)
    last = pl.num_programs(0) - 1
    tr = hp_ref.shape[0]
    n_rows = TOP_K * tr
    slot = i % 2

    @pl.when(i == 0)
    def _():
        zbuf[...] = jnp.zeros(zbuf.shape, f32)
        for present, copy in _zero_fill_copies(fs_ref, fl_ref, zbuf, xs_hbm, zsem):
            pl.when(present)(copy.start)

    h, n_valid = _token_tile(i, n_prompt_tiles, hp_ref, hs_ref)
    _store_rows_as_tiles(xn_tiles, _rmsnorm(h, g_ref[...]))

    @pl.when(i >= 2)
    def _():
        _wait_rows(n_rows, sorted_buf.at[slot], xs_hbm, sem.at[slot])

    def place(group, carry):
        for u in range(LOOP_UNROLL):
            t = group * LOOP_UNROLL + u
            row = _row_tile(xn_tiles, t)[...]
            for k in range(TOP_K):
                _row_tile_at(sorted_buf.at[slot], lpos_ref[0, 0, k * tr + t])[...] = row
        return carry
    lax.fori_loop(0, n_valid // LOOP_UNROLL, place, 0)

    _start_runs(tab_ref, sorted_buf.at[slot], xs_hbm, sem.at[slot], src_is_sorted_tile=True)

    @pl.when(i == last)
    def _():
        _wait_rows(n_rows, sorted_buf.at[1 - slot], xs_hbm, sem.at[1 - slot])
        _wait_rows(TOP_K * hs_ref.shape[0], sorted_buf.at[slot], xs_hbm, sem.at[slot])
        for present, copy in _zero_fill_copies(fs_ref, fl_ref, zbuf, xs_hbm, zsem):
            pl.when(present)(copy.wait)


def _dispatch(fill_start, fill_len, lpos_tiles, tab_tiles, h1p, h1s, g, n_rows):
    n_prompt_tiles = h1p.shape[0] // TOK_TILE
    n_tiles = lpos_tiles.shape[0]
    assert n_tiles >= 2
    smem_blk = lambda a: pl.BlockSpec((1, 1, a.shape[2]), lambda i, fs, fl: (i, 0, 0), memory_space=pltpu.SMEM)
    const = lambda a: pl.BlockSpec(a.shape, lambda i, fs, fl: (0,) * a.ndim)
    grid_spec = pltpu.PrefetchScalarGridSpec(
        num_scalar_prefetch=2,
        grid=(n_tiles,),
        in_specs=[smem_blk(lpos_tiles), smem_blk(tab_tiles),
                  pl.BlockSpec((TOK_TILE, D_MODEL), lambda i, fs, fl: (jnp.minimum(i, n_prompt_tiles - 1), 0)),
                  const(h1s), const(g)],
        out_specs=pl.BlockSpec(memory_space=pl.ANY),
        scratch_shapes=[pltpu.VMEM((TOK_TILE * ROW_SUB, LANES), f32),
                        pltpu.VMEM((2, TOP_K * TOK_TILE * ROW_SUB, LANES), f32),
                        pltpu.VMEM((FILL_CHUNKS[0] * ROW_SUB, LANES), f32),
                        pltpu.SemaphoreType.DMA((2,)), pltpu.SemaphoreType.DMA(())],
    )
    return pl.pallas_call(
        functools.partial(_dispatch_kernel, n_prompt_tiles=n_prompt_tiles),
        grid_spec=grid_spec,
        out_shape=jax.ShapeDtypeStruct((n_rows * ROW_SUB, LANES), f32),
        compiler_params=pltpu.CompilerParams(
            dimension_semantics=("arbitrary",), vmem_limit_bytes=VMEM_LIMIT),
        name="dispatch",
    )(fill_start, fill_len, lpos_tiles, tab_tiles, h1p, h1s, g)


def _expert_kernel(be_ref, nu_ref, xs_ref, wgu_ref, bgu_ref, wd_ref, bd_ref, ys_ref, wgu_bf, wd_bf):
    i = pl.program_id(0)
    n_used = nu_ref[0]
    new_expert = jnp.logical_or(i == 0, be_ref[i] != be_ref[jnp.maximum(i - 1, 0)])

    @pl.when(jnp.logical_and(i < n_used, new_expert))
    def _():
        wgu_bf[...] = wgu_ref[0].astype(bf16)
        wd_bf[...] = wd_ref[0].astype(bf16)

    @pl.when(i < n_used)
    def _():
        x = _load_tiles_as_rows(xs_ref, 0, EXP_ROWS).astype(bf16)
        gu = jnp.dot(x, wgu_bf[...], preferred_element_type=f32) + bgu_ref[0]
        gate = jnp.minimum(gu[:, :D_FF], SWIGLU_LIMIT)
        up = jnp.clip(gu[:, D_FF:], -SWIGLU_LIMIT, SWIGLU_LIMIT)
        glu = gate * jax.nn.sigmoid(gate * SWIGLU_ALPHA)
        act = ((up + 1.0) * glu).astype(bf16)
        _store_rows_as_tiles(ys_ref, jnp.dot(act, wd_bf[...], preferred_element_type=f32) + bd_ref[0])

    @pl.when(i >= n_used)
    def _():
        ys_ref[...] = jnp.zeros(ys_ref.shape, f32)


def _experts(block_e, n_used, xs, w_gu, b_gu, w_down, b_down):
    n_blocks = xs.shape[0] // (EXP_ROWS * ROW_SUB)
    by_expert = lambda a: pl.BlockSpec((1,) + a.shape[1:], lambda i, be, nu: (be[i],) + (0,) * (a.ndim - 1))
    grid_spec = pltpu.PrefetchScalarGridSpec(
        num_scalar_prefetch=2,
        grid=(n_blocks,),
        in_specs=[pl.BlockSpec((EXP_ROWS * ROW_SUB, LANES),
                               lambda i, be, nu: (jnp.minimum(i, jnp.maximum(nu[0] - 1, 0)), 0)),
                  by_expert(w_gu), by_expert(b_gu), by_expert(w_down), by_expert(b_down)],
        out_specs=pl.BlockSpec((EXP_ROWS * ROW_SUB, LANES), lambda i, be, nu: (i, 0)),
        scratch_shapes=[pltpu.VMEM(w_gu.shape[1:], bf16), pltpu.VMEM(w_down.shape[1:], bf16)],
    )
    return pl.pallas_call(
        _expert_kernel,
        grid_spec=grid_spec,
        out_shape=jax.ShapeDtypeStruct(xs.shape, f32),
        compiler_params=pltpu.CompilerParams(
            dimension_semantics=("arbitrary",), vmem_limit_bytes=EXPERT_VMEM_LIMIT),
        name="experts",
    )(block_e, n_used, xs, w_gu, b_gu, w_down, b_down)


def _combine_kernel(lpos_ref, gate_ref, tab_cur, tab_nxt, hp_ref, hs_ref, ys_hbm, gf_ref,
                    yp_ref, ysm_ref, ybuf, moe_tiles, sem, *, n_prompt_tiles):
    i = pl.program_id(0)
    tr = hp_ref.shape[0]
    n_rows = TOP_K * tr
    slot = i % 2

    @pl.when(i == 0)
    def _():
        _start_runs(tab_cur, ys_hbm, ybuf.at[0], sem.at[0], src_is_sorted_tile=False)

    @pl.when(i + 1 < pl.num_programs(0))
    def _():
        _start_runs(tab_nxt, ys_hbm, ybuf.at[1 - slot], sem.at[1 - slot], src_is_sorted_tile=False)

    n_sample = hs_ref.shape[0]
    h, n_valid = _token_tile(i, n_prompt_tiles, hp_ref, hs_ref)

    @pl.when(i < n_prompt_tiles)
    def _():
        _wait_rows(n_rows, ys_hbm, ybuf.at[slot], sem.at[slot])

    @pl.when(i >= n_prompt_tiles)
    def _():
        _wait_rows(TOP_K * n_sample, ys_hbm, ybuf.at[slot], sem.at[slot])

    def mix(group, carry):
        for u in range(LOOP_UNROLL):
            t = group * LOOP_UNROLL + u
            acc = gate_ref[0, 0, t] * _row_tile_at(ybuf.at[slot], lpos_ref[0, 0, t])[...]
            for k in range(1, TOP_K):
                acc = acc + (gate_ref[0, 0, k * tr + t]
                             * _row_tile_at(ybuf.at[slot], lpos_ref[0, 0, k * tr + t])[...])
            _row_tile(moe_tiles, t)[...] = acc
        return carry
    lax.fori_loop(0, n_valid // LOOP_UNROLL, mix, 0)

    out = _rmsnorm(h + _load_tiles_as_rows(moe_tiles, 0, tr), gf_ref[...])

    @pl.when(i < n_prompt_tiles)
    def _():
        yp_ref[...] = out

    @pl.when(i >= n_prompt_tiles)
    def _():
        ysm_ref[...] = out[0:n_sample, :]


def _combine(lpos_tiles, gate_tiles, tab_tiles, h1p, h1s, ys, g_final):
    n_prompt_tiles = h1p.shape[0] // TOK_TILE
    n_tiles = lpos_tiles.shape[0]
    const = lambda a: pl.BlockSpec(a.shape, lambda i: (0,) * a.ndim)
    prompt_blk = pl.BlockSpec((TOK_TILE, D_MODEL), lambda i: (jnp.minimum(i, n_prompt_tiles - 1), 0))
    smem_blk = lambda a, ahead: pl.BlockSpec(
        (1, 1, a.shape[2]), lambda i: (jnp.minimum(i + ahead, n_tiles - 1), 0, 0), memory_space=pltpu.SMEM)
    return pl.pallas_call(
        functools.partial(_combine_kernel, n_prompt_tiles=n_prompt_tiles),
        grid=(n_tiles,),
        in_specs=[smem_blk(lpos_tiles, 0), smem_blk(gate_tiles, 0), smem_blk(tab_tiles, 0), smem_blk(tab_tiles, 1),
                  prompt_blk, const(h1s),
                  pl.BlockSpec(memory_space=pl.ANY),
                  const(g_final)],
        out_specs=[prompt_blk, const(h1s)],
        out_shape=[jax.ShapeDtypeStruct(h1p.shape, f32), jax.ShapeDtypeStruct(h1s.shape, f32)],
        scratch_shapes=[pltpu.VMEM((2, TOP_K * TOK_TILE * ROW_SUB, LANES), f32),
                        pltpu.VMEM((TOK_TILE * ROW_SUB, LANES), f32),
                        pltpu.SemaphoreType.DMA((2,))],
        compiler_params=pltpu.CompilerParams(
            dimension_semantics=("arbitrary",), vmem_limit_bytes=VMEM_LIMIT),
        name="combine",
    )(lpos_tiles, gate_tiles, tab_tiles, tab_tiles, h1p, h1s, ys, g_final)


def _block_diag_gate_weights(wa, wx):
    heads_per_half = N_LRU_HEADS // 2
    eye = jnp.eye(heads_per_half, dtype=wa.dtype)

    def bd(w):
        return jnp.einsum("hij,hg->higj", w, eye).reshape(heads_per_half * LRU_HEAD_DIM,
                                                          heads_per_half * LRU_HEAD_DIM)

    halves = [jnp.concatenate([bd(wa[s * heads_per_half:(s + 1) * heads_per_half]),
                               bd(wx[s * heads_per_half:(s + 1) * heads_per_half])], axis=1)
              for s in range(2)]
    return jnp.stack(halves).astype(bf16)


def kernel(x_prompt, x_sample, state_conv_short, state_conv_lru, state_lru_h, meta_tokens,
           norm_mix_g, w_in, conv_short_w, conv_lru_w, conv_lru_b, w_rg_a, b_rg_a, w_rg_x, b_rg_x,
           rg_lambda, w_out, norm_ffn_g, w_router, b_router, w_gate_up, b_gate_up, w_down, b_down,
           final_norm_g):
    assert norm_mix_g.shape[0] == 1, "single layer"
    bp, seq, _ = x_prompt.shape
    bs, seq_s, _ = x_sample.shape
    n_meta = meta_tokens.shape[0]
    row = lambda a: a.reshape(1, -1).astype(f32)

    mw = dict(g=row(norm_mix_g[0]), w_in=w_in[0].astype(bf16), csw=conv_short_w[0], clw=conv_lru_w[0],
              clb=row(conv_lru_b[0]), wg=_block_diag_gate_weights(w_rg_a[0], w_rg_x[0]),
              ba=row(b_rg_a[0]), bx=row(b_rg_x[0]), lam=row(rg_lambda[0]), w_out=w_out[0].astype(bf16))

    zeros = lambda *s: jnp.zeros(s, f32)
    _, cs_m, cl_m, h_m = _mixer(meta_tokens[None].astype(f32), zeros(1, 2, W_SHORT), zeros(1, 3, W_LRU),
                                zeros(1, 1, W_LRU), mw, n_meta, 1)
    rep = lambda a: jnp.broadcast_to(a, (bp,) + a.shape[1:])
    h1p, cs_p, cl_p, h_p = _mixer(x_prompt, rep(cs_m), rep(cl_m), rep(h_m), mw, MIX_ROWS, MIX_STREAMS)
    h1s, cs_s, cl_s, h_s = _mixer(x_sample, state_conv_short[0], state_conv_lru[0],
                                  state_lru_h[0][:, None, :], mw, seq_s, MIX_STREAMS)
    h1p = h1p.reshape(bp * seq, D_MODEL)
    h1s = h1s.reshape(bs * seq_s, D_MODEL)
    n_tok = h1p.shape[0] + h1s.shape[0]

    g_ffn = row(norm_ffn_g[0])
    lpos, gates, tab, counts = _router(
        h1p, h1s, g_ffn, w_router[0].T.astype(f32), b_router[0].reshape(-1, 1).astype(f32))
    n_tiles = tab.shape[0]

    counts = counts[:, 0]
    padded = (counts + EXP_ROWS - 1) // EXP_ROWS * EXP_ROWS
    pad_end = jnp.cumsum(padded)
    pad_start = pad_end - padded
    n_blocks = -(-(n_tok * TOP_K + N_EXPERTS * (EXP_ROWS - 1)) // EXP_ROWS)
    block_first_row = jnp.arange(n_blocks, dtype=jnp.int32) * EXP_ROWS
    block_e = jnp.minimum(jnp.sum(pad_end[None, :] <= block_first_row[:, None], axis=1),
                          N_EXPERTS - 1).astype(jnp.int32)
    n_used = (pad_end[-1:] // EXP_ROWS).astype(jnp.int32)
    n_rows = n_blocks * EXP_ROWS
    fill_start = jnp.concatenate([pad_start + counts, pad_end[-1:]]).astype(jnp.int32)
    fill_len = jnp.concatenate([padded - counts, n_rows - pad_end[-1:]]).astype(jnp.int32)

    tab_tiles = jnp.concatenate([pad_start[None, :] + tab[:, :, 0], tab[:, :, 1], tab[:, :, 2],
                                 jnp.zeros((n_tiles, LANES - 3 * N_EXPERTS), jnp.int32)],
                                axis=1).reshape(n_tiles, 1, LANES)
    by_tile = lambda a: a.reshape(TOP_K, n_tiles, TOK_TILE).transpose(1, 0, 2).reshape(
        n_tiles, 1, TOP_K * TOK_TILE)

    xs = _dispatch(fill_start, fill_len, by_tile(lpos), tab_tiles, h1p, h1s, g_ffn, n_rows)
    ys = _experts(block_e, n_used, xs,
                  w_gate_up[0].astype(f32), b_gate_up[0][:, None, :].astype(f32),
                  w_down[0].astype(f32), b_down[0][:, None, :].astype(f32))
    yp, ysm = _combine(by_tile(lpos), by_tile(gates), tab_tiles, h1p, h1s, ys, row(final_norm_g))

    st = lambda a: a[None]
    return (yp.reshape(bp, seq, D_MODEL), ysm.reshape(bs, seq_s, D_MODEL),
            st(cs_p), st(cl_p), st(h_p[:, 0, :]), st(cs_s), st(cl_s), st(h_s[:, 0, :]))
```

```python
import functools

import jax
import jax.numpy as jnp
from jax import lax
from jax.experimental import pallas as pl
from jax.experimental.pallas import tpu as pltpu

D_MODEL = 1024
W_SHORT = 512
W_LRU = 512
N_LRU_HEADS = 8
LRU_HEAD_DIM = W_LRU // N_LRU_HEADS
D_IN_PROJ = 3 * W_SHORT + 2 * W_LRU
N_EXPERTS = 32
TOP_K = 4
D_FF = 1024
RG_C = 8.0
SWIGLU_LIMIT = 7.0
SWIGLU_ALPHA = 1.702
EPS = 1e-6

SUBLANES = 8
MIX_ROWS = 256
MIX_STREAMS = 4
TOK_TILE = 512
LOOP_UNROLL = 8
EXP_ROWS = 768
VMEM_LIMIT = 56 * 1024 * 1024
EXPERT_VMEM_LIMIT = 60 * 1024 * 1024

f32 = jnp.float32
bf16 = jnp.bfloat16


def _rmsnorm(x, g):
    return (x * lax.rsqrt(jnp.mean(x * x, axis=-1, keepdims=True) + EPS)) * g


LANES = 128
ROW_SUB = D_MODEL // LANES


def _row_tile(ref, r):
    return ref.at[pl.ds(pl.multiple_of(r * ROW_SUB, ROW_SUB), ROW_SUB)]


def _row_tile_at(ref, first_sub_row):
    return ref.at[pl.ds(pl.multiple_of(first_sub_row, ROW_SUB), ROW_SUB)]


def _store_rows_as_tiles(ref, x, first=0):
    for j in range(ROW_SUB):
        ref[pl.ds(first * ROW_SUB + j, x.shape[0], stride=ROW_SUB), :] = x[:, j * LANES:(j + 1) * LANES]


def _load_tiles_as_rows(ref, first, n):
    return jnp.concatenate([ref[pl.ds(first * ROW_SUB + j, n, stride=ROW_SUB), :] for j in range(ROW_SUB)],
                           axis=1)


def _mixer_kernel(x_ref, cs0_ref, cl0_ref, h0_ref, g_ref, win_ref, csw_ref, clw_ref, clb_ref,
                  wg_ref, ba_ref, bx_ref, lam_ref, wout_ref,
                  h1_ref, ncs_ref, ncl_ref, nh_ref,
                  cvtails, lxtails, hcars, *, tl):
    j = pl.program_id(1)
    n_streams = x_ref.shape[0]
    n_cs, n_cl = cs0_ref.shape[1], cl0_ref.shape[1]

    @pl.when(j == 0)
    def _():
        for r in range(n_streams):
            cvtails[r] = jnp.zeros((SUBLANES, W_SHORT), f32)
            cvtails[r, SUBLANES - n_cs:, :] = cs0_ref[r]
            lxtails[r] = jnp.zeros((SUBLANES, W_LRU), f32)
            lxtails[r, SUBLANES - n_cl:, :] = cl0_ref[r]
            hcars[r] = h0_ref[r]

    ends = _staggered([_mixer_chunk(x_ref.at[r], g_ref, win_ref, csw_ref, clw_ref, clb_ref, wg_ref, ba_ref,
                                    bx_ref, lam_ref, wout_ref, h1_ref.at[r], cvtails.at[r], lxtails.at[r],
                                    hcars.at[r], tl=tl) for r in range(n_streams)])

    @pl.when(j == pl.num_programs(1) - 1)
    def _():
        for r, (cv_end, lx_end, h_end) in enumerate(ends):
            ncs_ref[r] = cv_end[SUBLANES - n_cs:, :]
            ncl_ref[r] = lx_end[SUBLANES - n_cl:, :]
            nh_ref[r] = h_end


def _staggered(chains):
    results = [None] * len(chains)
    step = 0
    while any(r is None for r in results):
        for r, chain in enumerate(chains):
            if step >= r and results[r] is None:
                try:
                    next(chain)
                except StopIteration as done:
                    results[r] = done.value
        step += 1
    return results


def _delayed(x3, tail, s, sub):
    rolled = pltpu.roll(x3, s, 1)
    before = jnp.concatenate([pltpu.roll(tail, s, 0)[None], rolled[:-1]], axis=0)
    return jnp.where(sub < s, before, rolled)


def _mixer_chunk(x_ref, g_ref, win_ref, csw_ref, clw_ref, clb_ref, wg_ref, ba_ref, bx_ref, lam_ref, wout_ref,
                 h1_ref, cvtail, lxtail, hcar, *, tl):
    n_groups = tl // SUBLANES
    grouped = lambda a: a.reshape(n_groups, SUBLANES, a.shape[-1])
    sub = lax.broadcasted_iota(jnp.int32, (n_groups, SUBLANES, W_LRU), 1)
    x = x_ref[...]
    u = _rmsnorm(x, g_ref[...])
    proj = jnp.dot(u.astype(bf16), win_ref[...], preferred_element_type=f32)
    b_g = proj[:, 0:W_SHORT]
    c_g = proj[:, W_SHORT:2 * W_SHORT]
    v = proj[:, 2 * W_SHORT:3 * W_SHORT]
    lx = proj[:, 3 * W_SHORT:3 * W_SHORT + W_LRU]
    ly = proj[:, 3 * W_SHORT + W_LRU:]
    yield

    cv = grouped(c_g * v)
    cv_tail = cvtail[...]
    conv_a = (_delayed(cv, cv_tail, 2, sub) * csw_ref[0:1, :]
              + _delayed(cv, cv_tail, 1, sub) * csw_ref[1:2, :]
              + cv * csw_ref[2:3, :])
    out_a = b_g * conv_a.reshape(tl, W_SHORT)
    cv_end = cv[n_groups - 1]
    cvtail[...] = cv_end

    lx = grouped(lx)
    lx_tail = lxtail[...]
    xc = ((_delayed(lx, lx_tail, 3, sub) * clw_ref[0:1, :]
           + _delayed(lx, lx_tail, 2, sub) * clw_ref[1:2, :]
           + _delayed(lx, lx_tail, 1, sub) * clw_ref[2:3, :]
           + lx * clw_ref[3:4, :]) + clb_ref[...]).reshape(tl, W_LRU)
    lx_end = lx[n_groups - 1]
    lxtail[...] = lx_end
    xcb = xc.astype(bf16)
    half = W_LRU // 2
    g0 = jnp.dot(xcb[:, :half], wg_ref[0], preferred_element_type=f32)
    g1 = jnp.dot(xcb[:, half:], wg_ref[1], preferred_element_type=f32)
    ga = jnp.concatenate([g0[:, :half], g1[:, :half]], axis=1) + ba_ref[...]
    gx = jnp.concatenate([g0[:, half:], g1[:, half:]], axis=1) + bx_ref[...]
    r = jax.nn.sigmoid(ga)
    ig = jax.nn.sigmoid(gx)
    z = -lam_ref[...]
    softplus = jnp.maximum(z, 0.0) + jnp.log1p(jnp.exp(-jnp.abs(z)))
    log_a = (-RG_C * r) * softplus
    a = jnp.exp(log_a)
    th = jnp.tanh(log_a)
    uu = jnp.sqrt((-2.0 * th) / (1.0 - th)) * (ig * xc)
    yield

    aa = grouped(a)
    hh = grouped(uu)
    s = 1
    while s < SUBLANES:
        h_sh = jnp.where(sub >= s, pltpu.roll(hh, s, 1), 0.0)
        a_sh = jnp.where(sub >= s, pltpu.roll(aa, s, 1), 1.0)
        hh = aa * h_sh + hh
        aa = aa * a_sh
        s *= 2
    a_last = jnp.broadcast_to(aa[:, SUBLANES - 1:, :], aa.shape)
    h_last = jnp.broadcast_to(hh[:, SUBLANES - 1:, :], hh.shape)
    carry = jnp.broadcast_to(hcar[...], (SUBLANES, W_LRU))
    groups = []
    for gi in range(n_groups):
        groups.append(aa[gi] * carry + hh[gi])
        carry = a_last[gi] * carry + h_last[gi]
    hh = jnp.concatenate(groups, axis=0)
    hcar[...] = carry[0:1, :]
    yield

    gelu = ly *(0.5 * (1.0 + jnp.tanh(0.7978845608028654 * (ly + 0.044715 * (ly * ly * ly)))))
    out_b = hh * gelu
    y = jnp.concatenate([out_a, out_b], axis=1).astype(bf16)
    h1_ref[...] = x + jnp.dot(y, wout_ref[...], preferred_element_type=f32)
    return cv_end, lx_end, hh[tl - 1:tl, :]


def _mixer(x, cs0, cl0, h0, mw, tl, streams):
    nb, seq, _ = x.shape
    assert seq % tl == 0 and tl % SUBLANES == 0 and nb % streams == 0
    full = lambda a: pl.BlockSpec(a.shape, lambda b, j: (0,) * a.ndim)
    per_b = lambda a: pl.BlockSpec((streams,) + a.shape[1:], lambda b, j: (b,) + (0,) * (a.ndim - 1))
    weights = (mw["g"], mw["w_in"], mw["csw"], mw["clw"], mw["clb"], mw["wg"], mw["ba"], mw["bx"],
               mw["lam"], mw["w_out"])
    return pl.pallas_call(
        functools.partial(_mixer_kernel, tl=tl),
        grid=(nb // streams, seq // tl),
        in_specs=[pl.BlockSpec((streams, tl, D_MODEL), lambda b, j: (b, j, 0)),
                  per_b(cs0), per_b(cl0), per_b(h0)] + [full(w) for w in weights],
        out_specs=[pl.BlockSpec((streams, tl, D_MODEL), lambda b, j: (b, j, 0)),
                   per_b(cs0), per_b(cl0), per_b(h0)],
        out_shape=[jax.ShapeDtypeStruct(x.shape, f32),
                   jax.ShapeDtypeStruct(cs0.shape, f32),
                   jax.ShapeDtypeStruct(cl0.shape, f32),
                   jax.ShapeDtypeStruct(h0.shape, f32)],
        scratch_shapes=[pltpu.VMEM((streams, SUBLANES, W_SHORT), f32),
                        pltpu.VMEM((streams, SUBLANES, W_LRU), f32),
                        pltpu.VMEM((streams, 1, W_LRU), f32)],
        compiler_params=pltpu.CompilerParams(
            dimension_semantics=("arbitrary", "arbitrary"), vmem_limit_bytes=VMEM_LIMIT),
        name=f"mixer_{seq}",
    )(x, cs0, cl0, h0, *weights)


def _route_tile(h, n_valid, g, w_router_t, b_router_col):
    tr = h.shape[0]
    xn = _rmsnorm(h, g)
    logits = lax.dot_general(w_router_t.astype(bf16), xn.astype(bf16), (((1,), (1,)), ((), ())),
                             preferred_element_type=f32) + b_router_col

    rows = lax.broadcasted_iota(jnp.int32, (N_EXPERTS, tr), 0)
    vals, idxs = [], []
    cur = logits
    for _ in range(TOP_K):
        m = jnp.max(cur, axis=0, keepdims=True)
        ik = jnp.min(jnp.where(cur == m, rows, N_EXPERTS), axis=0, keepdims=True)
        vals.append(m)
        idxs.append(ik)
        cur = jnp.where(rows == ik, -jnp.inf, cur)
    ex = [jnp.exp(v - vals[0]) for v in vals]
    denom = ex[0] + ex[1] + ex[2] + ex[3]

    routed = lax.broadcasted_iota(jnp.int32, (N_EXPERTS, tr), 1) < n_valid
    onehot = jnp.zeros((N_EXPERTS, tr), f32)
    for ik in idxs:
        onehot = onehot + jnp.where(jnp.logical_and(rows == ik, routed), 1.0, 0.0)
    src = lax.broadcasted_iota(jnp.int32, (tr, tr), 0)
    dst = lax.broadcasted_iota(jnp.int32, (tr, tr), 1)
    tri = jnp.where(src < dst, 1.0, 0.0).astype(bf16)
    before = jnp.dot(onehot.astype(bf16), tri, preferred_element_type=f32)
    tile_cnt = jnp.broadcast_to(jnp.sum(onehot, axis=1, keepdims=True), (N_EXPERTS, LANES))
    e_row = lax.broadcasted_iota(jnp.int32, (N_EXPERTS, LANES), 0)
    upto = tile_cnt
    s = 1
    while s < N_EXPERTS:
        upto = upto + jnp.where(e_row >= s, pltpu.roll(upto, s, 0), 0.0)
        s *= 2
    first = upto - tile_cnt
    within = before + first[:, 0:1]
    lpos = [ROW_SUB * jnp.sum(jnp.where(rows == ik, within, 0.0), axis=0, keepdims=True).astype(jnp.int32)
            for ik in idxs]
    return lpos, [e / denom for e in ex], tile_cnt, first


def _token_tile(i, n_prompt_tiles, hp_ref, hs_ref):
    n_sample = hs_ref.shape[0]
    sample = jnp.concatenate([hs_ref[...]] * (TOK_TILE // n_sample), axis=0)
    on_prompt = i < n_prompt_tiles
    return jnp.where(on_prompt, hp_ref[...], sample), jnp.where(on_prompt, TOK_TILE, n_sample)


def _router_kernel(hp_ref, hs_ref, g_ref, wrt_ref, br_ref,
                   lpos_ref, gate_ref, tab_ref, cnt_ref, carry, *, n_prompt_tiles):
    i = pl.program_id(0)

    @pl.when(i == 0)
    def _():
        carry[...] = jnp.zeros(carry.shape, f32)

    h, n_valid = _token_tile(i, n_prompt_tiles, hp_ref, hs_ref)
    lpos, gates, tile_cnt, first = _route_tile(h, n_valid, g_ref[...], wrt_ref[...], br_ref[...])
    for k in range(TOP_K):
        lpos_ref[k:k + 1, :] = lpos[k]
        gate_ref[k:k + 1, :] = gates[k]
    lane = lax.broadcasted_iota(jnp.int32, carry.shape, 1)
    tab_ref[0] = jnp.where(lane == 0, carry[...], jnp.where(lane == 1, tile_cnt, first)).astype(jnp.int32)
    carry[...] = carry[...] + tile_cnt
    cnt_ref[...] = carry[...].astype(jnp.int32)


def _router(h1p, h1s, g, w_router_t, b_router_col):
    assert h1p.shape[0] % TOK_TILE == 0 and TOK_TILE % h1s.shape[0] == 0 and h1s.shape[0] % LOOP_UNROLL == 0
    n_prompt_tiles = h1p.shape[0] // TOK_TILE
    n_steps = n_prompt_tiles + 1
    const = lambda a: pl.BlockSpec(a.shape, lambda s: (0,) * a.ndim)
    lane_blk = pl.BlockSpec((TOP_K, TOK_TILE), lambda s: (0, s))
    return pl.pallas_call(
        functools.partial(_router_kernel, n_prompt_tiles=n_prompt_tiles),
        grid=(n_steps,),
        in_specs=[pl.BlockSpec((TOK_TILE, D_MODEL), lambda s: (jnp.minimum(s, n_prompt_tiles - 1), 0)),
                  const(h1s), const(g), const(w_router_t), const(b_router_col)],
        out_specs=[lane_blk, lane_blk,
                   pl.BlockSpec((1, N_EXPERTS, LANES), lambda s: (s, 0, 0)),
                   pl.BlockSpec((N_EXPERTS, LANES), lambda s: (0, 0))],
        out_shape=[jax.ShapeDtypeStruct((TOP_K, n_steps * TOK_TILE), jnp.int32),
                   jax.ShapeDtypeStruct((TOP_K, n_steps * TOK_TILE), f32),
                   jax.ShapeDtypeStruct((n_steps, N_EXPERTS, LANES), jnp.int32),
                   jax.ShapeDtypeStruct((N_EXPERTS, LANES), jnp.int32)],
        scratch_shapes=[pltpu.VMEM((N_EXPERTS, LANES), f32)],
        compiler_params=pltpu.CompilerParams(
            dimension_semantics=("arbitrary",), vmem_limit_bytes=VMEM_LIMIT),
        name="router",
    )(h1p, h1s, g, w_router_t, b_router_col)


def _wait_rows(n, src_ref, dst_ref, sem):
    pltpu.make_async_copy(src_ref.at[pl.ds(0, n * ROW_SUB)], dst_ref.at[pl.ds(0, n * ROW_SUB)], sem).wait()


def _rows(ref, first, n):
    return ref.at[pl.ds(pl.multiple_of(first * ROW_SUB, ROW_SUB), n * ROW_SUB)]


FILL_CHUNKS = tuple(c for c in (1 << b for b in reversed(range(EXP_ROWS.bit_length()))) if c < EXP_ROWS)
TAIL_CHUNK = min(EXP_ROWS & -EXP_ROWS, FILL_CHUNKS[0])
RUN_CHUNKS = tuple(1 << b for b in reversed(range(TOK_TILE.bit_length())))
RUN_LARGE = 2 * TOK_TILE * TOP_K // N_EXPERTS


def _start_runs(tab_ref, src_ref, dst_ref, sem, *, src_is_sorted_tile):
    for e in range(N_EXPERTS):
        hbm_first = tab_ref[0, 0, e]
        length = tab_ref[0, 0, N_EXPERTS + e]
        tile_first = tab_ref[0, 0, 2 * N_EXPERTS + e]
        src_first, dst_first = (tile_first, hbm_first) if src_is_sorted_tile else (hbm_first, tile_first)

        def start_pieces(chunks):
            for chunk in chunks:
                offset = length & ~(2 * chunk - 1)
                copy = pltpu.make_async_copy(_rows(src_ref, src_first + offset, chunk),
                                             _rows(dst_ref, dst_first + offset, chunk), sem)
                pl.when((length & chunk) != 0)(copy.start)

        large = tuple(c for c in RUN_CHUNKS if c >= RUN_LARGE)
        pl.when(length >= RUN_LARGE)(functools.partial(start_pieces, large))
        start_pieces(tuple(c for c in RUN_CHUNKS if c < RUN_LARGE))


def _zero_fill_copies(fs_ref, fl_ref, zbuf, xs_hbm, sem):
    out = []
    for e in range(N_EXPERTS):
        start, length = fs_ref[e], fl_ref[e]
        for chunk in FILL_CHUNKS:
            offset = length & ~(2 * chunk - 1)
            first = pl.multiple_of((start + offset) * ROW_SUB, ROW_SUB)
            copy = pltpu.make_async_copy(zbuf.at[pl.ds(0, chunk * ROW_SUB)],
                                         xs_hbm.at[pl.ds(first, chunk * ROW_SUB)], sem)
            out.append(((length & chunk) != 0, copy))
    start, length = fs_ref[N_EXPERTS], fl_ref[N_EXPERTS]
    chunk = TAIL_CHUNK
    for c in range(N_EXPERTS * EXP_ROWS // chunk):
        first = pl.multiple_of((start + c * chunk) * ROW_SUB, ROW_SUB)
        copy = pltpu.make_async_copy(zbuf.at[pl.ds(0, chunk * ROW_SUB)],
                                     xs_hbm.at[pl.ds(first, chunk * ROW_SUB)], sem)
        out.append((c * chunk < length, copy))
    return out


def _dispatch_kernel(fs_ref, fl_ref, lpos_ref, tab_ref, hp_ref, hs_ref, g_ref, xs_hbm,
                     xn_tiles, sorted_buf, zbuf, sem, zsem, *, n_prompt_tiles):
    i = pl.program_id(0)
    last = pl.num_programs(0) - 1
    tr = hp_ref.shape[0]
    n_rows = TOP_K * tr
    slot = i % 2

    @pl.when(i == 0)
    def _():
        zbuf[...] = jnp.zeros(zbuf.shape, f32)
        for present, copy in _zero_fill_copies(fs_ref, fl_ref, zbuf, xs_hbm, zsem):
            pl.when(present)(copy.start)

    h, n_valid = _token_tile(i, n_prompt_tiles, hp_ref, hs_ref)
    _store_rows_as_tiles(xn_tiles, _rmsnorm(h, g_ref[...]))

    @pl.when(i >= 2)
    def _():
        _wait_rows(n_rows, sorted_buf.at[slot], xs_hbm, sem.at[slot])

    def place(group, carry):
        for u in range(LOOP_UNROLL):
            t = group * LOOP_UNROLL + u
            row = _row_tile(xn_tiles, t)[...]
            for k in range(TOP_K):
                _row_tile_at(sorted_buf.at[slot], lpos_ref[0, 0, k * tr + t])[...] = row
        return carry
    lax.fori_loop(0, n_valid // LOOP_UNROLL, place, 0)

    _start_runs(tab_ref, sorted_buf.at[slot], xs_hbm, sem.at[slot], src_is_sorted_tile=True)

    @pl.when(i == last)
    def _():
        _wait_rows(n_rows, sorted_buf.at[1 - slot], xs_hbm, sem.at[1 - slot])
        _wait_rows(TOP_K * hs_ref.shape[0], sorted_buf.at[slot], xs_hbm, sem.at[slot])
        for present, copy in _zero_fill_copies(fs_ref, fl_ref, zbuf, xs_hbm, zsem):
            pl.when(present)(copy.wait)


def _dispatch(fill_start, fill_len, lpos_tiles, tab_tiles, h1p, h1s, g, n_rows):
    n_prompt_tiles = h1p.shape[0] // TOK_TILE
    n_tiles = lpos_tiles.shape[0]
    assert n_tiles >= 2
    smem_blk = lambda a: pl.BlockSpec((1, 1, a.shape[2]), lambda i, fs, fl: (i, 0, 0), memory_space=pltpu.SMEM)
    const = lambda a: pl.BlockSpec(a.shape, lambda i, fs, fl: (0,) * a.ndim)
    grid_spec = pltpu.PrefetchScalarGridSpec(
        num_scalar_prefetch=2,
        grid=(n_tiles,),
        in_specs=[smem_blk(lpos_tiles), smem_blk(tab_tiles),
                  pl.BlockSpec((TOK_TILE, D_MODEL), lambda i, fs, fl: (jnp.minimum(i, n_prompt_tiles - 1), 0)),
                  const(h1s), const(g)],
        out_specs=pl.BlockSpec(memory_space=pl.ANY),
        scratch_shapes=[pltpu.VMEM((TOK_TILE * ROW_SUB, LANES), f32),
                        pltpu.VMEM((2, TOP_K * TOK_TILE * ROW_SUB, LANES), f32),
                        pltpu.VMEM((FILL_CHUNKS[0] * ROW_SUB, LANES), f32),
                        pltpu.SemaphoreType.DMA((2,)), pltpu.SemaphoreType.DMA(())],
    )
    return pl.pallas_call(
        functools.partial(_dispatch_kernel, n_prompt_tiles=n_prompt_tiles),
        grid_spec=grid_spec,
        out_shape=jax.ShapeDtypeStruct((n_rows * ROW_SUB, LANES), f32),
        compiler_params=pltpu.CompilerParams(
            dimension_semantics=("arbitrary",), vmem_limit_bytes=VMEM_LIMIT),
        name="dispatch",
    )(fill_start, fill_len, lpos_tiles, tab_tiles, h1p, h1s, g)


def _expert_kernel(be_ref, nu_ref, xs_ref, wgu_ref, bgu_ref, wd_ref, bd_ref, ys_ref, wgu_bf, wd_bf):
    i = pl.program_id(0)
    n_used = nu_ref[0]
    new_expert = jnp.logical_or(i == 0, be_ref[i] != be_ref[jnp.maximum(i - 1, 0)])

    @pl.when(jnp.logical_and(i < n_used, new_expert))
    def _():
        wgu_bf[...] = wgu_ref[0].astype(bf16)
        wd_bf[...] = wd_ref[0].astype(bf16)

    @pl.when(i < n_used)
    def _():
        x = _load_tiles_as_rows(xs_ref, 0, EXP_ROWS).astype(bf16)
        gu = jnp.dot(x, wgu_bf[...], preferred_element_type=f32) + bgu_ref[0]
        gate = jnp.minimum(gu[:, :D_FF], SWIGLU_LIMIT)
        up = jnp.clip(gu[:, D_FF:], -SWIGLU_LIMIT, SWIGLU_LIMIT)
        glu = gate * jax.nn.sigmoid(gate * SWIGLU_ALPHA)
        act = ((up + 1.0) * glu).astype(bf16)
        _store_rows_as_tiles(ys_ref, jnp.dot(act, wd_bf[...], preferred_element_type=f32) + bd_ref[0])

    @pl.when(i >= n_used)
    def _():
        ys_ref[...] = jnp.zeros(ys_ref.shape, f32)


def _experts(block_e, n_used, xs, w_gu, b_gu, w_down, b_down):
    n_blocks = xs.shape[0] // (EXP_ROWS * ROW_SUB)
    by_expert = lambda a: pl.BlockSpec((1,) + a.shape[1:], lambda i, be, nu: (be[i],) + (0,) * (a.ndim - 1))
    grid_spec = pltpu.PrefetchScalarGridSpec(
        num_scalar_prefetch=2,
        grid=(n_blocks,),
        in_specs=[pl.BlockSpec((EXP_ROWS * ROW_SUB, LANES),
                               lambda i, be, nu: (jnp.minimum(i, jnp.maximum(nu[0] - 1, 0)), 0)),
                  by_expert(w_gu), by_expert(b_gu), by_expert(w_down), by_expert(b_down)],
        out_specs=pl.BlockSpec((EXP_ROWS * ROW_SUB, LANES), lambda i, be, nu: (i, 0)),
        scratch_shapes=[pltpu.VMEM(w_gu.shape[1:], bf16), pltpu.VMEM(w_down.shape[1:], bf16)],
    )
    return pl.pallas_call(
        _expert_kernel,
        grid_spec=grid_spec,
        out_shape=jax.ShapeDtypeStruct(xs.shape, f32),
        compiler_params=pltpu.CompilerParams(
            dimension_semantics=("arbitrary",), vmem_limit_bytes=EXPERT_VMEM_LIMIT),
        name="experts",
    )(block_e, n_used, xs, w_gu, b_gu, w_down, b_down)


def _combine_kernel(lpos_ref, gate_ref, tab_cur, tab_nxt, hp_ref, hs_ref, ys_hbm, gf_ref,
                    yp_ref, ysm_ref, ybuf, moe_tiles, sem, *, n_prompt_tiles):
    i = pl.program_id(0)
    tr = hp_ref.shape[0]
    n_rows = TOP_K * tr
    slot = i % 2

    @pl.when(i == 0)
    def _():
        _start_runs(tab_cur, ys_hbm, ybuf.at[0], sem.at[0], src_is_sorted_tile=False)

    @pl.when(i + 1 < pl.num_programs(0))
    def _():
        _start_runs(tab_nxt, ys_hbm, ybuf.at[1 - slot], sem.at[1 - slot], src_is_sorted_tile=False)

    n_sample = hs_ref.shape[0]
    h, n_valid = _token_tile(i, n_prompt_tiles, hp_ref, hs_ref)

    @pl.when(i < n_prompt_tiles)
    def _():
        _wait_rows(n_rows, ys_hbm, ybuf.at[slot], sem.at[slot])

    @pl.when(i >= n_prompt_tiles)
    def _():
        _wait_rows(TOP_K * n_sample, ys_hbm, ybuf.at[slot], sem.at[slot])

    def mix(group, carry):
        for u in range(LOOP_UNROLL):
            t = group * LOOP_UNROLL + u
            acc = gate_ref[0, 0, t] * _row_tile_at(ybuf.at[slot], lpos_ref[0, 0, t])[...]
            for k in range(1, TOP_K):
                acc = acc + (gate_ref[0, 0, k * tr + t]
                             * _row_tile_at(ybuf.at[slot], lpos_ref[0, 0, k * tr + t])[...])
            _row_tile(moe_tiles, t)[...] = acc
        return carry
    lax.fori_loop(0, n_valid // LOOP_UNROLL, mix, 0)

    out = _rmsnorm(h + _load_tiles_as_rows(moe_tiles, 0, tr), gf_ref[...])

    @pl.when(i < n_prompt_tiles)
    def _():
        yp_ref[...] = out

    @pl.when(i >= n_prompt_tiles)
    def _():
        ysm_ref[...] = out[0:n_sample, :]


def _combine(lpos_tiles, gate_tiles, tab_tiles, h1p, h1s, ys, g_final):
    n_prompt_tiles = h1p.shape[0] // TOK_TILE
    n_tiles = lpos_tiles.shape[0]
    const = lambda a: pl.BlockSpec(a.shape, lambda i: (0,) * a.ndim)
    prompt_blk = pl.BlockSpec((TOK_TILE, D_MODEL), lambda i: (jnp.minimum(i, n_prompt_tiles - 1), 0))
    smem_blk = lambda a, ahead: pl.BlockSpec(
        (1, 1, a.shape[2]), lambda i: (jnp.minimum(i + ahead, n_tiles - 1), 0, 0), memory_space=pltpu.SMEM)
    return pl.pallas_call(
        functools.partial(_combine_kernel, n_prompt_tiles=n_prompt_tiles),
        grid=(n_tiles,),
        in_specs=[smem_blk(lpos_tiles, 0), smem_blk(gate_tiles, 0), smem_blk(tab_tiles, 0), smem_blk(tab_tiles, 1),
                  prompt_blk, const(h1s),
                  pl.BlockSpec(memory_space=pl.ANY),
                  const(g_final)],
        out_specs=[prompt_blk, const(h1s)],
        out_shape=[jax.ShapeDtypeStruct(h1p.shape, f32), jax.ShapeDtypeStruct(h1s.shape, f32)],
        scratch_shapes=[pltpu.VMEM((2, TOP_K * TOK_TILE * ROW_SUB, LANES), f32),
                        pltpu.VMEM((TOK_TILE * ROW_SUB, LANES), f32),
                        pltpu.SemaphoreType.DMA((2,))],
        compiler_params=pltpu.CompilerParams(
            dimension_semantics=("arbitrary",), vmem_limit_bytes=VMEM_LIMIT),
        name="combine",
    )(lpos_tiles, gate_tiles, tab_tiles, tab_tiles, h1p, h1s, ys, g_final)


def _block_diag_gate_weights(wa, wx):
    heads_per_half = N_LRU_HEADS // 2
    eye = jnp.eye(heads_per_half, dtype=wa.dtype)

    def bd(w):
        return jnp.einsum("hij,hg->higj", w, eye).reshape(heads_per_half * LRU_HEAD_DIM,
                                                          heads_per_half * LRU_HEAD_DIM)

    halves = [jnp.concatenate([bd(wa[s * heads_per_half:(s + 1) * heads_per_half]),
                               bd(wx[s * heads_per_half:(s + 1) * heads_per_half])], axis=1)
              for s in range(2)]
    return jnp.stack(halves).astype(bf16)


def kernel(x_prompt, x_sample, state_conv_short, state_conv_lru, state_lru_h, meta_tokens,
           norm_mix_g, w_in, conv_short_w, conv_lru_w, conv_lru_b, w_rg_a, b_rg_a, w_rg_x, b_rg_x,
           rg_lambda, w_out, norm_ffn_g, w_router, b_router, w_gate_up, b_gate_up, w_down, b_down,
           final_norm_g):
    assert norm_mix_g.shape[0] == 1, "single layer"
    bp, seq, _ = x_prompt.shape
    bs, seq_s, _ = x_sample.shape
    n_meta = meta_tokens.shape[0]
    row = lambda a: a.reshape(1, -1).astype(f32)

    mw = dict(g=row(norm_mix_g[0]), w_in=w_in[0].astype(bf16), csw=conv_short_w[0], clw=conv_lru_w[0],
              clb=row(conv_lru_b[0]), wg=_block_diag_gate_weights(w_rg_a[0], w_rg_x[0]),
              ba=row(b_rg_a[0]), bx=row(b_rg_x[0]), lam=row(rg_lambda[0]), w_out=w_out[0].astype(bf16))

    zeros = lambda *s: jnp.zeros(s, f32)
    _, cs_m, cl_m, h_m = _mixer(meta_tokens[None].astype(f32), zeros(1, 2, W_SHORT), zeros(1, 3, W_LRU),
                                zeros(1, 1, W_LRU), mw, n_meta, 1)
    rep = lambda a: jnp.broadcast_to(a, (bp,) + a.shape[1:])
    h1p, cs_p, cl_p, h_p = _mixer(x_prompt, rep(cs_m), rep(cl_m), rep(h_m), mw, MIX_ROWS, MIX_STREAMS)
    h1s, cs_s, cl_s, h_s = _mixer(x_sample, state_conv_short[0], state_conv_lru[0],
                                  state_lru_h[0][:, None, :], mw, seq_s, MIX_STREAMS)
    h1p = h1p.reshape(bp * seq, D_MODEL)
    h1s = h1s.reshape(bs * seq_s, D_MODEL)
    n_tok = h1p.shape[0] + h1s.shape[0]

    g_ffn = row(norm_ffn_g[0])
    lpos, gates, tab, counts = _router(
        h1p, h1s, g_ffn, w_router[0].T.astype(f32), b_router[0].reshape(-1, 1).astype(f32))
    n_tiles = tab.shape[0]

    counts = counts[:, 0]
    padded = (counts + EXP_ROWS - 1) // EXP_ROWS * EXP_ROWS
    pad_end = jnp.cumsum(padded)
    pad_start = pad_end - padded
    n_blocks = -(-(n_tok * TOP_K + N_EXPERTS * (EXP_ROWS - 1)) // EXP_ROWS)
    block_first_row = jnp.arange(n_blocks, dtype=jnp.int32) * EXP_ROWS
    block_e = jnp.minimum(jnp.sum(pad_end[None, :] <= block_first_row[:, None], axis=1),
                          N_EXPERTS - 1).astype(jnp.int32)
    n_used = (pad_end[-1:] // EXP_ROWS).astype(jnp.int32)
    n_rows = n_blocks * EXP_ROWS
    fill_start = jnp.concatenate([pad_start + counts, pad_end[-1:]]).astype(jnp.int32)
    fill_len = jnp.concatenate([padded - counts, n_rows - pad_end[-1:]]).astype(jnp.int32)

    tab_tiles = jnp.concatenate([pad_start[None, :] + tab[:, :, 0], tab[:, :, 1], tab[:, :, 2],
                                 jnp.zeros((n_tiles, LANES - 3 * N_EXPERTS), jnp.int32)],
                                axis=1).reshape(n_tiles, 1, LANES)
    by_tile = lambda a: a.reshape(TOP_K, n_tiles, TOK_TILE).transpose(1, 0, 2).reshape(
        n_tiles, 1, TOP_K * TOK_TILE)

    xs = _dispatch(fill_start, fill_len, by_tile(lpos), tab_tiles, h1p, h1s, g_ffn, n_rows)
    ys = _experts(block_e, n_used, xs,
                  w_gate_up[0].astype(f32), b_gate_up[0][:, None, :].astype(f32),
                  w_down[0].astype(f32), b_down[0][:, None, :].astype(f32))
    yp, ysm = _combine(by_tile(lpos), by_tile(gates), tab_tiles, h1p, h1s, ys, row(final_norm_g))

    st = lambda a: a[None]
    return (yp.reshape(bp, seq, D_MODEL), ysm.reshape(bs, seq_s, D_MODEL),
            st(cs_p), st(cl_p), st(h_p[:, 0, :]), st(cs_s), st(cl_s), st(h_s[:, 0, :]))
```

```python
import functools

import jax
import jax.numpy as jnp
from jax import lax
from jax.experimental import pallas as pl
from jax.experimental.pallas import tpu as pltpu

D_MODEL = 1024
W_SHORT = 512
W_LRU = 512
N_LRU_HEADS = 8
LRU_HEAD_DIM = W_LRU // N_LRU_HEADS
D_IN_PROJ = 3 * W_SHORT + 2 * W_LRU
N_EXPERTS = 32
TOP_K = 4
D_FF = 1024
RG_C = 8.0
SWIGLU_LIMIT = 7.0
SWIGLU_ALPHA = 1.702
EPS = 1e-6

SUBLANES = 8
MIX_ROWS = 256
MIX_STREAMS = 4
TOK_TILE = 512
LOOP_UNROLL = 16
EXP_ROWS = 768
V7X_VMEM_BYTES = 64 * 1024 * 1024
VMEM_LIMIT = V7X_VMEM_BYTES * 7 // 8
EXPERT_VMEM_LIMIT = V7X_VMEM_BYTES * 15 // 16

f32 = jnp.float32
bf16 = jnp.bfloat16


def _rmsnorm(x, g):
    return (x * lax.rsqrt(jnp.mean(x * x, axis=-1, keepdims=True) + EPS)) * g


LANES = 128
ROW_SUB = D_MODEL // LANES


def _row_tile(ref, r):
    return ref.at[pl.ds(pl.multiple_of(r * ROW_SUB, ROW_SUB), ROW_SUB)]


def _row_tile_at(ref, first_sub_row):
    return ref.at[pl.ds(pl.multiple_of(first_sub_row, ROW_SUB), ROW_SUB)]


def _store_rows_as_tiles(ref, x, first=0):
    for j in range(ROW_SUB):
        ref[pl.ds(first * ROW_SUB + j, x.shape[0], stride=ROW_SUB), :] = x[:, j * LANES:(j + 1) * LANES]


def _load_tiles_as_rows(ref, first, n):
    return jnp.concatenate([ref[pl.ds(first * ROW_SUB + j, n, stride=ROW_SUB), :] for j in range(ROW_SUB)],
                           axis=1)


def _mixer_kernel(x_ref, cs0_ref, cl0_ref, h0_ref, g_ref, win_ref, csw_ref, clw_ref, clb_ref,
                  wg_ref, ba_ref, bx_ref, lam_ref, wout_ref,
                  h1_ref, ncs_ref, ncl_ref, nh_ref,
                  cvtails, lxtails, hcars, *, tl):
    j = pl.program_id(1)
    n_streams = x_ref.shape[0]
    n_cs, n_cl = cs0_ref.shape[1], cl0_ref.shape[1]

    @pl.when(j == 0)
    def _():
        for r in range(n_streams):
            cvtails[r] = jnp.zeros((SUBLANES, W_SHORT), f32)
            cvtails[r, SUBLANES - n_cs:, :] = cs0_ref[r]
            lxtails[r] = jnp.zeros((SUBLANES, W_LRU), f32)
            lxtails[r, SUBLANES - n_cl:, :] = cl0_ref[r]
            hcars[r] = h0_ref[r]

    ends = _staggered([_mixer_chunk(x_ref.at[r], g_ref, win_ref, csw_ref, clw_ref, clb_ref, wg_ref, ba_ref,
                                    bx_ref, lam_ref, wout_ref, h1_ref.at[r], cvtails.at[r], lxtails.at[r],
                                    hcars.at[r], tl=tl) for r in range(n_streams)])

    @pl.when(j == pl.num_programs(1) - 1)
    def _():
        for r, (cv_end, lx_end, h_end) in enumerate(ends):
            ncs_ref[r] = cv_end[SUBLANES - n_cs:, :]
            ncl_ref[r] = lx_end[SUBLANES - n_cl:, :]
            nh_ref[r] = h_end


def _staggered(chains):
    results = [None] * len(chains)
    step = 0
    while any(r is None for r in results):
        for r, chain in enumerate(chains):
            if step >= r and results[r] is None:
                try:
                    next(chain)
                except StopIteration as done:
                    results[r] = done.value
        step += 1
    return results


def _delayed(x3, tail, s, sub):
    rolled = pltpu.roll(x3, s, 1)
    before = jnp.concatenate([pltpu.roll(tail, s, 0)[None], rolled[:-1]], axis=0)
    return jnp.where(sub < s, before, rolled)


def _mixer_chunk(x_ref, g_ref, win_ref, csw_ref, clw_ref, clb_ref, wg_ref, ba_ref, bx_ref, lam_ref, wout_ref,
                 h1_ref, cvtail, lxtail, hcar, *, tl):
    n_groups = tl // SUBLANES
    grouped = lambda a: a.reshape(n_groups, SUBLANES, a.shape[-1])
    sub = lax.broadcasted_iota(jnp.int32, (n_groups, SUBLANES, W_LRU), 1)
    x = x_ref[...]
    u = _rmsnorm(x, g_ref[...])
    proj = jnp.dot(u.astype(bf16), win_ref[...], preferred_element_type=f32)
    b_g = proj[:, 0:W_SHORT]
    c_g = proj[:, W_SHORT:2 * W_SHORT]
    v = proj[:, 2 * W_SHORT:3 * W_SHORT]
    lx = proj[:, 3 * W_SHORT:3 * W_SHORT + W_LRU]
    ly = proj[:, 3 * W_SHORT + W_LRU:]
    yield

    cv = grouped(c_g * v)
    cv_tail = cvtail[...]
    conv_a = (_delayed(cv, cv_tail, 2, sub) * csw_ref[0:1, :]
              + _delayed(cv, cv_tail, 1, sub) * csw_ref[1:2, :]
              + cv * csw_ref[2:3, :])
    out_a = b_g * conv_a.reshape(tl, W_SHORT)
    cv_end = cv[n_groups - 1]
    cvtail[...] = cv_end

    lx = grouped(lx)
    lx_tail = lxtail[...]
    xc = ((_delayed(lx, lx_tail, 3, sub) * clw_ref[0:1, :]
           + _delayed(lx, lx_tail, 2, sub) * clw_ref[1:2, :]
           + _delayed(lx, lx_tail, 1, sub) * clw_ref[2:3, :]
           + lx * clw_ref[3:4, :]) + clb_ref[...]).reshape(tl, W_LRU)
    lx_end = lx[n_groups - 1]
    lxtail[...] = lx_end
    xcb = xc.astype(bf16)
    half = W_LRU // 2
    g0 = jnp.dot(xcb[:, :half], wg_ref[0], preferred_element_type=f32)
    g1 = jnp.dot(xcb[:, half:], wg_ref[1], preferred_element_type=f32)
    ga = jnp.concatenate([g0[:, :half], g1[:, :half]], axis=1) + ba_ref[...]
    gx = jnp.concatenate([g0[:, half:], g1[:, half:]], axis=1) + bx_ref[...]
    r = jax.nn.sigmoid(ga)
    ig = jax.nn.sigmoid(gx)
    z = -lam_ref[...]
    softplus = jnp.maximum(z, 0.0) + jnp.log1p(jnp.exp(-jnp.abs(z)))
    log_a = (-RG_C * r) * softplus
    a = jnp.exp(log_a)
    th = jnp.tanh(log_a)
    uu = jnp.sqrt((-2.0 * th) / (1.0 - th)) * (ig * xc)
    yield

    aa = grouped(a)
    hh = grouped(uu)
    s = 1
    while s < SUBLANES:
        h_sh = jnp.where(sub >= s, pltpu.roll(hh, s, 1), 0.0)
        a_sh = jnp.where(sub >= s, pltpu.roll(aa, s, 1), 1.0)
        hh = aa * h_sh + hh
        aa = aa * a_sh
        s *= 2
    a_last = jnp.broadcast_to(aa[:, SUBLANES - 1:, :], aa.shape)
    h_last = jnp.broadcast_to(hh[:, SUBLANES - 1:, :], hh.shape)
    carry = jnp.broadcast_to(hcar[...], (SUBLANES, W_LRU))
    groups = []
    for gi in range(n_groups):
        groups.append(aa[gi] * carry + hh[gi])
        carry = a_last[gi] * carry + h_last[gi]
    hh = jnp.concatenate(groups, axis=0)
    hcar[...] = carry[0:1, :]
    yield

    gelu = ly *(0.5 * (1.0 + jnp.tanh(0.7978845608028654 * (ly + 0.044715 * (ly * ly * ly)))))
    out_b = hh * gelu
    y = jnp.concatenate([out_a, out_b], axis=1).astype(bf16)
    h1_ref[...] = x + jnp.dot(y, wout_ref[...], preferred_element_type=f32)
    return cv_end, lx_end, hh[tl - 1:tl, :]


def _mixer(x, cs0, cl0, h0, mw, tl, streams):
    nb, seq, _ = x.shape
    assert seq % tl == 0 and tl % SUBLANES == 0 and nb % streams == 0
    full = lambda a: pl.BlockSpec(a.shape, lambda b, j: (0,) * a.ndim)
    per_b = lambda a: pl.BlockSpec((streams,) + a.shape[1:], lambda b, j: (b,) + (0,) * (a.ndim - 1))
    weights = (mw["g"], mw["w_in"], mw["csw"], mw["clw"], mw["clb"], mw["wg"], mw["ba"], mw["bx"],
               mw["lam"], mw["w_out"])
    return pl.pallas_call(
        functools.partial(_mixer_kernel, tl=tl),
        grid=(nb // streams, seq // tl),
        in_specs=[pl.BlockSpec((streams, tl, D_MODEL), lambda b, j: (b, j, 0)),
                  per_b(cs0), per_b(cl0), per_b(h0)] + [full(w) for w in weights],
        out_specs=[pl.BlockSpec((streams, tl, D_MODEL), lambda b, j: (b, j, 0)),
                   per_b(cs0), per_b(cl0), per_b(h0)],
        out_shape=[jax.ShapeDtypeStruct(x.shape, f32),
                   jax.ShapeDtypeStruct(cs0.shape, f32),
                   jax.ShapeDtypeStruct(cl0.shape, f32),
                   jax.ShapeDtypeStruct(h0.shape, f32)],
        scratch_shapes=[pltpu.VMEM((streams, SUBLANES, W_SHORT), f32),
                        pltpu.VMEM((streams, SUBLANES, W_LRU), f32),
                        pltpu.VMEM((streams, 1, W_LRU), f32)],
        compiler_params=pltpu.CompilerParams(
            dimension_semantics=("arbitrary", "arbitrary"), vmem_limit_bytes=VMEM_LIMIT),
        name=f"mixer_{seq}",
    )(x, cs0, cl0, h0, *weights)


def _route_tile(h, n_valid, g, w_router_t, b_router_col):
    tr = h.shape[0]
    xn = _rmsnorm(h, g)
    logits = lax.dot_general(w_router_t.astype(bf16), xn.astype(bf16), (((1,), (1,)), ((), ())),
                             preferred_element_type=f32) + b_router_col

    rows = lax.broadcasted_iota(jnp.int32, (N_EXPERTS, tr), 0)
    vals, idxs = [], []
    cur = logits
    for _ in range(TOP_K):
        m = jnp.max(cur, axis=0, keepdims=True)
        ik = jnp.min(jnp.where(cur == m, rows, N_EXPERTS), axis=0, keepdims=True)
        vals.append(m)
        idxs.append(ik)
        cur = jnp.where(rows == ik, -jnp.inf, cur)
    ex = [jnp.exp(v - vals[0]) for v in vals]
    denom = ex[0] + ex[1] + ex[2] + ex[3]

    routed = lax.broadcasted_iota(jnp.int32, (N_EXPERTS, tr), 1) < n_valid
    onehot = jnp.zeros((N_EXPERTS, tr), f32)
    for ik in idxs:
        onehot = onehot + jnp.where(jnp.logical_and(rows == ik, routed), 1.0, 0.0)
    src = lax.broadcasted_iota(jnp.int32, (tr, tr), 0)
    dst = lax.broadcasted_iota(jnp.int32, (tr, tr), 1)
    tri = jnp.where(src < dst, 1.0, 0.0).astype(bf16)
    before = jnp.dot(onehot.astype(bf16), tri, preferred_element_type=f32)
    tile_cnt = jnp.broadcast_to(jnp.sum(onehot, axis=1, keepdims=True), (N_EXPERTS, LANES))
    e_row = lax.broadcasted_iota(jnp.int32, (N_EXPERTS, LANES), 0)
    upto = tile_cnt
    s = 1
    while s < N_EXPERTS:
        upto = upto + jnp.where(e_row >= s, pltpu.roll(upto, s, 0), 0.0)
        s *= 2
    first = upto - tile_cnt
    within = before + first[:, 0:1]
    lpos = [ROW_SUB * jnp.sum(jnp.where(rows == ik, within, 0.0), axis=0, keepdims=True).astype(jnp.int32)
            for ik in idxs]
    return lpos, [e / denom for e in ex], tile_cnt, first


def _token_tile(i, n_prompt_tiles, hp_ref, hs_ref):
    n_sample = hs_ref.shape[0]
    sample = jnp.concatenate([hs_ref[...]] * (TOK_TILE // n_sample), axis=0)
    on_prompt = i < n_prompt_tiles
    return jnp.where(on_prompt, hp_ref[...], sample), jnp.where(on_prompt, TOK_TILE, n_sample)


def _router_kernel(hp_ref, hs_ref, g_ref, wrt_ref, br_ref,
                   lpos_ref, gate_ref, tab_ref, cnt_ref, carry, *, n_prompt_tiles):
    i = pl.program_id(0)

    @pl.when(i == 0)
    def _():
        carry[...] = jnp.zeros(carry.shape, f32)

    h, n_valid = _token_tile(i, n_prompt_tiles, hp_ref, hs_ref)
    lpos, gates, tile_cnt, first = _route_tile(h, n_valid, g_ref[...], wrt_ref[...], br_ref[...])
    for k in range(TOP_K):
        lpos_ref[k:k + 1, :] = lpos[k]
        gate_ref[k:k + 1, :] = gates[k]
    lane = lax.broadcasted_iota(jnp.int32, carry.shape, 1)
    tab_ref[0] = jnp.where(lane == 0, carry[...], jnp.where(lane == 1, tile_cnt, first)).astype(jnp.int32)
    carry[...] = carry[...] + tile_cnt
    cnt_ref[...] = carry[...].astype(jnp.int32)


def _router(h1p, h1s, g, w_router_t, b_router_col):
    assert h1p.shape[0] % TOK_TILE == 0 and TOK_TILE % h1s.shape[0] == 0 and h1s.shape[0] % LOOP_UNROLL == 0
    n_prompt_tiles = h1p.shape[0] // TOK_TILE
    n_steps = n_prompt_tiles + 1
    const = lambda a: pl.BlockSpec(a.shape, lambda s: (0,) * a.ndim)
    lane_blk = pl.BlockSpec((TOP_K, TOK_TILE), lambda s: (0, s))
    return pl.pallas_call(
        functools.partial(_router_kernel, n_prompt_tiles=n_prompt_tiles),
        grid=(n_steps,),
        in_specs=[pl.BlockSpec((TOK_TILE, D_MODEL), lambda s: (jnp.minimum(s, n_prompt_tiles - 1), 0)),
                  const(h1s), const(g), const(w_router_t), const(b_router_col)],
        out_specs=[lane_blk, lane_blk,
                   pl.BlockSpec((1, N_EXPERTS, LANES), lambda s: (s, 0, 0)),
                   pl.BlockSpec((N_EXPERTS, LANES), lambda s: (0, 0))],
        out_shape=[jax.ShapeDtypeStruct((TOP_K, n_steps * TOK_TILE), jnp.int32),
                   jax.ShapeDtypeStruct((TOP_K, n_steps * TOK_TILE), f32),
                   jax.ShapeDtypeStruct((n_steps, N_EXPERTS, LANES), jnp.int32),
                   jax.ShapeDtypeStruct((N_EXPERTS, LANES), jnp.int32)],
        scratch_shapes=[pltpu.VMEM((N_EXPERTS, LANES), f32)],
        compiler_params=pltpu.CompilerParams(
            dimension_semantics=("arbitrary",), vmem_limit_bytes=VMEM_LIMIT),
        name="router",
    )(h1p, h1s, g, w_router_t, b_router_col)


def _wait_rows(n, src_ref, dst_ref, sem):
    pltpu.make_async_copy(src_ref.at[pl.ds(0, n * ROW_SUB)], dst_ref.at[pl.ds(0, n * ROW_SUB)], sem).wait()


def _rows(ref, first, n):
    return ref.at[pl.ds(pl.multiple_of(first * ROW_SUB, ROW_SUB), n * ROW_SUB)]


FILL_CHUNKS = tuple(c for c in (1 << b for b in reversed(range(EXP_ROWS.bit_length()))) if c < EXP_ROWS)
TAIL_CHUNK = min(EXP_ROWS & -EXP_ROWS, FILL_CHUNKS[0])
RUN_CHUNKS = tuple(1 << b for b in reversed(range(TOK_TILE.bit_length())))
RUN_LARGE = 2 * TOK_TILE * TOP_K // N_EXPERTS


def _start_runs(tab_ref, src_ref, dst_ref, sem, *, src_is_sorted_tile):
    for e in range(N_EXPERTS):
        hbm_first = tab_ref[0, 0, e]
        length = tab_ref[0, 0, N_EXPERTS + e]
        tile_first = tab_ref[0, 0, 2 * N_EXPERTS + e]
        src_first, dst_first = (tile_first, hbm_first) if src_is_sorted_tile else (hbm_first, tile_first)

        def start_pieces(chunks):
            for chunk in chunks:
                offset = length & ~(2 * chunk - 1)
                copy = pltpu.make_async_copy(_rows(src_ref, src_first + offset, chunk),
                                             _rows(dst_ref, dst_first + offset, chunk), sem)
                pl.when((length & chunk) != 0)(copy.start)

        large = tuple(c for c in RUN_CHUNKS if c >= RUN_LARGE)
        pl.when(length >= RUN_LARGE)(functools.partial(start_pieces, large))
        start_pieces(tuple(c for c in RUN_CHUNKS if c < RUN_LARGE))


def _zero_fill_copies(fs_ref, fl_ref, zbuf, xs_hbm, sem):
    out = []
    for e in range(N_EXPERTS):
        start, length = fs_ref[e], fl_ref[e]
        for chunk in FILL_CHUNKS:
            offset = length & ~(2 * chunk - 1)
            first = pl.multiple_of((start + offset) * ROW_SUB, ROW_SUB)
            copy = pltpu.make_async_copy(zbuf.at[pl.ds(0, chunk * ROW_SUB)],
                                         xs_hbm.at[pl.ds(first, chunk * ROW_SUB)], sem)
            out.append(((length & chunk) != 0, copy))
    start, length = fs_ref[N_EXPERTS], fl_ref[N_EXPERTS]
    chunk = TAIL_CHUNK
    for c in range(N_EXPERTS * EXP_ROWS // chunk):
        first = pl.multiple_of((start + c * chunk) * ROW_SUB, ROW_SUB)
        copy = pltpu.make_async_copy(zbuf.at[pl.ds(0, chunk * ROW_SUB)],
                                     xs_hbm.at[pl.ds(first, chunk * ROW_SUB)], sem)
        out.append((c * chunk < length, copy))
    return out


def _dispatch_kernel(fs_ref, fl_ref, lpos_ref, tab_ref, hp_ref, hs_ref, g_ref, xs_hbm,
                     xn_tiles, sorted_buf, zbuf, sem, zsem, *, n_prompt_tiles):
    i = pl.program_id(0)
    last = pl.num_programs(0) - 1
    tr = hp_ref.shape[0]
    n_rows = TOP_K * tr
    slot = i % 2

    @pl.when(i == 0)
    def _():
        zbuf[...] = jnp.zeros(zbuf.shape, f32)
        for present, copy in _zero_fill_copies(fs_ref, fl_ref, zbuf, xs_hbm, zsem):
            pl.when(present)(copy.start)

    h, n_valid = _token_tile(i, n_prompt_tiles, hp_ref, hs_ref)
    _store_rows_as_tiles(xn_tiles, _rmsnorm(h, g_ref[...]))

    @pl.when(i >= 2)
    def _():
        _wait_rows(n_rows, sorted_buf.at[slot], xs_hbm, sem.at[slot])

    def place(group, carry):
        for u in range(LOOP_UNROLL):
            t = group * LOOP_UNROLL + u
            row = _row_tile(xn_tiles, t)[...]
            for k in range(TOP_K):
                _row_tile_at(sorted_buf.at[slot], lpos_ref[0, 0, k * tr + t])[...] = row
        return carry
    lax.fori_loop(0, n_valid // LOOP_UNROLL, place, 0)

    _start_runs(tab_ref, sorted_buf.at[slot], xs_hbm, sem.at[slot], src_is_sorted_tile=True)

    @pl.when(i == last)
    def _():
        _wait_rows(n_rows, sorted_buf.at[1 - slot], xs_hbm, sem.at[1 - slot])
        _wait_rows(TOP_K * hs_ref.shape[0], sorted_buf.at[slot], xs_hbm, sem.at[slot])
        for present, copy in _zero_fill_copies(fs_ref, fl_ref, zbuf, xs_hbm, zsem):
            pl.when(present)(copy.wait)


def _dispatch(fill_start, fill_len, lpos_tiles, tab_tiles, h1p, h1s, g, n_rows):
    n_prompt_tiles = h1p.shape[0] // TOK_TILE
    n_tiles = lpos_tiles.shape[0]
    assert n_tiles >= 2
    smem_blk = lambda a: pl.BlockSpec((1, 1, a.shape[2]), lambda i, fs, fl: (i, 0, 0), memory_space=pltpu.SMEM)
    const = lambda a: pl.BlockSpec(a.shape, lambda i, fs, fl: (0,) * a.ndim)
    grid_spec = pltpu.PrefetchScalarGridSpec(
        num_scalar_prefetch=2,
        grid=(n_tiles,),
        in_specs=[smem_blk(lpos_tiles), smem_blk(tab_tiles),
                  pl.BlockSpec((TOK_TILE, D_MODEL), lambda i, fs, fl: (jnp.minimum(i, n_prompt_tiles - 1), 0)),
                  const(h1s), const(g)],
        out_specs=pl.BlockSpec(memory_space=pl.ANY),
        scratch_shapes=[pltpu.VMEM((TOK_TILE * ROW_SUB, LANES), f32),
                        pltpu.VMEM((2, TOP_K * TOK_TILE * ROW_SUB, LANES), f32),
                        pltpu.VMEM((FILL_CHUNKS[0] * ROW_SUB, LANES), f32),
                        pltpu.SemaphoreType.DMA((2,)), pltpu.SemaphoreType.DMA(())],
    )
    return pl.pallas_call(
        functools.partial(_dispatch_kernel, n_prompt_tiles=n_prompt_tiles),
        grid_spec=grid_spec,
        out_shape=jax.ShapeDtypeStruct((n_rows * ROW_SUB, LANES), f32),
        compiler_params=pltpu.CompilerParams(
            dimension_semantics=("arbitrary",), vmem_limit_bytes=VMEM_LIMIT),
        name="dispatch",
    )(fill_start, fill_len, lpos_tiles, tab_tiles, h1p, h1s, g)


def _expert_kernel(be_ref, nu_ref, xs_ref, wgu_ref, bgu_ref, wd_ref, bd_ref, ys_ref, wgu_bf, wd_bf):
    i = pl.program_id(0)
    n_used = nu_ref[0]
    new_expert = jnp.logical_or(i == 0, be_ref[i] != be_ref[jnp.maximum(i - 1, 0)])

    @pl.when(jnp.logical_and(i < n_used, new_expert))
    def _():
        wgu_bf[...] = wgu_ref[0].astype(bf16)
        wd_bf[...] = wd_ref[0].astype(bf16)

    @pl.when(i < n_used)
    def _():
        x = _load_tiles_as_rows(xs_ref, 0, EXP_ROWS).astype(bf16)
        gu = jnp.dot(x, wgu_bf[...], preferred_element_type=f32) + bgu_ref[0]
        gate = jnp.minimum(gu[:, :D_FF], SWIGLU_LIMIT)
        up = jnp.clip(gu[:, D_FF:], -SWIGLU_LIMIT, SWIGLU_LIMIT)
        glu = gate * jax.nn.sigmoid(gate * SWIGLU_ALPHA)
        act = ((up + 1.0) * glu).astype(bf16)
        _store_rows_as_tiles(ys_ref, jnp.dot(act, wd_bf[...], preferred_element_type=f32) + bd_ref[0])

    @pl.when(i >= n_used)
    def _():
        ys_ref[...] = jnp.zeros(ys_ref.shape, f32)


def _experts(block_e, n_used, xs, w_gu, b_gu, w_down, b_down):
    n_blocks = xs.shape[0] // (EXP_ROWS * ROW_SUB)
    by_expert = lambda a: pl.BlockSpec((1,) + a.shape[1:], lambda i, be, nu: (be[i],) + (0,) * (a.ndim - 1))
    grid_spec = pltpu.PrefetchScalarGridSpec(
        num_scalar_prefetch=2,
        grid=(n_blocks,),
        in_specs=[pl.BlockSpec((EXP_ROWS * ROW_SUB, LANES),
                               lambda i, be, nu: (jnp.minimum(i, jnp.maximum(nu[0] - 1, 0)), 0)),
                  by_expert(w_gu), by_expert(b_gu), by_expert(w_down), by_expert(b_down)],
        out_specs=pl.BlockSpec((EXP_ROWS * ROW_SUB, LANES), lambda i, be, nu: (i, 0)),
        scratch_shapes=[pltpu.VMEM(w_gu.shape[1:], bf16), pltpu.VMEM(w_down.shape[1:], bf16)],
    )
    return pl.pallas_call(
        _expert_kernel,
        grid_spec=grid_spec,
        out_shape=jax.ShapeDtypeStruct(xs.shape, f32),
        compiler_params=pltpu.CompilerParams(
            dimension_semantics=("arbitrary",), vmem_limit_bytes=EXPERT_VMEM_LIMIT),
        name="experts",
    )(block_e, n_used, xs, w_gu, b_gu, w_down, b_down)


def _combine_kernel(lpos_ref, gate_ref, tab_cur, tab_nxt, hp_ref, hs_ref, ys_hbm, gf_ref,
                    yp_ref, ysm_ref, ybuf, moe_tiles, sem, *, n_prompt_tiles):
    i = pl.program_id(0)
    tr = hp_ref.shape[0]
    n_rows = TOP_K * tr
    slot = i % 2

    @pl.when(i == 0)
    def _():
        _start_runs(tab_cur, ys_hbm, ybuf.at[0], sem.at[0], src_is_sorted_tile=False)

    @pl.when(i + 1 < pl.num_programs(0))
    def _():
        _start_runs(tab_nxt, ys_hbm, ybuf.at[1 - slot], sem.at[1 - slot], src_is_sorted_tile=False)

    n_sample = hs_ref.shape[0]
    h, n_valid = _token_tile(i, n_prompt_tiles, hp_ref, hs_ref)

    @pl.when(i < n_prompt_tiles)
    def _():
        _wait_rows(n_rows, ys_hbm, ybuf.at[slot], sem.at[slot])

    @pl.when(i >= n_prompt_tiles)
    def _():
        _wait_rows(TOP_K * n_sample, ys_hbm, ybuf.at[slot], sem.at[slot])

    def mix(group, carry):
        for u in range(LOOP_UNROLL):
            t = group * LOOP_UNROLL + u
            acc = gate_ref[0, 0, t] * _row_tile_at(ybuf.at[slot], lpos_ref[0, 0, t])[...]
            for k in range(1, TOP_K):
                acc = acc + (gate_ref[0, 0, k * tr + t]
                             * _row_tile_at(ybuf.at[slot], lpos_ref[0, 0, k * tr + t])[...])
            _row_tile(moe_tiles, t)[...] = acc
        return carry
    lax.fori_loop(0, n_valid // LOOP_UNROLL, mix, 0)

    out = _rmsnorm(h + _load_tiles_as_rows(moe_tiles, 0, tr), gf_ref[...])

    @pl.when(i < n_prompt_tiles)
    def _():
        yp_ref[...] = out

    @pl.when(i >= n_prompt_tiles)
    def _():
        ysm_ref[...] = out[0:n_sample, :]


def _combine(lpos_tiles, gate_tiles, tab_tiles, h1p, h1s, ys, g_final):
    n_prompt_tiles = h1p.shape[0] // TOK_TILE
    n_tiles = lpos_tiles.shape[0]
    const = lambda a: pl.BlockSpec(a.shape, lambda i: (0,) * a.ndim)
    prompt_blk = pl.BlockSpec((TOK_TILE, D_MODEL), lambda i: (jnp.minimum(i, n_prompt_tiles - 1), 0))
    smem_blk = lambda a, ahead: pl.BlockSpec(
        (1, 1, a.shape[2]), lambda i: (jnp.minimum(i + ahead, n_tiles - 1), 0, 0), memory_space=pltpu.SMEM)
    return pl.pallas_call(
        functools.partial(_combine_kernel, n_prompt_tiles=n_prompt_tiles),
        grid=(n_tiles,),
        in_specs=[smem_blk(lpos_tiles, 0), smem_blk(gate_tiles, 0), smem_blk(tab_tiles, 0), smem_blk(tab_tiles, 1),
                  prompt_blk, const(h1s),
                  pl.BlockSpec(memory_space=pl.ANY),
                  const(g_final)],
        out_specs=[prompt_blk, const(h1s)],
        out_shape=[jax.ShapeDtypeStruct(h1p.shape, f32), jax.ShapeDtypeStruct(h1s.shape, f32)],
        scratch_shapes=[pltpu.VMEM((2, TOP_K * TOK_TILE * ROW_SUB, LANES), f32),
                        pltpu.VMEM((TOK_TILE * ROW_SUB, LANES), f32),
                        pltpu.SemaphoreType.DMA((2,))],
        compiler_params=pltpu.CompilerParams(
            dimension_semantics=("arbitrary",), vmem_limit_bytes=VMEM_LIMIT),
        name="combine",
    )(lpos_tiles, gate_tiles, tab_tiles, tab_tiles, h1p, h1s, ys, g_final)


def _block_diag_gate_weights(wa, wx):
    heads_per_half = N_LRU_HEADS // 2
    eye = jnp.eye(heads_per_half, dtype=wa.dtype)

    def bd(w):
        return jnp.einsum("hij,hg->higj", w, eye).reshape(heads_per_half * LRU_HEAD_DIM,
                                                          heads_per_half * LRU_HEAD_DIM)

    halves = [jnp.concatenate([bd(wa[s * heads_per_half:(s + 1) * heads_per_half]),
                               bd(wx[s * heads_per_half:(s + 1) * heads_per_half])], axis=1)
              for s in range(2)]
    return jnp.stack(halves).astype(bf16)


def kernel(x_prompt, x_sample, state_conv_short, state_conv_lru, state_lru_h, meta_tokens,
           norm_mix_g, w_in, conv_short_w, conv_lru_w, conv_lru_b, w_rg_a, b_rg_a, w_rg_x, b_rg_x,
           rg_lambda, w_out, norm_ffn_g, w_router, b_router, w_gate_up, b_gate_up, w_down, b_down,
           final_norm_g):
    assert norm_mix_g.shape[0] == 1, "single layer"
    bp, seq, _ = x_prompt.shape
    bs, seq_s, _ = x_sample.shape
    n_meta = meta_tokens.shape[0]
    row = lambda a: a.reshape(1, -1).astype(f32)

    mw = dict(g=row(norm_mix_g[0]), w_in=w_in[0].astype(bf16), csw=conv_short_w[0], clw=conv_lru_w[0],
              clb=row(conv_lru_b[0]), wg=_block_diag_gate_weights(w_rg_a[0], w_rg_x[0]),
              ba=row(b_rg_a[0]), bx=row(b_rg_x[0]), lam=row(rg_lambda[0]), w_out=w_out[0].astype(bf16))

    zeros = lambda *s: jnp.zeros(s, f32)
    _, cs_m, cl_m, h_m = _mixer(meta_tokens[None].astype(f32), zeros(1, 2, W_SHORT), zeros(1, 3, W_LRU),
                                zeros(1, 1, W_LRU), mw, n_meta, 1)
    rep = lambda a: jnp.broadcast_to(a, (bp,) + a.shape[1:])
    h1p, cs_p, cl_p, h_p = _mixer(x_prompt, rep(cs_m), rep(cl_m), rep(h_m), mw, MIX_ROWS, MIX_STREAMS)
    h1s, cs_s, cl_s, h_s = _mixer(x_sample, state_conv_short[0], state_conv_lru[0],
                                  state_lru_h[0][:, None, :], mw, seq_s, MIX_STREAMS)
    h1p = h1p.reshape(bp * seq, D_MODEL)
    h1s = h1s.reshape(bs * seq_s, D_MODEL)
    n_tok = h1p.shape[0] + h1s.shape[0]

    g_ffn = row(norm_ffn_g[0])
    lpos, gates, tab, counts = _router(
        h1p, h1s, g_ffn, w_router[0].T.astype(f32), b_router[0].reshape(-1, 1).astype(f32))
    n_tiles = tab.shape[0]

    counts = counts[:, 0]
    padded = (counts + EXP_ROWS - 1) // EXP_ROWS * EXP_ROWS
    pad_end = jnp.cumsum(padded)
    pad_start = pad_end - padded
    n_blocks = -(-(n_tok * TOP_K + N_EXPERTS * (EXP_ROWS - 1)) // EXP_ROWS)
    block_first_row = jnp.arange(n_blocks, dtype=jnp.int32) * EXP_ROWS
    block_e = jnp.minimum(jnp.sum(pad_end[None, :] <= block_first_row[:, None], axis=1),
                          N_EXPERTS - 1).astype(jnp.int32)
    n_used = (pad_end[-1:] // EXP_ROWS).astype(jnp.int32)
    n_rows = n_blocks * EXP_ROWS
    fill_start = jnp.concatenate([pad_start + counts, pad_end[-1:]]).astype(jnp.int32)
    fill_len = jnp.concatenate([padded - counts, n_rows - pad_end[-1:]]).astype(jnp.int32)

    tab_tiles = jnp.concatenate([pad_start[None, :] + tab[:, :, 0], tab[:, :, 1], tab[:, :, 2],
                                 jnp.zeros((n_tiles, LANES - 3 * N_EXPERTS), jnp.int32)],
                                axis=1).reshape(n_tiles, 1, LANES)
    by_tile = lambda a: a.reshape(TOP_K, n_tiles, TOK_TILE).transpose(1, 0, 2).reshape(
        n_tiles, 1, TOP_K * TOK_TILE)

    xs = _dispatch(fill_start, fill_len, by_tile(lpos), tab_tiles, h1p, h1s, g_ffn, n_rows)
    ys = _experts(block_e, n_used, xs,
                  w_gate_up[0].astype(f32), b_gate_up[0][:, None, :].astype(f32),
                  w_down[0].astype(f32), b_down[0][:, None, :].astype(f32))
    yp, ysm = _combine(by_tile(lpos), by_tile(gates), tab_tiles, h1p, h1s, ys, row(final_norm_g))

    st = lambda a: a[None]
    return (yp.reshape(bp, seq, D_MODEL), ysm.reshape(bs, seq_s, D_MODEL),
            st(cs_p), st(cl_p), st(h_p[:, 0, :]), st(cs_s), st(cl_s), st(h_s[:, 0, :]))
```

```python
import functools

import jax
import jax.numpy as jnp
from jax import lax
from jax.experimental import pallas as pl
from jax.experimental.pallas import tpu as pltpu

D_MODEL = 1024
W_SHORT = 512
W_LRU = 512
N_LRU_HEADS = 8
LRU_HEAD_DIM = W_LRU // N_LRU_HEADS
D_IN_PROJ = 3 * W_SHORT + 2 * W_LRU
N_EXPERTS = 32
TOP_K = 4
D_FF = 1024
RG_C = 8.0
SWIGLU_LIMIT = 7.0
SWIGLU_ALPHA = 1.702
EPS = 1e-6

SUBLANES = 8
MIX_ROWS = 256
MIX_STREAMS = 4
TOK_TILE = 1024
LOOP_UNROLL = 16
EXP_ROWS = 768
V7X_VMEM_BYTES = 64 * 1024 * 1024
VMEM_LIMIT = V7X_VMEM_BYTES * 7 // 8
BIG_VMEM_LIMIT = V7X_VMEM_BYTES * 15 // 16

f32 = jnp.float32
bf16 = jnp.bfloat16


def _rmsnorm(x, g):
    return (x * lax.rsqrt(jnp.mean(x * x, axis=-1, keepdims=True) + EPS)) * g


LANES = 128
ROW_SUB = D_MODEL // LANES


def _row_tile(ref, r):
    return ref.at[pl.ds(pl.multiple_of(r * ROW_SUB, ROW_SUB), ROW_SUB)]


def _row_tile_at(ref, first_sub_row):
    return ref.at[pl.ds(pl.multiple_of(first_sub_row, ROW_SUB), ROW_SUB)]


def _store_rows_as_tiles(ref, x, first=0):
    for j in range(ROW_SUB):
        ref[pl.ds(first * ROW_SUB + j, x.shape[0], stride=ROW_SUB), :] = x[:, j * LANES:(j + 1) * LANES]


def _load_tiles_as_rows(ref, first, n):
    return jnp.concatenate([ref[pl.ds(first * ROW_SUB + j, n, stride=ROW_SUB), :] for j in range(ROW_SUB)],
                           axis=1)


def _mixer_kernel(x_ref, cs0_ref, cl0_ref, h0_ref, g_ref, win_ref, csw_ref, clw_ref, clb_ref,
                  wg_ref, ba_ref, bx_ref, lam_ref, wout_ref,
                  h1_ref, ncs_ref, ncl_ref, nh_ref,
                  cvtails, lxtails, hcars, *, tl):
    j = pl.program_id(1)
    n_streams = x_ref.shape[0]
    n_cs, n_cl = cs0_ref.shape[1], cl0_ref.shape[1]

    @pl.when(j == 0)
    def _():
        for r in range(n_streams):
            cvtails[r] = jnp.zeros((SUBLANES, W_SHORT), f32)
            cvtails[r, SUBLANES - n_cs:, :] = cs0_ref[r]
            lxtails[r] = jnp.zeros((SUBLANES, W_LRU), f32)
            lxtails[r, SUBLANES - n_cl:, :] = cl0_ref[r]
            hcars[r] = h0_ref[r]

    ends = _staggered([_mixer_chunk(x_ref.at[r], g_ref, win_ref, csw_ref, clw_ref, clb_ref, wg_ref, ba_ref,
                                    bx_ref, lam_ref, wout_ref, h1_ref.at[r], cvtails.at[r], lxtails.at[r],
                                    hcars.at[r], tl=tl) for r in range(n_streams)])

    @pl.when(j == pl.num_programs(1) - 1)
    def _():
        for r, (cv_end, lx_end, h_end) in enumerate(ends):
            ncs_ref[r] = cv_end[SUBLANES - n_cs:, :]
            ncl_ref[r] = lx_end[SUBLANES - n_cl:, :]
            nh_ref[r] = h_end


def _staggered(chains):
    results = {}
    step = 0
    while len(results) < len(chains):
        for r, chain in enumerate(chains):
            if step >= r and r not in results:
                try:
                    next(chain)
                except StopIteration as done:
                    results[r] = done.value
        step += 1
    return [results[r] for r in range(len(chains))]


def _delayed(x3, tail, s, sub):
    rolled = pltpu.roll(x3, s, 1)
    before = jnp.concatenate([pltpu.roll(tail, s, 0)[None], rolled[:-1]], axis=0)
    return jnp.where(sub < s, before, rolled)


def _mixer_chunk(x_ref, g_ref, win_ref, csw_ref, clw_ref, clb_ref, wg_ref, ba_ref, bx_ref, lam_ref, wout_ref,
                 h1_ref, cvtail, lxtail, hcar, *, tl):
    n_groups = tl // SUBLANES
    grouped = lambda a: a.reshape(n_groups, SUBLANES, a.shape[-1])
    sub = lax.broadcasted_iota(jnp.int32, (n_groups, SUBLANES, W_LRU), 1)
    x = x_ref[...]
    u = _rmsnorm(x, g_ref[...])
    proj = jnp.dot(u.astype(bf16), win_ref[...], preferred_element_type=f32)
    b_g = proj[:, 0:W_SHORT]
    c_g = proj[:, W_SHORT:2 * W_SHORT]
    v = proj[:, 2 * W_SHORT:3 * W_SHORT]
    lx = proj[:, 3 * W_SHORT:3 * W_SHORT + W_LRU]
    ly = proj[:, 3 * W_SHORT + W_LRU:]
    yield

    cv = grouped(c_g * v)
    cv_tail = cvtail[...]
    conv_a = (_delayed(cv, cv_tail, 2, sub) * csw_ref[0:1, :]
              + _delayed(cv, cv_tail, 1, sub) * csw_ref[1:2, :]
              + cv * csw_ref[2:3, :])
    out_a = b_g * conv_a.reshape(tl, W_SHORT)
    cv_end = cv[n_groups - 1]
    cvtail[...] = cv_end

    lx = grouped(lx)
    lx_tail = lxtail[...]
    xc = ((_delayed(lx, lx_tail, 3, sub) * clw_ref[0:1, :]
           + _delayed(lx, lx_tail, 2, sub) * clw_ref[1:2, :]
           + _delayed(lx, lx_tail, 1, sub) * clw_ref[2:3, :]
           + lx * clw_ref[3:4, :]) + clb_ref[...]).reshape(tl, W_LRU)
    lx_end = lx[n_groups - 1]
    lxtail[...] = lx_end
    xcb = xc.astype(bf16)
    half = W_LRU // 2
    g0 = jnp.dot(xcb[:, :half], wg_ref[0], preferred_element_type=f32)
    g1 = jnp.dot(xcb[:, half:], wg_ref[1], preferred_element_type=f32)
    ga = jnp.concatenate([g0[:, :half], g1[:, :half]], axis=1) + ba_ref[...]
    gx = jnp.concatenate([g0[:, half:], g1[:, half:]], axis=1) + bx_ref[...]
    r = jax.nn.sigmoid(ga)
    ig = jax.nn.sigmoid(gx)
    z = -lam_ref[...]
    softplus = jnp.maximum(z, 0.0) + jnp.log1p(jnp.exp(-jnp.abs(z)))
    log_a = (-RG_C * r) * softplus
    a = jnp.exp(log_a)
    th = jnp.tanh(log_a)
    uu = jnp.sqrt((-2.0 * th) / (1.0 - th)) * (ig * xc)
    yield

    aa = grouped(a)
    hh = grouped(uu)
    s = 1
    while s < SUBLANES:
        h_sh = jnp.where(sub >= s, pltpu.roll(hh, s, 1), 0.0)
        a_sh = jnp.where(sub >= s, pltpu.roll(aa, s, 1), 1.0)
        hh = aa * h_sh + hh
        aa = aa * a_sh
        s *= 2
    a_last = jnp.broadcast_to(aa[:, SUBLANES - 1:, :], aa.shape)
    h_last = jnp.broadcast_to(hh[:, SUBLANES - 1:, :], hh.shape)
    carry = jnp.broadcast_to(hcar[...], (SUBLANES, W_LRU))
    groups = []
    for gi in range(n_groups):
        groups.append(aa[gi] * carry + hh[gi])
        carry = a_last[gi] * carry + h_last[gi]
    hh = jnp.concatenate(groups, axis=0)
    hcar[...] = carry[0:1, :]
    yield

    gelu = ly *(0.5 * (1.0 + jnp.tanh(0.7978845608028654 * (ly + 0.044715 * (ly * ly * ly)))))
    out_b = hh * gelu
    y = jnp.concatenate([out_a, out_b], axis=1).astype(bf16)
    h1_ref[...] = x + jnp.dot(y, wout_ref[...], preferred_element_type=f32)
    return cv_end, lx_end, hh[tl - 1:tl, :]


def _mixer(x, cs0, cl0, h0, mw, tl, streams):
    nb, seq, _ = x.shape
    assert seq % tl == 0 and tl % SUBLANES == 0 and nb % streams == 0
    full = lambda a: pl.BlockSpec(a.shape, lambda b, j: (0,) * a.ndim)
    per_b = lambda a: pl.BlockSpec((streams,) + a.shape[1:], lambda b, j: (b,) + (0,) * (a.ndim - 1))
    weights = (mw["g"], mw["w_in"], mw["csw"], mw["clw"], mw["clb"], mw["wg"], mw["ba"], mw["bx"],
               mw["lam"], mw["w_out"])
    return pl.pallas_call(
        functools.partial(_mixer_kernel, tl=tl),
        grid=(nb // streams, seq // tl),
        in_specs=[pl.BlockSpec((streams, tl, D_MODEL), lambda b, j: (b, j, 0)),
                  per_b(cs0), per_b(cl0), per_b(h0)] + [full(w) for w in weights],
        out_specs=[pl.BlockSpec((streams, tl, D_MODEL), lambda b, j: (b, j, 0)),
                   per_b(cs0), per_b(cl0), per_b(h0)],
        out_shape=[jax.ShapeDtypeStruct(x.shape, f32),
                   jax.ShapeDtypeStruct(cs0.shape, f32),
                   jax.ShapeDtypeStruct(cl0.shape, f32),
                   jax.ShapeDtypeStruct(h0.shape, f32)],
        scratch_shapes=[pltpu.VMEM((streams, SUBLANES, W_SHORT), f32),
                        pltpu.VMEM((streams, SUBLANES, W_LRU), f32),
                        pltpu.VMEM((streams, 1, W_LRU), f32)],
        compiler_params=pltpu.CompilerParams(
            dimension_semantics=("arbitrary", "arbitrary"), vmem_limit_bytes=VMEM_LIMIT),
        name=f"mixer_{seq}",
    )(x, cs0, cl0, h0, *weights)


def _route_part(h, n_routed, g, w_router_t, b_router_col):
    tr = h.shape[0]
    xn = _rmsnorm(h, g)
    logits = lax.dot_general(w_router_t.astype(bf16), xn.astype(bf16), (((1,), (1,)), ((), ())),
                             preferred_element_type=f32) + b_router_col
    yield

    rows = lax.broadcasted_iota(jnp.int32, (N_EXPERTS, tr), 0)
    vals, idxs = [], []
    cur = logits
    for _ in range(TOP_K):
        m = jnp.max(cur, axis=0, keepdims=True)
        ik = jnp.min(jnp.where(cur == m, rows, N_EXPERTS), axis=0, keepdims=True)
        vals.append(m)
        idxs.append(ik)
        cur = jnp.where(rows == ik, -jnp.inf, cur)
    ex = [jnp.exp(v - vals[0]) for v in vals]
    denom = ex[0] + ex[1] + ex[2] + ex[3]
    yield

    routed = lax.broadcasted_iota(jnp.int32, (N_EXPERTS, tr), 1) < n_routed
    onehot = jnp.zeros((N_EXPERTS, tr), f32)
    for ik in idxs:
        onehot = onehot + jnp.where(jnp.logical_and(rows == ik, routed), 1.0, 0.0)
    src = lax.broadcasted_iota(jnp.int32, (tr, tr), 0)
    dst = lax.broadcasted_iota(jnp.int32, (tr, tr), 1)
    tri = jnp.where(src < dst, 1.0, 0.0).astype(bf16)
    before = jnp.dot(onehot.astype(bf16), tri, preferred_element_type=f32)
    cnt = jnp.broadcast_to(jnp.sum(onehot, axis=1, keepdims=True), (N_EXPERTS, LANES))
    return idxs, [e / denom for e in ex], before, cnt


def _router_kernel(hp_ref, hs_ref, g_ref, wrt_ref, br_ref,
                   lpos_ref, gate_ref, tab_ref, cnt_ref, carry, *, n_prompt_tiles):
    i = pl.program_id(0)
    part = hs_ref.shape[0]
    n_parts = TOK_TILE // part

    @pl.when(i == 0)
    def _():
        carry[...] = jnp.zeros(carry.shape, f32)

    on_prompt = i < n_prompt_tiles
    n_valid = jnp.where(on_prompt, TOK_TILE, part)
    tokens = ([jnp.where(on_prompt, hp_ref[0:part, :], hs_ref[...])]
              + [hp_ref[p * part:(p + 1) * part, :] for p in range(1, n_parts)])
    routed = _staggered([_route_part(h, jnp.clip(n_valid - p * part, 0, part), g_ref[...], wrt_ref[...],
                                     br_ref[...]) for p, h in enumerate(tokens)])

    tile_cnt = sum(cnt for _, _, _, cnt in routed)
    e_row = lax.broadcasted_iota(jnp.int32, (N_EXPERTS, LANES), 0)
    upto = tile_cnt
    s = 1
    while s < N_EXPERTS:
        upto = upto + jnp.where(e_row >= s, pltpu.roll(upto, s, 0), 0.0)
        s *= 2
    first = upto - tile_cnt

    rows = lax.broadcasted_iota(jnp.int32, (N_EXPERTS, part), 0)
    part_first = first
    for p, (idxs, gates, before, cnt) in enumerate(routed):
        within = before + part_first[:, 0:1]
        for k in range(TOP_K):
            lpos_ref[k:k + 1, p * part:(p + 1) * part] = ROW_SUB * jnp.sum(
                jnp.where(rows == idxs[k], within, 0.0), axis=0, keepdims=True).astype(jnp.int32)
            gate_ref[k:k + 1, p * part:(p + 1) * part] = gates[k]
        part_first = part_first + cnt
    lane = lax.broadcasted_iota(jnp.int32, carry.shape, 1)
    tab_ref[0] = jnp.where(lane == 0, carry[...], jnp.where(lane == 1, tile_cnt, first)).astype(jnp.int32)
    carry[...] = carry[...] + tile_cnt
    cnt_ref[...] = carry[...].astype(jnp.int32)


def _router(h1p, h1s, g, w_router_t, b_router_col):
    assert h1p.shape[0] % TOK_TILE == 0 and TOK_TILE % h1s.shape[0] == 0 and h1s.shape[0] % LOOP_UNROLL == 0
    n_prompt_tiles = h1p.shape[0] // TOK_TILE
    n_steps = n_prompt_tiles + 1
    const = lambda a: pl.BlockSpec(a.shape, lambda s: (0,) * a.ndim)
    lane_blk = pl.BlockSpec((TOP_K, TOK_TILE), lambda s: (0, s))
    return pl.pallas_call(
        functools.partial(_router_kernel, n_prompt_tiles=n_prompt_tiles),
        grid=(n_steps,),
        in_specs=[pl.BlockSpec((TOK_TILE, D_MODEL), lambda s: (jnp.minimum(s, n_prompt_tiles - 1), 0)),
                  const(h1s), const(g), const(w_router_t), const(b_router_col)],
        out_specs=[lane_blk, lane_blk,
                   pl.BlockSpec((1, N_EXPERTS, LANES), lambda s: (s, 0, 0)),
                   pl.BlockSpec((N_EXPERTS, LANES), lambda s: (0, 0))],
        out_shape=[jax.ShapeDtypeStruct((TOP_K, n_steps * TOK_TILE), jnp.int32),
                   jax.ShapeDtypeStruct((TOP_K, n_steps * TOK_TILE), f32),
                   jax.ShapeDtypeStruct((n_steps, N_EXPERTS, LANES), jnp.int32),
                   jax.ShapeDtypeStruct((N_EXPERTS, LANES), jnp.int32)],
        scratch_shapes=[pltpu.VMEM((N_EXPERTS, LANES), f32)],
        compiler_params=pltpu.CompilerParams(
            dimension_semantics=("arbitrary",), vmem_limit_bytes=VMEM_LIMIT),
        name="router",
    )(h1p, h1s, g, w_router_t, b_router_col)


def _wait_rows(n, src_ref, dst_ref, sem):
    pltpu.make_async_copy(src_ref.at[pl.ds(0, n * ROW_SUB)], dst_ref.at[pl.ds(0, n * ROW_SUB)], sem).wait()


def _rows(ref, first, n):
    return ref.at[pl.ds(pl.multiple_of(first * ROW_SUB, ROW_SUB), n * ROW_SUB)]


FILL_CHUNKS = tuple(c for c in (1 << b for b in reversed(range(EXP_ROWS.bit_length()))) if c < EXP_ROWS)
TAIL_CHUNK = min(EXP_ROWS & -EXP_ROWS, FILL_CHUNKS[0])
RUN_CHUNKS = tuple(1 << b for b in reversed(range(TOK_TILE.bit_length())))
RUN_LARGE = 2 * TOK_TILE * TOP_K // N_EXPERTS


def _start_runs(tab_ref, src_ref, dst_ref, sem, *, src_is_sorted_tile):
    for e in range(N_EXPERTS):
        hbm_first = tab_ref[0, 0, e]
        length = tab_ref[0, 0, N_EXPERTS + e]
        tile_first = tab_ref[0, 0, 2 * N_EXPERTS + e]
        src_first, dst_first = (tile_first, hbm_first) if src_is_sorted_tile else (hbm_first, tile_first)

        def start_pieces(chunks):
            for chunk in chunks:
                offset = length & ~(2 * chunk - 1)
                copy = pltpu.make_async_copy(_rows(src_ref, src_first + offset, chunk),
                                             _rows(dst_ref, dst_first + offset, chunk), sem)
                pl.when((length & chunk) != 0)(copy.start)

        large = tuple(c for c in RUN_CHUNKS if c >= RUN_LARGE)
        pl.when(length >= RUN_LARGE)(functools.partial(start_pieces, large))
        start_pieces(tuple(c for c in RUN_CHUNKS if c < RUN_LARGE))


def _zero_fill_copies(fs_ref, fl_ref, zbuf, xs_hbm, sem):
    out = []
    for e in range(N_EXPERTS):
        start, length = fs_ref[e], fl_ref[e]
        for chunk in FILL_CHUNKS:
            offset = length & ~(2 * chunk - 1)
            first = pl.multiple_of((start + offset) * ROW_SUB, ROW_SUB)
            copy = pltpu.make_async_copy(zbuf.at[pl.ds(0, chunk * ROW_SUB)],
                                         xs_hbm.at[pl.ds(first, chunk * ROW_SUB)], sem)
            out.append(((length & chunk) != 0, copy))
    start, length = fs_ref[N_EXPERTS], fl_ref[N_EXPERTS]
    chunk = TAIL_CHUNK
    for c in range(N_EXPERTS * EXP_ROWS // chunk):
        first = pl.multiple_of((start + c * chunk) * ROW_SUB, ROW_SUB)
        copy = pltpu.make_async_copy(zbuf.at[pl.ds(0, chunk * ROW_SUB)],
                                     xs_hbm.at[pl.ds(first, chunk * ROW_SUB)], sem)
        out.append((c * chunk < length, copy))
    return out


def _dispatch_kernel(fs_ref, fl_ref, lpos_ref, tab_ref, hp_ref, hs_ref, g_ref, xs_hbm,
                     xn_tiles, sorted_buf, zbuf, sem, zsem, *, n_prompt_tiles):
    i = pl.program_id(0)
    last = pl.num_programs(0) - 1
    tr = hp_ref.shape[0]
    n_rows = TOP_K * tr
    slot = i % 2

    @pl.when(i == 0)
    def _():
        zbuf[...] = jnp.zeros(zbuf.shape, f32)
        for present, copy in _zero_fill_copies(fs_ref, fl_ref, zbuf, xs_hbm, zsem):
            pl.when(present)(copy.start)

    n_valid = jnp.where(i < n_prompt_tiles, tr, hs_ref.shape[0])

    @pl.when(i < n_prompt_tiles)
    def _():
        _store_rows_as_tiles(xn_tiles, _rmsnorm(hp_ref[...], g_ref[...]))

    @pl.when(i >= n_prompt_tiles)
    def _():
        _store_rows_as_tiles(xn_tiles, _rmsnorm(hs_ref[...], g_ref[...]))

    @pl.when(i >= 2)
    def _():
        _wait_rows(n_rows, sorted_buf.at[slot], xs_hbm, sem.at[slot])

    def place(group, carry):
        for u in range(LOOP_UNROLL):
            t = group * LOOP_UNROLL + u
            row = _row_tile(xn_tiles, t)[...]
            for k in range(TOP_K):
                _row_tile_at(sorted_buf.at[slot], lpos_ref[0, 0, k * tr + t])[...] = row
        return carry
    lax.fori_loop(0, n_valid // LOOP_UNROLL, place, 0)

    _start_runs(tab_ref, sorted_buf.at[slot], xs_hbm, sem.at[slot], src_is_sorted_tile=True)

    @pl.when(i == last)
    def _():
        _wait_rows(n_rows, sorted_buf.at[1 - slot], xs_hbm, sem.at[1 - slot])
        _wait_rows(TOP_K * hs_ref.shape[0], sorted_buf.at[slot], xs_hbm, sem.at[slot])
        for present, copy in _zero_fill_copies(fs_ref, fl_ref, zbuf, xs_hbm, zsem):
            pl.when(present)(copy.wait)


def _dispatch(fill_start, fill_len, lpos_tiles, tab_tiles, h1p, h1s, g, n_rows):
    n_prompt_tiles = h1p.shape[0] // TOK_TILE
    n_tiles = lpos_tiles.shape[0]
    assert n_tiles >= 2
    smem_blk = lambda a: pl.BlockSpec((1, 1, a.shape[2]), lambda i, fs, fl: (i, 0, 0), memory_space=pltpu.SMEM)
    const = lambda a: pl.BlockSpec(a.shape, lambda i, fs, fl: (0,) * a.ndim)
    grid_spec = pltpu.PrefetchScalarGridSpec(
        num_scalar_prefetch=2,
        grid=(n_tiles,),
        in_specs=[smem_blk(lpos_tiles), smem_blk(tab_tiles),
                  pl.BlockSpec((TOK_TILE, D_MODEL), lambda i, fs, fl: (jnp.minimum(i, n_prompt_tiles - 1), 0)),
                  const(h1s), const(g)],
        out_specs=pl.BlockSpec(memory_space=pl.ANY),
        scratch_shapes=[pltpu.VMEM((TOK_TILE * ROW_SUB, LANES), f32),
                        pltpu.VMEM((2, TOP_K * TOK_TILE * ROW_SUB, LANES), f32),
                        pltpu.VMEM((FILL_CHUNKS[0] * ROW_SUB, LANES), f32),
                        pltpu.SemaphoreType.DMA((2,)), pltpu.SemaphoreType.DMA(())],
    )
    return pl.pallas_call(
        functools.partial(_dispatch_kernel, n_prompt_tiles=n_prompt_tiles),
        grid_spec=grid_spec,
        out_shape=jax.ShapeDtypeStruct((n_rows * ROW_SUB, LANES), f32),
        compiler_params=pltpu.CompilerParams(
            dimension_semantics=("arbitrary",), vmem_limit_bytes=VMEM_LIMIT),
        name="dispatch",
    )(fill_start, fill_len, lpos_tiles, tab_tiles, h1p, h1s, g)


def _expert_kernel(be_ref, nu_ref, xs_ref, wgu_ref, bgu_ref, wd_ref, bd_ref, ys_ref, wgu_bf, wd_bf):
    i = pl.program_id(0)
    n_used = nu_ref[0]
    new_expert = jnp.logical_or(i == 0, be_ref[i] != be_ref[jnp.maximum(i - 1, 0)])

    @pl.when(jnp.logical_and(i < n_used, new_expert))
    def _():
        wgu_bf[...] = wgu_ref[0].astype(bf16)
        wd_bf[...] = wd_ref[0].astype(bf16)

    @pl.when(i < n_used)
    def _():
        x = _load_tiles_as_rows(xs_ref, 0, EXP_ROWS).astype(bf16)
        gu = jnp.dot(x, wgu_bf[...], preferred_element_type=f32) + bgu_ref[0]
        gate = jnp.minimum(gu[:, :D_FF], SWIGLU_LIMIT)
        up = jnp.clip(gu[:, D_FF:], -SWIGLU_LIMIT, SWIGLU_LIMIT)
        glu = gate * jax.nn.sigmoid(gate * SWIGLU_ALPHA)
        act = ((up + 1.0) * glu).astype(bf16)
        _store_rows_as_tiles(ys_ref, jnp.dot(act, wd_bf[...], preferred_element_type=f32) + bd_ref[0])

    @pl.when(i >= n_used)
    def _():
        ys_ref[...] = jnp.zeros(ys_ref.shape, f32)


def _experts(block_e, n_used, xs, w_gu, b_gu, w_down, b_down):
    n_blocks = xs.shape[0] // (EXP_ROWS * ROW_SUB)
    by_expert = lambda a: pl.BlockSpec((1,) + a.shape[1:], lambda i, be, nu: (be[i],) + (0,) * (a.ndim - 1))
    grid_spec = pltpu.PrefetchScalarGridSpec(
        num_scalar_prefetch=2,
        grid=(n_blocks,),
        in_specs=[pl.BlockSpec((EXP_ROWS * ROW_SUB, LANES),
                               lambda i, be, nu: (jnp.minimum(i, jnp.maximum(nu[0] - 1, 0)), 0)),
                  by_expert(w_gu), by_expert(b_gu), by_expert(w_down), by_expert(b_down)],
        out_specs=pl.BlockSpec((EXP_ROWS * ROW_SUB, LANES), lambda i, be, nu: (i, 0)),
        scratch_shapes=[pltpu.VMEM(w_gu.shape[1:], bf16), pltpu.VMEM(w_down.shape[1:], bf16)],
    )
    return pl.pallas_call(
        _expert_kernel,
        grid_spec=grid_spec,
        out_shape=jax.ShapeDtypeStruct(xs.shape, f32),
        compiler_params=pltpu.CompilerParams(
            dimension_semantics=("arbitrary",), vmem_limit_bytes=BIG_VMEM_LIMIT),
        name="experts",
    )(block_e, n_used, xs, w_gu, b_gu, w_down, b_down)


def _combine_kernel(lpos_ref, gate_ref, tab_cur, tab_nxt, hp_ref, hs_ref, ys_hbm, gf_ref,
                    yp_ref, ysm_ref, ybuf, moe_tiles, sem, *, n_prompt_tiles):
    i = pl.program_id(0)
    tr = hp_ref.shape[0]
    n_rows = TOP_K * tr
    slot = i % 2

    @pl.when(i == 0)
    def _():
        _start_runs(tab_cur, ys_hbm, ybuf.at[0], sem.at[0], src_is_sorted_tile=False)

    @pl.when(i + 1 < pl.num_programs(0))
    def _():
        _start_runs(tab_nxt, ys_hbm, ybuf.at[1 - slot], sem.at[1 - slot], src_is_sorted_tile=False)

    n_sample = hs_ref.shape[0]
    n_valid = jnp.where(i < n_prompt_tiles, tr, n_sample)

    @pl.when(i < n_prompt_tiles)
    def _():
        _wait_rows(n_rows, ys_hbm, ybuf.at[slot], sem.at[slot])

    @pl.when(i >= n_prompt_tiles)
    def _():
        _wait_rows(TOP_K * n_sample, ys_hbm, ybuf.at[slot], sem.at[slot])

    def mix(group, carry):
        for u in range(LOOP_UNROLL):
            t = group * LOOP_UNROLL + u
            acc = gate_ref[0, 0, t] * _row_tile_at(ybuf.at[slot], lpos_ref[0, 0, t])[...]
            for k in range(1, TOP_K):
                acc = acc + (gate_ref[0, 0, k * tr + t]
                             * _row_tile_at(ybuf.at[slot], lpos_ref[0, 0, k * tr + t])[...])
            _row_tile(moe_tiles, t)[...] = acc
        return carry
    lax.fori_loop(0, n_valid // LOOP_UNROLL, mix, 0)

    @pl.when(i < n_prompt_tiles)
    def _():
        yp_ref[...] = _rmsnorm(hp_ref[...] + _load_tiles_as_rows(moe_tiles, 0, tr), gf_ref[...])

    @pl.when(i >= n_prompt_tiles)
    def _():
        ysm_ref[...] = _rmsnorm(hs_ref[...] + _load_tiles_as_rows(moe_tiles, 0, n_sample), gf_ref[...])


def _combine(lpos_tiles, gate_tiles, tab_tiles, h1p, h1s, ys, g_final):
    n_prompt_tiles = h1p.shape[0] // TOK_TILE
    n_tiles = lpos_tiles.shape[0]
    const = lambda a: pl.BlockSpec(a.shape, lambda i: (0,) * a.ndim)
    prompt_blk = pl.BlockSpec((TOK_TILE, D_MODEL), lambda i: (jnp.minimum(i, n_prompt_tiles - 1), 0))
    smem_blk = lambda a, ahead: pl.BlockSpec(
        (1, 1, a.shape[2]), lambda i: (jnp.minimum(i + ahead, n_tiles - 1), 0, 0), memory_space=pltpu.SMEM)
    return pl.pallas_call(
        functools.partial(_combine_kernel, n_prompt_tiles=n_prompt_tiles),
        grid=(n_tiles,),
        in_specs=[smem_blk(lpos_tiles, 0), smem_blk(gate_tiles, 0), smem_blk(tab_tiles, 0), smem_blk(tab_tiles, 1),
                  prompt_blk, const(h1s),
                  pl.BlockSpec(memory_space=pl.ANY),
                  const(g_final)],
        out_specs=[prompt_blk, const(h1s)],
        out_shape=[jax.ShapeDtypeStruct(h1p.shape, f32), jax.ShapeDtypeStruct(h1s.shape, f32)],
        scratch_shapes=[pltpu.VMEM((2, TOP_K * TOK_TILE * ROW_SUB, LANES), f32),
                        pltpu.VMEM((TOK_TILE * ROW_SUB, LANES), f32),
                        pltpu.SemaphoreType.DMA((2,))],
        compiler_params=pltpu.CompilerParams(
            dimension_semantics=("arbitrary",), vmem_limit_bytes=BIG_VMEM_LIMIT),
        name="combine",
    )(lpos_tiles, gate_tiles, tab_tiles, tab_tiles, h1p, h1s, ys, g_final)


def _block_diag_gate_weights(wa, wx):
    heads_per_half = N_LRU_HEADS // 2
    eye = jnp.eye(heads_per_half, dtype=wa.dtype)

    def bd(w):
        return jnp.einsum("hij,hg->higj", w, eye).reshape(heads_per_half * LRU_HEAD_DIM,
                                                          heads_per_half * LRU_HEAD_DIM)

    halves = [jnp.concatenate([bd(wa[s * heads_per_half:(s + 1) * heads_per_half]),
                               bd(wx[s * heads_per_half:(s + 1) * heads_per_half])], axis=1)
              for s in range(2)]
    return jnp.stack(halves).astype(bf16)


def kernel(x_prompt, x_sample, state_conv_short, state_conv_lru, state_lru_h, meta_tokens,
           norm_mix_g, w_in, conv_short_w, conv_lru_w, conv_lru_b, w_rg_a, b_rg_a, w_rg_x, b_rg_x,
           rg_lambda, w_out, norm_ffn_g, w_router, b_router, w_gate_up, b_gate_up, w_down, b_down,
           final_norm_g):
    assert norm_mix_g.shape[0] == 1, "single layer"
    bp, seq, _ = x_prompt.shape
    bs, seq_s, _ = x_sample.shape
    n_meta = meta_tokens.shape[0]
    row = lambda a: a.reshape(1, -1).astype(f32)

    mw = dict(g=row(norm_mix_g[0]), w_in=w_in[0].astype(bf16), csw=conv_short_w[0], clw=conv_lru_w[0],
              clb=row(conv_lru_b[0]), wg=_block_diag_gate_weights(w_rg_a[0], w_rg_x[0]),
              ba=row(b_rg_a[0]), bx=row(b_rg_x[0]), lam=row(rg_lambda[0]), w_out=w_out[0].astype(bf16))

    zeros = lambda *s: jnp.zeros(s, f32)
    _, cs_m, cl_m, h_m = _mixer(meta_tokens[None].astype(f32), zeros(1, 2, W_SHORT), zeros(1, 3, W_LRU),
                                zeros(1, 1, W_LRU), mw, n_meta, 1)
    rep = lambda a: jnp.broadcast_to(a, (bp,) + a.shape[1:])
    h1p, cs_p, cl_p, h_p = _mixer(x_prompt, rep(cs_m), rep(cl_m), rep(h_m), mw, MIX_ROWS, MIX_STREAMS)
    h1s, cs_s, cl_s, h_s = _mixer(x_sample, state_conv_short[0], state_conv_lru[0],
                                  state_lru_h[0][:, None, :], mw, seq_s, MIX_STREAMS)
    h1p = h1p.reshape(bp * seq, D_MODEL)
    h1s = h1s.reshape(bs * seq_s, D_MODEL)
    n_tok = h1p.shape[0] + h1s.shape[0]

    g_ffn = row(norm_ffn_g[0])
    lpos, gates, tab, counts = _router(
        h1p, h1s, g_ffn, w_router[0].T.astype(f32), b_router[0].reshape(-1, 1).astype(f32))
    n_tiles = tab.shape[0]

    counts = counts[:, 0]
    padded = (counts + EXP_ROWS - 1) // EXP_ROWS * EXP_ROWS
    pad_end = jnp.cumsum(padded)
    pad_start = pad_end - padded
    n_blocks = -(-(n_tok * TOP_K + N_EXPERTS * (EXP_ROWS - 1)) // EXP_ROWS)
    block_first_row = jnp.arange(n_blocks, dtype=jnp.int32) * EXP_ROWS
    block_e = jnp.minimum(jnp.sum(pad_end[None, :] <= block_first_row[:, None], axis=1),
                          N_EXPERTS - 1).astype(jnp.int32)
    n_used = (pad_end[-1:] // EXP_ROWS).astype(jnp.int32)
    n_rows = n_blocks * EXP_ROWS
    fill_start = jnp.concatenate([pad_start + counts, pad_end[-1:]]).astype(jnp.int32)
    fill_len = jnp.concatenate([padded - counts, n_rows - pad_end[-1:]]).astype(jnp.int32)

    tab_tiles = jnp.concatenate([pad_start[None, :] + tab[:, :, 0], tab[:, :, 1], tab[:, :, 2],
                                 jnp.zeros((n_tiles, LANES - 3 * N_EXPERTS), jnp.int32)],
                                axis=1).reshape(n_tiles, 1, LANES)
    by_tile = lambda a: a.reshape(TOP_K, n_tiles, TOK_TILE).transpose(1, 0, 2).reshape(
        n_tiles, 1, TOP_K * TOK_TILE)

    xs = _dispatch(fill_start, fill_len, by_tile(lpos), tab_tiles, h1p, h1s, g_ffn, n_rows)
    ys = _experts(block_e, n_used, xs,
                  w_gate_up[0].astype(f32), b_gate_up[0][:, None, :].astype(f32),
                  w_down[0].astype(f32), b_down[0][:, None, :].astype(f32))
    yp, ysm = _combine(by_tile(lpos), by_tile(gates), tab_tiles, h1p, h1s, ys, row(final_norm_g))

    st = lambda a: a[None]
    return (yp.reshape(bp, seq, D_MODEL), ysm.reshape(bs, seq_s, D_MODEL),
            st(cs_p), st(cl_p), st(h_p[:, 0, :]), st(cs_s), st(cl_s), st(h_s[:, 0, :]))
```

```python
import functools

import jax
import jax.numpy as jnp
from jax import lax
from jax.experimental import pallas as pl
from jax.experimental.pallas import tpu as pltpu

D_MODEL = 1024
W_SHORT = 512
W_LRU = 512
N_LRU_HEADS = 8
LRU_HEAD_DIM = W_LRU // N_LRU_HEADS
N_EXPERTS = 32
TOP_K = 4
D_FF = 1024
RG_C = 8.0
SWIGLU_LIMIT = 7.0
SWIGLU_ALPHA = 1.702
EPS = 1e-6

SUBLANES = 8
MIX_ROWS = 256
MIX_STREAMS = 4
TOK_TILE = 1024
LOOP_UNROLL = 32
EXP_ROWS = 768
V7X_VMEM_BYTES = 64 * 1024 * 1024
VMEM_LIMIT = V7X_VMEM_BYTES * 7 // 8
BIG_VMEM_LIMIT = V7X_VMEM_BYTES * 15 // 16

f32 = jnp.float32
bf16 = jnp.bfloat16


def _rmsnorm(x, g):
    return (x * lax.rsqrt(jnp.mean(x * x, axis=-1, keepdims=True) + EPS)) * g


LANES = 128
ROW_SUB = D_MODEL // LANES


def _row_tile(ref, r):
    return ref.at[pl.ds(pl.multiple_of(r * ROW_SUB, ROW_SUB), ROW_SUB)]


def _row_tile_at(ref, first_sub_row):
    return ref.at[pl.ds(pl.multiple_of(first_sub_row, ROW_SUB), ROW_SUB)]


def _store_rows_as_tiles(ref, x, first=0):
    for j in range(ROW_SUB):
        ref[pl.ds(first * ROW_SUB + j, x.shape[0], stride=ROW_SUB), :] = x[:, j * LANES:(j + 1) * LANES]


def _load_tiles_as_rows(ref, first, n):
    return jnp.concatenate([ref[pl.ds(first * ROW_SUB + j, n, stride=ROW_SUB), :] for j in range(ROW_SUB)],
                           axis=1)


def _mixer_kernel(x_ref, cs0_ref, cl0_ref, h0_ref, g_ref, win_ref, csw_ref, clw_ref, clb_ref,
                  wg_ref, ba_ref, bx_ref, lam_ref, wout_ref,
                  h1_ref, ncs_ref, ncl_ref, nh_ref,
                  cvtails, lxtails, hcars, *, tl):
    j = pl.program_id(1)
    n_streams = x_ref.shape[0]
    n_cs, n_cl = cs0_ref.shape[1], cl0_ref.shape[1]

    @pl.when(j == 0)
    def _():
        for r in range(n_streams):
            cvtails[r] = jnp.zeros((SUBLANES, W_SHORT), f32)
            cvtails[r, SUBLANES - n_cs:, :] = cs0_ref[r]
            lxtails[r] = jnp.zeros((SUBLANES, W_LRU), f32)
            lxtails[r, SUBLANES - n_cl:, :] = cl0_ref[r]
            hcars[r] = h0_ref[r]

    ends = _staggered([_mixer_chunk(x_ref.at[r], g_ref, win_ref, csw_ref, clw_ref, clb_ref, wg_ref, ba_ref,
                                    bx_ref, lam_ref, wout_ref, h1_ref.at[r], cvtails.at[r], lxtails.at[r],
                                    hcars.at[r], tl=tl) for r in range(n_streams)])

    @pl.when(j == pl.num_programs(1) - 1)
    def _():
        for r, (cv_end, lx_end, h_end) in enumerate(ends):
            ncs_ref[r] = cv_end[SUBLANES - n_cs:, :]
            ncl_ref[r] = lx_end[SUBLANES - n_cl:, :]
            nh_ref[r] = h_end


def _staggered(chains):
    results = {}
    step = 0
    while len(results) < len(chains):
        for r, chain in enumerate(chains):
            if step >= r and r not in results:
                try:
                    next(chain)
                except StopIteration as done:
                    results[r] = done.value
        step += 1
    return [results[r] for r in range(len(chains))]


def _delayed(x3, tail, s, sub):
    rolled = pltpu.roll(x3, s, 1)
    before = jnp.concatenate([pltpu.roll(tail, s, 0)[None], rolled[:-1]], axis=0)
    return jnp.where(sub < s, before, rolled)


def _mixer_chunk(x_ref, g_ref, win_ref, csw_ref, clw_ref, clb_ref, wg_ref, ba_ref, bx_ref, lam_ref, wout_ref,
                 h1_ref, cvtail, lxtail, hcar, *, tl):
    n_groups = tl // SUBLANES
    grouped = lambda a: a.reshape(n_groups, SUBLANES, a.shape[-1])
    sub = lax.broadcasted_iota(jnp.int32, (n_groups, SUBLANES, W_LRU), 1)
    x = x_ref[...]
    u = _rmsnorm(x, g_ref[...])
    proj = jnp.dot(u.astype(bf16), win_ref[...], preferred_element_type=f32)
    b_g = proj[:, 0:W_SHORT]
    c_g = proj[:, W_SHORT:2 * W_SHORT]
    v = proj[:, 2 * W_SHORT:3 * W_SHORT]
    lx = proj[:, 3 * W_SHORT:3 * W_SHORT + W_LRU]
    ly = proj[:, 3 * W_SHORT + W_LRU:]
    yield

    cv = grouped(c_g * v)
    cv_tail = cvtail[...]
    conv_a = (_delayed(cv, cv_tail, 2, sub) * csw_ref[0:1, :]
              + _delayed(cv, cv_tail, 1, sub) * csw_ref[1:2, :]
              + cv * csw_ref[2:3, :])
    out_a = b_g * conv_a.reshape(tl, W_SHORT)
    cv_end = cv[n_groups - 1]
    cvtail[...] = cv_end

    lx = grouped(lx)
    lx_tail = lxtail[...]
    xc = ((_delayed(lx, lx_tail, 3, sub) * clw_ref[0:1, :]
           + _delayed(lx, lx_tail, 2, sub) * clw_ref[1:2, :]
           + _delayed(lx, lx_tail, 1, sub) * clw_ref[2:3, :]
           + lx * clw_ref[3:4, :]) + clb_ref[...]).reshape(tl, W_LRU)
    lx_end = lx[n_groups - 1]
    lxtail[...] = lx_end
    xcb = xc.astype(bf16)
    half = W_LRU // 2
    g0 = jnp.dot(xcb[:, :half], wg_ref[0], preferred_element_type=f32)
    g1 = jnp.dot(xcb[:, half:], wg_ref[1], preferred_element_type=f32)
    ga = jnp.concatenate([g0[:, :half], g1[:, :half]], axis=1) + ba_ref[...]
    gx = jnp.concatenate([g0[:, half:], g1[:, half:]], axis=1) + bx_ref[...]
    r = jax.nn.sigmoid(ga)
    ig = jax.nn.sigmoid(gx)
    z = -lam_ref[...]
    softplus = jnp.maximum(z, 0.0) + jnp.log1p(jnp.exp(-jnp.abs(z)))
    log_a = (-RG_C * r) * softplus
    a = jnp.exp(log_a)
    th = jnp.tanh(log_a)
    uu = jnp.sqrt((-2.0 * th) / (1.0 - th)) * (ig * xc)
    yield

    aa = grouped(a)
    hh = grouped(uu)
    s = 1
    while s < SUBLANES:
        h_sh = jnp.where(sub >= s, pltpu.roll(hh, s, 1), 0.0)
        a_sh = jnp.where(sub >= s, pltpu.roll(aa, s, 1), 1.0)
        hh = aa * h_sh + hh
        aa = aa * a_sh
        s *= 2
    a_last = jnp.broadcast_to(aa[:, SUBLANES - 1:, :], aa.shape)
    h_last = jnp.broadcast_to(hh[:, SUBLANES - 1:, :], hh.shape)
    carry = jnp.broadcast_to(hcar[...], (SUBLANES, W_LRU))
    groups = []
    for gi in range(n_groups):
        groups.append(aa[gi] * carry + hh[gi])
        carry = a_last[gi] * carry + h_last[gi]
    hh = jnp.concatenate(groups, axis=0)
    hcar[...] = carry[0:1, :]
    yield

    gelu = ly *(0.5 * (1.0 + jnp.tanh(0.7978845608028654 * (ly + 0.044715 * (ly * ly * ly)))))
    out_b = hh * gelu
    y = jnp.concatenate([out_a, out_b], axis=1).astype(bf16)
    h1_ref[...] = x + jnp.dot(y, wout_ref[...], preferred_element_type=f32)
    return cv_end, lx_end, hh[tl - 1:tl, :]


def _mixer(x, cs0, cl0, h0, mw, tl, streams):
    nb, seq, _ = x.shape
    assert seq % tl == 0 and tl % SUBLANES == 0 and nb % streams == 0
    full = lambda a: pl.BlockSpec(a.shape, lambda b, j: (0,) * a.ndim)
    per_b = lambda a: pl.BlockSpec((streams,) + a.shape[1:], lambda b, j: (b,) + (0,) * (a.ndim - 1))
    weights = (mw["g"], mw["w_in"], mw["csw"], mw["clw"], mw["clb"], mw["wg"], mw["ba"], mw["bx"],
               mw["lam"], mw["w_out"])
    return pl.pallas_call(
        functools.partial(_mixer_kernel, tl=tl),
        grid=(nb // streams, seq // tl),
        in_specs=[pl.BlockSpec((streams, tl, D_MODEL), lambda b, j: (b, j, 0)),
                  per_b(cs0), per_b(cl0), per_b(h0)] + [full(w) for w in weights],
        out_specs=[pl.BlockSpec((streams, tl, D_MODEL), lambda b, j: (b, j, 0)),
                   per_b(cs0), per_b(cl0), per_b(h0)],
        out_shape=[jax.ShapeDtypeStruct(x.shape, f32),
                   jax.ShapeDtypeStruct(cs0.shape, f32),
                   jax.ShapeDtypeStruct(cl0.shape, f32),
                   jax.ShapeDtypeStruct(h0.shape, f32)],
        scratch_shapes=[pltpu.VMEM((streams, SUBLANES, W_SHORT), f32),
                        pltpu.VMEM((streams, SUBLANES, W_LRU), f32),
                        pltpu.VMEM((streams, 1, W_LRU), f32)],
        compiler_params=pltpu.CompilerParams(
            dimension_semantics=("arbitrary", "arbitrary"), vmem_limit_bytes=VMEM_LIMIT),
        name=f"mixer_{seq}",
    )(x, cs0, cl0, h0, *weights)


def _route_part(h, n_routed, g, w_router_t, b_router_col):
    tr = h.shape[0]
    xn = _rmsnorm(h, g)
    logits = lax.dot_general(w_router_t.astype(bf16), xn.astype(bf16), (((1,), (1,)), ((), ())),
                             preferred_element_type=f32) + b_router_col
    yield

    rows = lax.broadcasted_iota(jnp.int32, (N_EXPERTS, tr), 0)
    vals, idxs = [], []
    cur = logits
    for _ in range(TOP_K):
        m = jnp.max(cur, axis=0, keepdims=True)
        ik = jnp.min(jnp.where(cur == m, rows, N_EXPERTS), axis=0, keepdims=True)
        vals.append(m)
        idxs.append(ik)
        cur = jnp.where(rows == ik, -jnp.inf, cur)
    ex = [jnp.exp(v - vals[0]) for v in vals]
    denom = ex[0] + ex[1] + ex[2] + ex[3]
    yield

    routed = lax.broadcasted_iota(jnp.int32, (N_EXPERTS, tr), 1) < n_routed
    onehot = jnp.zeros((N_EXPERTS, tr), f32)
    for ik in idxs:
        onehot = onehot + jnp.where(jnp.logical_and(rows == ik, routed), 1.0, 0.0)
    src = lax.broadcasted_iota(jnp.int32, (tr, tr), 0)
    dst = lax.broadcasted_iota(jnp.int32, (tr, tr), 1)
    tri = jnp.where(src < dst, 1.0, 0.0).astype(bf16)
    before = jnp.dot(onehot.astype(bf16), tri, preferred_element_type=f32)
    cnt = jnp.broadcast_to(jnp.sum(onehot, axis=1, keepdims=True), (N_EXPERTS, LANES))
    return idxs, [e / denom for e in ex], before, cnt


def _router_kernel(hp_ref, hs_ref, g_ref, wrt_ref, br_ref,
                   lpos_ref, gate_ref, tab_ref, cnt_ref, carry, *, n_prompt_tiles):
    i = pl.program_id(0)
    part = hs_ref.shape[0]
    n_parts = TOK_TILE // part

    @pl.when(i == 0)
    def _():
        carry[...] = jnp.zeros(carry.shape, f32)

    on_prompt = i < n_prompt_tiles
    n_valid = jnp.where(on_prompt, TOK_TILE, part)
    tokens = ([jnp.where(on_prompt, hp_ref[0:part, :], hs_ref[...])]
              + [hp_ref[p * part:(p + 1) * part, :] for p in range(1, n_parts)])
    routed = _staggered([_route_part(h, jnp.clip(n_valid - p * part, 0, part), g_ref[...], wrt_ref[...],
                                     br_ref[...]) for p, h in enumerate(tokens)])

    tile_cnt = sum(cnt for _, _, _, cnt in routed)
    e_row = lax.broadcasted_iota(jnp.int32, (N_EXPERTS, LANES), 0)
    upto = tile_cnt
    s = 1
    while s < N_EXPERTS:
        upto = upto + jnp.where(e_row >= s, pltpu.roll(upto, s, 0), 0.0)
        s *= 2
    first = upto - tile_cnt

    rows = lax.broadcasted_iota(jnp.int32, (N_EXPERTS, part), 0)
    part_first = first
    for p, (idxs, gates, before, cnt) in enumerate(routed):
        within = before + part_first[:, 0:1]
        for k in range(TOP_K):
            lpos_ref[k:k + 1, p * part:(p + 1) * part] = ROW_SUB * jnp.sum(
                jnp.where(rows == idxs[k], within, 0.0), axis=0, keepdims=True).astype(jnp.int32)
            gate_ref[k:k + 1, p * part:(p + 1) * part] = gates[k]
        part_first = part_first + cnt
    lane = lax.broadcasted_iota(jnp.int32, carry.shape, 1)
    tab_ref[0] = jnp.where(lane == 0, carry[...], jnp.where(lane == 1, tile_cnt, first)).astype(jnp.int32)
    carry[...] = carry[...] + tile_cnt
    cnt_ref[...] = carry[...].astype(jnp.int32)


def _router(h1p, h1s, g, w_router_t, b_router_col):
    assert h1p.shape[0] % TOK_TILE == 0 and TOK_TILE % h1s.shape[0] == 0 and h1s.shape[0] % LOOP_UNROLL == 0
    n_prompt_tiles = h1p.shape[0] // TOK_TILE
    n_steps = n_prompt_tiles + 1
    const = lambda a: pl.BlockSpec(a.shape, lambda s: (0,) * a.ndim)
    lane_blk = pl.BlockSpec((TOP_K, TOK_TILE), lambda s: (0, s))
    return pl.pallas_call(
        functools.partial(_router_kernel, n_prompt_tiles=n_prompt_tiles),
        grid=(n_steps,),
        in_specs=[pl.BlockSpec((TOK_TILE, D_MODEL), lambda s: (jnp.minimum(s, n_prompt_tiles - 1), 0)),
                  const(h1s), const(g), const(w_router_t), const(b_router_col)],
        out_specs=[lane_blk, lane_blk,
                   pl.BlockSpec((1, N_EXPERTS, LANES), lambda s: (s, 0, 0)),
                   pl.BlockSpec((N_EXPERTS, LANES), lambda s: (0, 0))],
        out_shape=[jax.ShapeDtypeStruct((TOP_K, n_steps * TOK_TILE), jnp.int32),
                   jax.ShapeDtypeStruct((TOP_K, n_steps * TOK_TILE), f32),
                   jax.ShapeDtypeStruct((n_steps, N_EXPERTS, LANES), jnp.int32),
                   jax.ShapeDtypeStruct((N_EXPERTS, LANES), jnp.int32)],
        scratch_shapes=[pltpu.VMEM((N_EXPERTS, LANES), f32)],
        compiler_params=pltpu.CompilerParams(
            dimension_semantics=("arbitrary",), vmem_limit_bytes=VMEM_LIMIT),
        name="router",
    )(h1p, h1s, g, w_router_t, b_router_col)


def _wait_rows(n, src_ref, dst_ref, sem):
    pltpu.make_async_copy(src_ref.at[pl.ds(0, n * ROW_SUB)], dst_ref.at[pl.ds(0, n * ROW_SUB)], sem).wait()


def _rows(ref, first, n):
    return ref.at[pl.ds(pl.multiple_of(first * ROW_SUB, ROW_SUB), n * ROW_SUB)]


FILL_CHUNKS = tuple(c for c in (1 << b for b in reversed(range(EXP_ROWS.bit_length()))) if c < EXP_ROWS)
TAIL_CHUNK = min(EXP_ROWS & -EXP_ROWS, FILL_CHUNKS[0])
RUN_CHUNKS = tuple(1 << b for b in reversed(range(TOK_TILE.bit_length())))
RUN_LARGE = 2 * TOK_TILE * TOP_K // N_EXPERTS


def _start_runs(tab_ref, src_ref, dst_ref, sem, *, src_is_sorted_tile):
    for e in range(N_EXPERTS):
        hbm_first = tab_ref[0, 0, e]
        length = tab_ref[0, 0, N_EXPERTS + e]
        tile_first = tab_ref[0, 0, 2 * N_EXPERTS + e]
        src_first, dst_first = (tile_first, hbm_first) if src_is_sorted_tile else (hbm_first, tile_first)

        def start_pieces(chunks):
            for chunk in chunks:
                offset = length & ~(2 * chunk - 1)
                copy = pltpu.make_async_copy(_rows(src_ref, src_first + offset, chunk),
                                             _rows(dst_ref, dst_first + offset, chunk), sem)
                pl.when((length & chunk) != 0)(copy.start)

        large = tuple(c for c in RUN_CHUNKS if c >= RUN_LARGE)
        pl.when(length >= RUN_LARGE)(functools.partial(start_pieces, large))
        start_pieces(tuple(c for c in RUN_CHUNKS if c < RUN_LARGE))


def _zero_fill_copies(fs_ref, fl_ref, zbuf, xs_hbm, sem):
    out = []
    for e in range(N_EXPERTS):
        start, length = fs_ref[e], fl_ref[e]
        for chunk in FILL_CHUNKS:
            offset = length & ~(2 * chunk - 1)
            first = pl.multiple_of((start + offset) * ROW_SUB, ROW_SUB)
            copy = pltpu.make_async_copy(zbuf.at[pl.ds(0, chunk * ROW_SUB)],
                                         xs_hbm.at[pl.ds(first, chunk * ROW_SUB)], sem)
            out.append(((length & chunk) != 0, copy))
    start, length = fs_ref[N_EXPERTS], fl_ref[N_EXPERTS]
    chunk = TAIL_CHUNK
    for c in range(N_EXPERTS * EXP_ROWS // chunk):
        first = pl.multiple_of((start + c * chunk) * ROW_SUB, ROW_SUB)
        copy = pltpu.make_async_copy(zbuf.at[pl.ds(0, chunk * ROW_SUB)],
                                     xs_hbm.at[pl.ds(first, chunk * ROW_SUB)], sem)
        out.append((c * chunk < length, copy))
    return out


def _dispatch_kernel(fs_ref, fl_ref, lpos_ref, tab_ref, hp_ref, hs_ref, g_ref, xs_hbm,
                     xn_tiles, sorted_buf, zbuf, sem, zsem, *, n_prompt_tiles):
    i = pl.program_id(0)
    last = pl.num_programs(0) - 1
    tr = hp_ref.shape[0]
    n_rows = TOP_K * tr
    slot = i % 2

    @pl.when(i == 0)
    def _():
        zbuf[...] = jnp.zeros(zbuf.shape, f32)
        for present, copy in _zero_fill_copies(fs_ref, fl_ref, zbuf, xs_hbm, zsem):
            pl.when(present)(copy.start)

    n_valid = jnp.where(i < n_prompt_tiles, tr, hs_ref.shape[0])

    @pl.when(i < n_prompt_tiles)
    def _():
        _store_rows_as_tiles(xn_tiles, _rmsnorm(hp_ref[...], g_ref[...]))

    @pl.when(i >= n_prompt_tiles)
    def _():
        _store_rows_as_tiles(xn_tiles, _rmsnorm(hs_ref[...], g_ref[...]))

    @pl.when(i >= 2)
    def _():
        _wait_rows(n_rows, sorted_buf.at[slot], xs_hbm, sem.at[slot])

    def place(group, carry):
        for u in range(LOOP_UNROLL):
            t = group * LOOP_UNROLL + u
            row = _row_tile(xn_tiles, t)[...]
            for k in range(TOP_K):
                _row_tile_at(sorted_buf.at[slot], lpos_ref[0, 0, k * tr + t])[...] = row
        return carry
    lax.fori_loop(0, n_valid // LOOP_UNROLL, place, 0)

    _start_runs(tab_ref, sorted_buf.at[slot], xs_hbm, sem.at[slot], src_is_sorted_tile=True)

    @pl.when(i == last)
    def _():
        _wait_rows(n_rows, sorted_buf.at[1 - slot], xs_hbm, sem.at[1 - slot])
        _wait_rows(TOP_K * hs_ref.shape[0], sorted_buf.at[slot], xs_hbm, sem.at[slot])
        for present, copy in _zero_fill_copies(fs_ref, fl_ref, zbuf, xs_hbm, zsem):
            pl.when(present)(copy.wait)


def _dispatch(fill_start, fill_len, lpos_tiles, tab_tiles, h1p, h1s, g, n_rows):
    n_prompt_tiles = h1p.shape[0] // TOK_TILE
    n_tiles = lpos_tiles.shape[0]
    assert n_tiles >= 2
    smem_blk = lambda a: pl.BlockSpec((1, 1, a.shape[2]), lambda i, fs, fl: (i, 0, 0), memory_space=pltpu.SMEM)
    const = lambda a: pl.BlockSpec(a.shape, lambda i, fs, fl: (0,) * a.ndim)
    grid_spec = pltpu.PrefetchScalarGridSpec(
        num_scalar_prefetch=2,
        grid=(n_tiles,),
        in_specs=[smem_blk(lpos_tiles), smem_blk(tab_tiles),
                  pl.BlockSpec((TOK_TILE, D_MODEL), lambda i, fs, fl: (jnp.minimum(i, n_prompt_tiles - 1), 0)),
                  const(h1s), const(g)],
        out_specs=pl.BlockSpec(memory_space=pl.ANY),
        scratch_shapes=[pltpu.VMEM((TOK_TILE * ROW_SUB, LANES), f32),
                        pltpu.VMEM((2, TOP_K * TOK_TILE * ROW_SUB, LANES), f32),
                        pltpu.VMEM((FILL_CHUNKS[0] * ROW_SUB, LANES), f32),
                        pltpu.SemaphoreType.DMA((2,)), pltpu.SemaphoreType.DMA(())],
    )
    return pl.pallas_call(
        functools.partial(_dispatch_kernel, n_prompt_tiles=n_prompt_tiles),
        grid_spec=grid_spec,
        out_shape=jax.ShapeDtypeStruct((n_rows * ROW_SUB, LANES), f32),
        compiler_params=pltpu.CompilerParams(
            dimension_semantics=("arbitrary",), vmem_limit_bytes=VMEM_LIMIT),
        name="dispatch",
    )(fill_start, fill_len, lpos_tiles, tab_tiles, h1p, h1s, g)


def _expert_kernel(be_ref, nu_ref, xs_ref, wgu_ref, bgu_ref, wd_ref, bd_ref, ys_ref, wgu_bf, wd_bf):
    i = pl.program_id(0)
    n_used = nu_ref[0]
    new_expert = jnp.logical_or(i == 0, be_ref[i] != be_ref[jnp.maximum(i - 1, 0)])

    @pl.when(jnp.logical_and(i < n_used, new_expert))
    def _():
        wgu_bf[...] = wgu_ref[0].astype(bf16)
        wd_bf[...] = wd_ref[0].astype(bf16)

    @pl.when(i < n_used)
    def _():
        x = _load_tiles_as_rows(xs_ref, 0, EXP_ROWS).astype(bf16)
        gu = jnp.dot(x, wgu_bf[...], preferred_element_type=f32) + bgu_ref[0]
        gate = jnp.minimum(gu[:, :D_FF], SWIGLU_LIMIT)
        up = jnp.clip(gu[:, D_FF:], -SWIGLU_LIMIT, SWIGLU_LIMIT)
        glu = gate * jax.nn.sigmoid(gate * SWIGLU_ALPHA)
        act = ((up + 1.0) * glu).astype(bf16)
        _store_rows_as_tiles(ys_ref, jnp.dot(act, wd_bf[...], preferred_element_type=f32) + bd_ref[0])

    @pl.when(i >= n_used)
    def _():
        ys_ref[...] = jnp.zeros(ys_ref.shape, f32)


def _experts(block_e, n_used, xs, w_gu, b_gu, w_down, b_down):
    n_blocks = xs.shape[0] // (EXP_ROWS * ROW_SUB)
    by_expert = lambda a: pl.BlockSpec((1,) + a.shape[1:], lambda i, be, nu: (be[i],) + (0,) * (a.ndim - 1))
    grid_spec = pltpu.PrefetchScalarGridSpec(
        num_scalar_prefetch=2,
        grid=(n_blocks,),
        in_specs=[pl.BlockSpec((EXP_ROWS * ROW_SUB, LANES),
                               lambda i, be, nu: (jnp.minimum(i, jnp.maximum(nu[0] - 1, 0)), 0)),
                  by_expert(w_gu), by_expert(b_gu), by_expert(w_down), by_expert(b_down)],
        out_specs=pl.BlockSpec((EXP_ROWS * ROW_SUB, LANES), lambda i, be, nu: (i, 0)),
        scratch_shapes=[pltpu.VMEM(w_gu.shape[1:], bf16), pltpu.VMEM(w_down.shape[1:], bf16)],
    )
    return pl.pallas_call(
        _expert_kernel,
        grid_spec=grid_spec,
        out_shape=jax.ShapeDtypeStruct(xs.shape, f32),
        compiler_params=pltpu.CompilerParams(
            dimension_semantics=("arbitrary",), vmem_limit_bytes=BIG_VMEM_LIMIT),
        name="experts",
    )(block_e, n_used, xs, w_gu, b_gu, w_down, b_down)


def _combine_kernel(lpos_ref, gate_ref, tab_cur, tab_nxt, hp_ref, hs_ref, ys_hbm, gf_ref,
                    yp_ref, ysm_ref, ybuf, moe_tiles, sem, *, n_prompt_tiles):
    i = pl.program_id(0)
    tr = hp_ref.shape[0]
    n_rows = TOP_K * tr
    slot = i % 2

    @pl.when(i == 0)
    def _():
        _start_runs(tab_cur, ys_hbm, ybuf.at[0], sem.at[0], src_is_sorted_tile=False)

    @pl.when(i + 1 < pl.num_programs(0))
    def _():
        _start_runs(tab_nxt, ys_hbm, ybuf.at[1 - slot], sem.at[1 - slot], src_is_sorted_tile=False)

    n_sample = hs_ref.shape[0]
    n_valid = jnp.where(i < n_prompt_tiles, tr, n_sample)

    @pl.when(i < n_prompt_tiles)
    def _():
        _wait_rows(n_rows, ys_hbm, ybuf.at[slot], sem.at[slot])

    @pl.when(i >= n_prompt_tiles)
    def _():
        _wait_rows(TOP_K * n_sample, ys_hbm, ybuf.at[slot], sem.at[slot])

    def mix(group, carry):
        for u in range(LOOP_UNROLL):
            t = group * LOOP_UNROLL + u
            acc = gate_ref[0, 0, t] * _row_tile_at(ybuf.at[slot], lpos_ref[0, 0, t])[...]
            for k in range(1, TOP_K):
                acc = acc + (gate_ref[0, 0, k * tr + t]
                             * _row_tile_at(ybuf.at[slot], lpos_ref[0, 0, k * tr + t])[...])
            _row_tile(moe_tiles, t)[...] = acc
        return carry
    lax.fori_loop(0, n_valid // LOOP_UNROLL, mix, 0)

    @pl.when(i < n_prompt_tiles)
    def _():
        yp_ref[...] = _rmsnorm(hp_ref[...] + _load_tiles_as_rows(moe_tiles, 0, tr), gf_ref[...])

    @pl.when(i >= n_prompt_tiles)
    def _():
        ysm_ref[...] = _rmsnorm(hs_ref[...] + _load_tiles_as_rows(moe_tiles, 0, n_sample), gf_ref[...])


def _combine(lpos_tiles, gate_tiles, tab_tiles, h1p, h1s, ys, g_final):
    n_prompt_tiles = h1p.shape[0] // TOK_TILE
    n_tiles = lpos_tiles.shape[0]
    const = lambda a: pl.BlockSpec(a.shape, lambda i: (0,) * a.ndim)
    prompt_blk = pl.BlockSpec((TOK_TILE, D_MODEL), lambda i: (jnp.minimum(i, n_prompt_tiles - 1), 0))
    smem_blk = lambda a, ahead: pl.BlockSpec(
        (1, 1, a.shape[2]), lambda i: (jnp.minimum(i + ahead, n_tiles - 1), 0, 0), memory_space=pltpu.SMEM)
    return pl.pallas_call(
        functools.partial(_combine_kernel, n_prompt_tiles=n_prompt_tiles),
        grid=(n_tiles,),
        in_specs=[smem_blk(lpos_tiles, 0), smem_blk(gate_tiles, 0), smem_blk(tab_tiles, 0), smem_blk(tab_tiles, 1),
                  prompt_blk, const(h1s),
                  pl.BlockSpec(memory_space=pl.ANY),
                  const(g_final)],
        out_specs=[prompt_blk, const(h1s)],
        out_shape=[jax.ShapeDtypeStruct(h1p.shape, f32), jax.ShapeDtypeStruct(h1s.shape, f32)],
        scratch_shapes=[pltpu.VMEM((2, TOP_K * TOK_TILE * ROW_SUB, LANES), f32),
                        pltpu.VMEM((TOK_TILE * ROW_SUB, LANES), f32),
                        pltpu.SemaphoreType.DMA((2,))],
        compiler_params=pltpu.CompilerParams(
            dimension_semantics=("arbitrary",), vmem_limit_bytes=BIG_VMEM_LIMIT),
        name="combine",
    )(lpos_tiles, gate_tiles, tab_tiles, tab_tiles, h1p, h1s, ys, g_final)


def _block_diag_gate_weights(wa, wx):
    heads_per_half = N_LRU_HEADS // 2
    eye = jnp.eye(heads_per_half, dtype=wa.dtype)

    def bd(w):
        return jnp.einsum("hij,hg->higj", w, eye).reshape(heads_per_half * LRU_HEAD_DIM,
                                                          heads_per_half * LRU_HEAD_DIM)

    halves = [jnp.concatenate([bd(wa[s * heads_per_half:(s + 1) * heads_per_half]),
                               bd(wx[s * heads_per_half:(s + 1) * heads_per_half])], axis=1)
              for s in range(2)]
    return jnp.stack(halves).astype(bf16)


def kernel(x_prompt, x_sample, state_conv_short, state_conv_lru, state_lru_h, meta_tokens,
           norm_mix_g, w_in, conv_short_w, conv_lru_w, conv_lru_b, w_rg_a, b_rg_a, w_rg_x, b_rg_x,
           rg_lambda, w_out, norm_ffn_g, w_router, b_router, w_gate_up, b_gate_up, w_down, b_down,
           final_norm_g):
    assert norm_mix_g.shape[0] == 1, "single layer"
    bp, seq, _ = x_prompt.shape
    bs, seq_s, _ = x_sample.shape
    n_meta = meta_tokens.shape[0]
    row = lambda a: a.reshape(1, -1).astype(f32)

    mw = dict(g=row(norm_mix_g[0]), w_in=w_in[0].astype(bf16), csw=conv_short_w[0], clw=conv_lru_w[0],
              clb=row(conv_lru_b[0]), wg=_block_diag_gate_weights(w_rg_a[0], w_rg_x[0]),
              ba=row(b_rg_a[0]), bx=row(b_rg_x[0]), lam=row(rg_lambda[0]), w_out=w_out[0].astype(bf16))

    zeros = lambda *s: jnp.zeros(s, f32)
    _, cs_m, cl_m, h_m = _mixer(meta_tokens[None].astype(f32), zeros(1, 2, W_SHORT), zeros(1, 3, W_LRU),
                                zeros(1, 1, W_LRU), mw, n_meta, 1)
    rep = lambda a: jnp.broadcast_to(a, (bp,) + a.shape[1:])
    h1p, cs_p, cl_p, h_p = _mixer(x_prompt, rep(cs_m), rep(cl_m), rep(h_m), mw, MIX_ROWS, MIX_STREAMS)
    h1s, cs_s, cl_s, h_s = _mixer(x_sample, state_conv_short[0], state_conv_lru[0],
                                  state_lru_h[0][:, None, :], mw, seq_s, MIX_STREAMS)
    h1p = h1p.reshape(bp * seq, D_MODEL)
    h1s = h1s.reshape(bs * seq_s, D_MODEL)
    n_tok = h1p.shape[0] + h1s.shape[0]

    g_ffn = row(norm_ffn_g[0])
    lpos, gates, tab, counts = _router(
        h1p, h1s, g_ffn, w_router[0].T.astype(f32), b_router[0].reshape(-1, 1).astype(f32))
    n_tiles = tab.shape[0]

    counts = counts[:, 0]
    padded = (counts + EXP_ROWS - 1) // EXP_ROWS * EXP_ROWS
    pad_end = jnp.cumsum(padded)
    pad_start = pad_end - padded
    n_blocks = -(-(n_tok * TOP_K + N_EXPERTS * (EXP_ROWS - 1)) // EXP_ROWS)
    block_first_row = jnp.arange(n_blocks, dtype=jnp.int32) * EXP_ROWS
    block_e = jnp.minimum(jnp.sum(pad_end[None, :] <= block_first_row[:, None], axis=1),
                          N_EXPERTS - 1).astype(jnp.int32)
    n_used = (pad_end[-1:] // EXP_ROWS).astype(jnp.int32)
    n_rows = n_blocks * EXP_ROWS
    fill_start = jnp.concatenate([pad_start + counts, pad_end[-1:]]).astype(jnp.int32)
    fill_len = jnp.concatenate([padded - counts, n_rows - pad_end[-1:]]).astype(jnp.int32)

    tab_tiles = jnp.concatenate([pad_start[None, :] + tab[:, :, 0], tab[:, :, 1], tab[:, :, 2],
                                 jnp.zeros((n_tiles, LANES - 3 * N_EXPERTS), jnp.int32)],
                                axis=1).reshape(n_tiles, 1, LANES)
    by_tile = lambda a: a.reshape(TOP_K, n_tiles, TOK_TILE).transpose(1, 0, 2).reshape(
        n_tiles, 1, TOP_K * TOK_TILE)

    xs = _dispatch(fill_start, fill_len, by_tile(lpos), tab_tiles, h1p, h1s, g_ffn, n_rows)
    ys = _experts(block_e, n_used, xs,
                  w_gate_up[0].astype(f32), b_gate_up[0][:, None, :].astype(f32),
                  w_down[0].astype(f32), b_down[0][:, None, :].astype(f32))
    yp, ysm = _combine(by_tile(lpos), by_tile(gates), tab_tiles, h1p, h1s, ys, row(final_norm_g))

    st = lambda a: a[None]
    return (yp.reshape(bp, seq, D_MODEL), ysm.reshape(bs, seq_s, D_MODEL),
            st(cs_p), st(cl_p), st(h_p[:, 0, :]), st(cs_s), st(cl_s), st(h_s[:, 0, :]))
```
